```python
import jax, jax.numpy as jnp
from jax import lax
import numpy as np

D_MODEL = 1024
BATCH = 4
SEQ = 4096
DEPTH = 2

N_HEADS = 8
HEAD_DIM = 64
ATTN_WIDTH = N_HEADS * HEAD_DIM
DILATED_PATTERNS = ((128, 1), (512, 4), (2048, 16))
ATTN_BLOCK = 128
POOL_WINDOWS = (2, 4, 8, 16)
N_POOL_GROUPS = len(POOL_WINDOWS)
POOL_GROUP_DIM = 128
POOL_WIDTH = N_POOL_GROUPS * POOL_GROUP_DIM
N_BRANCHES = 2
IN_PROJ_WIDTH = 3 * ATTN_WIDTH + POOL_WIDTH + N_BRANCHES * D_MODEL
N_EXPERT_GROUPS = 4
EXPERTS_PER_GROUP = 8
N_EXPERTS = N_EXPERT_GROUPS * EXPERTS_PER_GROUP
TOP_K = 2
EXPERT_HIDDEN = 512
MOE_BLOCK = 128
RMS_EPS = 1e-6
NEG_INF = -1e30

kernel_name = "hybrid_dilated_attn_pool_hmoe"


def _rmsnorm(x, g):
    xf = x.astype(jnp.float32)
    y = xf * lax.rsqrt(jnp.mean(xf * xf, axis=-1, keepdims=True) + RMS_EPS)
    return (y * g.astype(jnp.float32)).astype(x.dtype)


def _dilated_branch(q, k, v, window, dilation):
    b_, h_, s_, e_ = q.shape
    steps = window // dilation
    span = dilation * ATTN_BLOCK
    s_pad = -(-s_ // span) * span
    n_strided = s_pad // dilation
    n_blk = n_strided // ATTN_BLOCK

    def to_blocks(t):
        t = jnp.pad(t, ((0, 0), (0, 0), (0, s_pad - s_), (0, 0)))
        t = t.reshape(b_, h_, n_strided, dilation, e_).transpose(0, 1, 3, 2, 4)
        return t.reshape(b_, h_, dilation, n_blk, ATTN_BLOCK, e_)

    def with_prev(t):
        prev = jnp.pad(t, ((0, 0), (0, 0), (0, 0), (1, 0), (0, 0), (0, 0)))[:, :, :, :-1]
        return jnp.concatenate([prev, t], axis=4)

    qb = to_blocks(q)
    kk = with_prev(to_blocks(k))
    vv = with_prev(to_blocks(v))
    s = jnp.einsum('bhcnqe,bhcnke->bhcnqk', qb, kk) * (HEAD_DIM ** -0.5)
    qi = jnp.arange(ATTN_BLOCK)[:, None]
    kj = jnp.arange(2 * ATTN_BLOCK)[None, :]
    dist = ATTN_BLOCK + qi - kj
    blk = jnp.arange(n_blk)[:, None, None]
    valid = (dist >= 0) & (dist <= steps) & (blk * ATTN_BLOCK - ATTN_BLOCK + kj >= 0)
    s = jnp.where(valid, s, NEG_INF)
    lse = jax.nn.logsumexp(s, axis=-1)
    p = jnp.exp(s - lse[..., None])
    o = jnp.einsum('bhcnqk,bhcnke->bhcnqe', p, vv)
    o = o.reshape(b_, h_, dilation, n_strided, e_).transpose(0, 1, 3, 2, 4).reshape(b_, h_, s_pad, e_)
    lse = lse.reshape(b_, h_, dilation, n_strided).transpose(0, 1, 3, 2).reshape(b_, h_, s_pad)
    return o[:, :, :s_], lse[:, :, :s_]


def _dilated_attention(q, k, v):
    outs, lses = [], []
    for window, dilation in DILATED_PATTERNS:
        o, l = _dilated_branch(q, k, v, window, dilation)
        outs.append(o)
        lses.append(l)
    wts = jax.nn.softmax(jnp.stack(lses, axis=0), axis=0)
    return jnp.einsum('gbhs,gbhse->bhse', wts, jnp.stack(outs, axis=0))


def _pool_mixer(u, w_mix, scale):
    b_, s_, _ = u.shape
    ug = u.reshape(b_, s_, N_POOL_GROUPS, POOL_GROUP_DIM).astype(jnp.float32)
    cs = jnp.cumsum(ug, axis=1)
    pos = jnp.arange(s_)
    outs = []
    for gi, w in enumerate(POOL_WINDOWS):
        c = cs[:, :, gi]
        lower = jnp.pad(c, ((0, 0), (w, 0), (0, 0)))[:, :s_]
        cnt = jnp.minimum(pos + 1, w).astype(jnp.float32)[None, :, None]
        outs.append((c - lower) / cnt - ug[:, :, gi])
    pooled = jnp.stack(outs, axis=2)
    mixed = jnp.einsum('bsgc,gcd->bsgd', pooled, w_mix.astype(jnp.float32))
    return (mixed.reshape(b_, s_, POOL_WIDTH) * scale.astype(jnp.float32)).astype(u.dtype)


def _hier_moe(h, w_rg, b_rg, w_re, b_re, w_gate, w_up, w_down):
    b_, s_, d_ = h.shape
    n_tok = b_ * s_
    hf = h.reshape(n_tok, d_)
    g_logits = jnp.dot(hf, w_rg).astype(jnp.float32) + b_rg.astype(jnp.float32)
    g_prob = jax.nn.softmax(g_logits, axis=-1)
    g_sel = jnp.argmax(g_logits, axis=-1).astype(jnp.int32)
    g_gate = jnp.take_along_axis(g_prob, g_sel[:, None], axis=1)[:, 0]
    e_logits_all = jnp.einsum('nd,gde->nge', hf, w_re).astype(jnp.float32) + b_re.astype(jnp.float32)
    e_logits = jnp.take_along_axis(e_logits_all, g_sel[:, None, None], axis=1)[:, 0]
    e_prob = jax.nn.softmax(e_logits, axis=-1)
    top_p, top_i = lax.top_k(e_prob, TOP_K)
    top_p = top_p / jnp.sum(top_p, axis=-1, keepdims=True)
    gate = g_gate[:, None] * top_p
    expert_id = g_sel[:, None] * EXPERTS_PER_GROUP + top_i.astype(jnp.int32)

    n_assign = n_tok * TOP_K
    e_flat = expert_id.reshape(n_assign)
    tok_flat = jnp.repeat(jnp.arange(n_tok, dtype=jnp.int32), TOP_K)
    w_flat = gate.reshape(n_assign)
    order = jnp.argsort(e_flat)
    e_s, tok_s, w_s = e_flat[order], tok_flat[order], w_flat[order]
    counts = jnp.bincount(e_flat, length=N_EXPERTS).astype(jnp.int32)
    start = jnp.cumsum(counts) - counts
    padded = (counts + MOE_BLOCK - 1) // MOE_BLOCK * MOE_BLOCK
    pend = jnp.cumsum(padded)
    pstart = pend - padded
    dest = pstart[e_s] + (jnp.arange(n_assign, dtype=jnp.int32) - start[e_s])
    buf_len = n_assign + N_EXPERTS * MOE_BLOCK
    n_blocks = buf_len // MOE_BLOCK
    buf_tok = jnp.full((buf_len,), n_tok, jnp.int32).at[dest].set(tok_s)
    buf_w = jnp.zeros((buf_len,), jnp.float32).at[dest].set(w_s)
    block_start = jnp.arange(n_blocks, dtype=jnp.int32) * MOE_BLOCK
    block_expert = jnp.minimum(jnp.searchsorted(pend, block_start, side='right'), N_EXPERTS - 1).astype(jnp.int32)
    x_rows = jnp.concatenate([hf, jnp.zeros((1, d_), hf.dtype)], axis=0)[buf_tok]
    x_rows = x_rows.reshape(n_blocks, MOE_BLOCK, d_)

    def expert_block(args):
        xb, e = args
        a = xb @ w_gate[e]
        bb = xb @ w_up[e]
        return (jax.nn.silu(a) * bb) @ w_down[e]

    y_rows = lax.map(expert_block, (x_rows, block_expert)).reshape(buf_len, d_)
    y = jnp.zeros((n_tok + 1, d_), jnp.float32).at[buf_tok].add(y_rows.astype(jnp.float32) * buf_w[:, None])
    return y[:n_tok].reshape(b_, s_, d_).astype(h.dtype)


def setup_inputs(seed: int = 0) -> dict:
    key = jax.random.key(seed)
    ks = jax.random.split(key, 18)
    f32 = jnp.float32
    nrm = lambda k, shape, scale: (jax.random.normal(k, shape, f32) * scale)
    return {
        "x": jax.random.normal(ks[0], (BATCH, SEQ, D_MODEL), f32),
        "norm1_g": 1.0 + nrm(ks[1], (DEPTH, D_MODEL), 0.02),
        "w_in": nrm(ks[2], (DEPTH, D_MODEL, IN_PROJ_WIDTH), D_MODEL ** -0.5),
        "q_norm_g": 1.0 + nrm(ks[3], (DEPTH, HEAD_DIM), 0.02),
        "k_norm_g": 1.0 + nrm(ks[4], (DEPTH, HEAD_DIM), 0.02),
        "w_attn_out": nrm(ks[5], (DEPTH, ATTN_WIDTH, D_MODEL), ATTN_WIDTH ** -0.5),
        "w_pool_mix": nrm(ks[6], (DEPTH, N_POOL_GROUPS, POOL_GROUP_DIM, POOL_GROUP_DIM), POOL_GROUP_DIM ** -0.5),
        "pool_scale": 1.0 + nrm(ks[7], (DEPTH, POOL_WIDTH), 0.1),
        "w_pool_out": nrm(ks[8], (DEPTH, POOL_WIDTH, D_MODEL), POOL_WIDTH ** -0.5),
        "w_o": nrm(ks[9], (DEPTH, D_MODEL, D_MODEL), D_MODEL ** -0.5),
        "norm2_g": 1.0 + nrm(ks[10], (DEPTH, D_MODEL), 0.02),
        "w_router_group": nrm(ks[11], (DEPTH, D_MODEL, N_EXPERT_GROUPS), D_MODEL ** -0.5),
        "b_router_group": nrm(ks[12], (DEPTH, N_EXPERT_GROUPS), 0.01),
        "w_router_expert": nrm(ks[13], (DEPTH, N_EXPERT_GROUPS, D_MODEL, EXPERTS_PER_GROUP), D_MODEL ** -0.5),
        "b_router_expert": nrm(ks[14], (DEPTH, N_EXPERT_GROUPS, EXPERTS_PER_GROUP), 0.01),
        "w_exp_gate": nrm(ks[15], (DEPTH, N_EXPERTS, D_MODEL, EXPERT_HIDDEN), D_MODEL ** -0.5),
        "w_exp_up": nrm(ks[16], (DEPTH, N_EXPERTS, D_MODEL, EXPERT_HIDDEN), D_MODEL ** -0.5),
        "w_exp_down": nrm(ks[17], (DEPTH, N_EXPERTS, EXPERT_HIDDEN, D_MODEL), EXPERT_HIDDEN ** -0.5),
    }


def reference(x, norm1_g, w_in, q_norm_g, k_norm_g, w_attn_out, w_pool_mix, pool_scale, w_pool_out,
              w_o, norm2_g, w_router_group, b_router_group, w_router_expert, b_router_expert,
              w_exp_gate, w_exp_up, w_exp_down):
    b_, s_, d_ = x.shape
    splits = [ATTN_WIDTH, 2 * ATTN_WIDTH, 3 * ATTN_WIDTH, 3 * ATTN_WIDTH + POOL_WIDTH,
              3 * ATTN_WIDTH + POOL_WIDTH + D_MODEL]
    for layer in range(DEPTH):
        h = _rmsnorm(x, norm1_g[layer])
        proj = h @ w_in[layer]
        q, k, v, u, gate_a, gate_p = jnp.split(proj, splits, axis=-1)
        q = _rmsnorm(q.reshape(b_, s_, N_HEADS, HEAD_DIM), q_norm_g[layer])
        k = _rmsnorm(k.reshape(b_, s_, N_HEADS, HEAD_DIM), k_norm_g[layer])
        v = v.reshape(b_, s_, N_HEADS, HEAD_DIM)
        q, k, v = (t.transpose(0, 2, 1, 3).astype(jnp.float32) for t in (q, k, v))
        attn = _dilated_attention(q, k, v).transpose(0, 2, 1, 3).reshape(b_, s_, ATTN_WIDTH).astype(x.dtype)
        y_a = attn @ w_attn_out[layer]
        y_p = _pool_mixer(u, w_pool_mix[layer], pool_scale[layer]) @ w_pool_out[layer]
        merged = jax.nn.sigmoid(gate_a) * y_a + jax.nn.sigmoid(gate_p) * y_p
        x = x + merged @ w_o[layer]
        h2 = _rmsnorm(x, norm2_g[layer])
        x = x + _hier_moe(h2, w_router_group[layer], b_router_group[layer], w_router_expert[layer],
                          b_router_expert[layer], w_exp_gate[layer], w_exp_up[layer], w_exp_down[layer])
    return x
```

```python
import functools

import jax
import jax.numpy as jnp
from jax import lax
from jax.experimental import pallas as pl
from jax.experimental.pallas import tpu as pltpu

D_MODEL = 1024
SEQ = 4096
N_HEADS = 8
HEAD_DIM = 64
ATTN_WIDTH = N_HEADS * HEAD_DIM
DILATIONS = (1, 4, 16)
ATTN_BLOCK = 128
POOL_WINDOWS = (2, 4, 8, 16)
POOL_GROUP_DIM = 128
POOL_WIDTH = len(POOL_WINDOWS) * POOL_GROUP_DIM
POOL_HALO = 16
IN_PROJ_WIDTH = 3 * ATTN_WIDTH + POOL_WIDTH + 2 * D_MODEL
N_EXPERT_GROUPS = 4
EXPERTS_PER_GROUP = 8
N_EXPERTS = N_EXPERT_GROUPS * EXPERTS_PER_GROUP
TOP_K = 2
EXPERT_HIDDEN = 512
RMS_EPS = 1e-6
NEG_INF = -1e30

LANES = 128
ROW_TILE = 256
EXPERT_BLOCK = 256
VMEM_LIMIT = 48 * 1024 * 1024

F32 = jnp.float32
BF16 = jnp.bfloat16


def _params(n_axes):
    return pltpu.CompilerParams(
        dimension_semantics=("arbitrary",) * n_axes, vmem_limit_bytes=VMEM_LIMIT)


def _inproj_kernel(x_ref, g_ref, w_ref, qg_ref, kg_ref, hsum_ref,
                   q_ref, k_ref, v_ref, u_ref, gate_ref):
    x = x_ref[...]
    ms = jnp.mean(x * x, axis=-1, keepdims=True)
    h = (x * lax.rsqrt(ms + RMS_EPS) * g_ref[...]).astype(BF16)

    def proj(lo, hi):
        return jnp.dot(h, w_ref[:, lo:hi], preferred_element_type=F32)

    def head_norm(t, gain):
        sq = (t * t).astype(BF16)
        half = ATTN_WIDTH // 2
        ssq = jnp.concatenate(
            [jnp.dot(sq[:, j * half:(j + 1) * half], hsum_ref[...], preferred_element_type=F32)
             for j in range(2)], axis=-1)
        return t * lax.rsqrt(ssq * (1.0 / HEAD_DIM) + RMS_EPS) * gain

    w = ATTN_WIDTH
    q_ref[...] = head_norm(proj(0, w), qg_ref[...])
    k_ref[...] = head_norm(proj(w, 2 * w), kg_ref[...])
    v_ref[...] = proj(2 * w, 3 * w)
    u_ref[...] = proj(3 * w, 3 * w + POOL_WIDTH)
    base = 3 * w + POOL_WIDTH
    for j in range(2 * D_MODEL // 512):
        gate_ref[:, j * 512:(j + 1) * 512] = jax.nn.sigmoid(
            proj(base + j * 512, base + (j + 1) * 512)).astype(BF16)


def _inproj(x, g1, w_in, qg, kg, hsum):
    n = x.shape[0]
    row = lambda i: (i, 0)
    const = lambda i: (0, 0)
    return pl.pallas_call(
        _inproj_kernel,
        grid=(n // ROW_TILE,),
        in_specs=[
            pl.BlockSpec((ROW_TILE, D_MODEL), row),
            pl.BlockSpec((1, D_MODEL), const),
            pl.BlockSpec((D_MODEL, IN_PROJ_WIDTH), const),
            pl.BlockSpec((1, ATTN_WIDTH), const),
            pl.BlockSpec((1, ATTN_WIDTH), const),
            pl.BlockSpec((ATTN_WIDTH // 2, ATTN_WIDTH // 2), const),
        ],
        out_specs=[
            pl.BlockSpec((ROW_TILE, ATTN_WIDTH), row),
            pl.BlockSpec((ROW_TILE, ATTN_WIDTH), row),
            pl.BlockSpec((ROW_TILE, ATTN_WIDTH), row),
            pl.BlockSpec((ROW_TILE, POOL_WIDTH), row),
            pl.BlockSpec((ROW_TILE, 2 * D_MODEL), row),
        ],
        out_shape=[
            jax.ShapeDtypeStruct((n, ATTN_WIDTH), F32),
            jax.ShapeDtypeStruct((n, ATTN_WIDTH), F32),
            jax.ShapeDtypeStruct((n, ATTN_WIDTH), F32),
            jax.ShapeDtypeStruct((n, POOL_WIDTH), F32),
            jax.ShapeDtypeStruct((n, 2 * D_MODEL), BF16),
        ],
        compiler_params=_params(1),
        name="inproj",
    )(x, g1, w_in, qg, kg, hsum)


def _attn_kernel(q_ref, k_ref, v_ref, out_ref, o0, o1, o2, l0, l1, l2):
    blk = ATTN_BLOCK
    o_scr = (o0, o1, o2)
    l_scr = (l0, l1, l2)
    head_a = lax.broadcasted_iota(jnp.int32, (blk, LANES), 1) < HEAD_DIM
    qi = lax.broadcasted_iota(jnp.int32, (2 * blk, 2 * blk), 0) & (blk - 1)
    kj = lax.broadcasted_iota(jnp.int32, (2 * blk, 2 * blk), 1)
    q_minus_k = qi - kj
    ones_cols = jnp.ones((2 * blk, LANES), BF16)

    for g, d in enumerate(DILATIONS):
        n_blk = SEQ // (blk * d)
        log_n_blk = n_blk.bit_length() - 1

        def rows(start, size, d=d):
            return pl.ds(start, size) if d == 1 else pl.ds(start, size, stride=d)

        def body(idx, carry, d=d, g=g, n_blk=n_blk, log_n_blk=log_n_blk, rows=rows):
            c = lax.shift_right_logical(idx, log_n_blk)
            nb = idx & (n_blk - 1)
            kb = jnp.maximum(nb - 1, 0)
            q_start = nb * (blk * d) + c
            k_start = kb * (blk * d) + c
            off = (nb - kb) * blk

            q2 = q_ref[rows(q_start, blk), :] * (HEAD_DIM ** -0.5)
            k2 = k_ref[rows(k_start, 2 * blk), :].astype(BF16)
            v2 = v_ref[rows(k_start, 2 * blk), :].astype(BF16)
            qs = jnp.concatenate(
                [jnp.where(head_a, q2, 0.0), jnp.where(head_a, 0.0, q2)], axis=0).astype(BF16)
            s = lax.dot_general(qs, k2, (((1,), (1,)), ((), ())), preferred_element_type=F32)
            dist = q_minus_k + off
            s = jnp.where((dist >= 0) & (dist <= blk), s, NEG_INF)
            m = jnp.max(s, axis=-1, keepdims=True)
            e = jnp.exp(s - m).astype(BF16)
            r = jnp.dot(e, jnp.concatenate([v2, ones_cols], axis=1), preferred_element_type=F32)
            denom = r[:, LANES:]
            o = r[:, :LANES] / denom
            lse = m + jnp.log(denom)
            o_scr[g][rows(q_start, blk), :] = jnp.where(head_a, o[:blk], o[blk:])
            l_scr[g][rows(q_start, blk), :] = jnp.where(head_a, lse[:blk], lse[blk:])
            return carry

        lax.fori_loop(0, SEQ // blk, body, 0)

    chunk = 512

    def mix(i, carry):
        r = pl.ds(pl.multiple_of(i * chunk, chunk), chunk)
        la, lb, lc = l0[r, :], l1[r, :], l2[r, :]
        m = jnp.maximum(jnp.maximum(la, lb), lc)
        wa, wb, wc = jnp.exp(la - m), jnp.exp(lb - m), jnp.exp(lc - m)
        acc = wa * o0[r, :] + wb * o1[r, :] + wc * o2[r, :]
        out_ref[r, :] = (acc / (wa + wb + wc)).astype(out_ref.dtype)
        return carry

    lax.fori_loop(0, SEQ // chunk, mix, 0)


def _attn(q, k, v, batch):
    spec = pl.BlockSpec((None, SEQ, LANES), lambda b, hp: (b, 0, hp))
    return pl.pallas_call(
        _attn_kernel,
        grid=(batch, ATTN_WIDTH // LANES),
        in_specs=[spec, spec, spec],
        out_specs=spec,
        out_shape=jax.ShapeDtypeStruct((batch, SEQ, ATTN_WIDTH), BF16),
        scratch_shapes=[pltpu.VMEM((SEQ, LANES), F32) for _ in range(6)],
        compiler_params=_params(2),
        name="dilated_attn",
    )(q, k, v)


def _post_kernel(x_ref, attn_ref, u_ref, halo_ref, gate_ref, wao_ref, wmix_ref, pscale_ref,
                 wpo_ref, wo_ref, g2_ref, wr_ref, br_ref,
                 xo_ref, h2_ref, ri_ref, rf_ref):
    tm = x_ref.shape[0]
    pos0 = lax.rem(pl.program_id(0) * tm, SEQ)
    pos = pos0 + lax.broadcasted_iota(jnp.int32, (tm, 1), 0)
    u = u_ref[...]
    halo = halo_ref[...] * (pos0 > 0).astype(F32)

    mixed = []
    for gi, w in enumerate(POOL_WINDOWS):
        lo = gi * POOL_GROUP_DIM
        ug = u[:, lo:lo + POOL_GROUP_DIM]
        ext = jnp.concatenate([halo[:, lo:lo + POOL_GROUP_DIM], ug], axis=0)
        shift = 1
        while shift < w:
            ext = ext + pltpu.roll(ext, shift, 0)
            shift *= 2
        cnt = jnp.minimum(pos + 1, w).astype(F32)
        pooled = ext[POOL_HALO:] / cnt - ug
        mixed.append(jnp.dot(pooled.astype(BF16), wmix_ref[gi], preferred_element_type=F32))
    pool_out = (jnp.concatenate(mixed, axis=-1) * pscale_ref[...]).astype(BF16)

    y_a = jnp.dot(attn_ref[...], wao_ref[...], preferred_element_type=F32)
    y_p = jnp.dot(pool_out, wpo_ref[...], preferred_element_type=F32)
    gates = gate_ref[...]
    merged = gates[:, :D_MODEL].astype(F32) * y_a + gates[:, D_MODEL:].astype(F32) * y_p
    x_new = x_ref[...] + jnp.dot(merged.astype(BF16), wo_ref[...], preferred_element_type=F32)
    xo_ref[...] = x_new

    ms = jnp.mean(x_new * x_new, axis=-1, keepdims=True)
    h2 = x_new * lax.rsqrt(ms + RMS_EPS) * g2_ref[...]
    h2_ref[...] = h2.astype(BF16)

    logits = jnp.dot(h2, wr_ref[...], preferred_element_type=F32,
                     precision=lax.Precision.HIGHEST) + br_ref[...]
    lane = lax.broadcasted_iota(jnp.int32, (tm, LANES), 1)
    big = jnp.int32(LANES)

    def first_max(vals, mask):
        vmax = jnp.max(jnp.where(mask, vals, -jnp.inf), axis=-1, keepdims=True)
        idx = jnp.min(jnp.where(mask & (vals == vmax), lane, big), axis=-1, keepdims=True)
        return vmax, idx

    gmask = lane < N_EXPERT_GROUPS
    gmax, g_sel = first_max(logits, gmask)
    g_gate = 1.0 / jnp.sum(jnp.where(gmask, jnp.exp(logits - gmax), 0.0), axis=-1, keepdims=True)

    e_lo = N_EXPERT_GROUPS + g_sel * EXPERTS_PER_GROUP
    emask = (lane >= e_lo) & (lane < e_lo + EXPERTS_PER_GROUP)
    emax, _ = first_max(logits, emask)
    e_exp = jnp.where(emask, jnp.exp(logits - emax), 0.0)
    prob = e_exp / jnp.sum(e_exp, axis=-1, keepdims=True)
    p1, i1 = first_max(prob, emask)
    p2, i2 = first_max(prob, emask & (lane != i1))
    scale = g_gate / (p1 + p2)
    ri_ref[...] = jnp.where(lane == 0, i1 - N_EXPERT_GROUPS,
                            jnp.where(lane == 1, i2 - N_EXPERT_GROUPS, 0))
    rf_ref[...] = jnp.where(lane == 0, p1 * scale, jnp.where(lane == 1, p2 * scale, 0.0))


def _post(x, attn, u, gates, wao, wmix, pscale, wpo, wo, g2, wr, br):
    n = x.shape[0]
    tm = ROW_TILE
    row = lambda i: (i, 0)
    const = lambda i: (0, 0)
    halo_blocks = tm // POOL_HALO
    return pl.pallas_call(
        _post_kernel,
        grid=(n // tm,),
        in_specs=[
            pl.BlockSpec((tm, D_MODEL), row),
            pl.BlockSpec((tm, ATTN_WIDTH), row),
            pl.BlockSpec((tm, POOL_WIDTH), row),
            pl.BlockSpec((POOL_HALO, POOL_WIDTH),
                         lambda i: (jnp.maximum(i * halo_blocks - 1, 0), 0)),
            pl.BlockSpec((tm, 2 * D_MODEL), row),
            pl.BlockSpec((ATTN_WIDTH, D_MODEL), const),
            pl.BlockSpec((len(POOL_WINDOWS), POOL_GROUP_DIM, POOL_GROUP_DIM), lambda i: (0, 0, 0)),
            pl.BlockSpec((1, POOL_WIDTH), const),
            pl.BlockSpec((POOL_WIDTH, D_MODEL), const),
            pl.BlockSpec((D_MODEL, D_MODEL), const),
            pl.BlockSpec((1, D_MODEL), const),
            pl.BlockSpec((D_MODEL, LANES), const),
            pl.BlockSpec((1, LANES), const),
        ],
        out_specs=[
            pl.BlockSpec((tm, D_MODEL), row),
            pl.BlockSpec((tm, D_MODEL), row),
            pl.BlockSpec((tm, LANES), row),
            pl.BlockSpec((tm, LANES), row),
        ],
        out_shape=[
            jax.ShapeDtypeStruct((n, D_MODEL), F32),
            jax.ShapeDtypeStruct((n, D_MODEL), BF16),
            jax.ShapeDtypeStruct((n, LANES), jnp.int32),
            jax.ShapeDtypeStruct((n, LANES), F32),
        ],
        compiler_params=_params(1),
        name="post_attn_router",
    )(x, attn, u, u, gates, wao, wmix, pscale, wpo, wo, g2, wr, br)


def _experts_kernel(be_ref, x_ref, wg_ref, wu_ref, wd_ref, y_ref, wg_bf, wu_bf, wd_bf):
    i = pl.program_id(0)
    changed = (i == 0) | (be_ref[i] != be_ref[jnp.maximum(i - 1, 0)])

    @pl.when(changed)
    def _():
        wg_bf[...] = wg_ref[...].astype(BF16)
        wu_bf[...] = wu_ref[...].astype(BF16)
        wd_bf[...] = wd_ref[...].astype(BF16)

    x = x_ref[...]
    a = jnp.dot(x, wg_bf[...], preferred_element_type=F32)
    b = jnp.dot(x, wu_bf[...], preferred_element_type=F32)
    mid = (a * jax.nn.sigmoid(a) * b).astype(BF16)
    y_ref[...] = jnp.dot(mid, wd_bf[...], preferred_element_type=F32)


def _experts(block_expert, x_rows, w_gate, w_up, w_down):
    n_rows = x_rows.shape[0]
    bm = EXPERT_BLOCK
    grid_spec = pltpu.PrefetchScalarGridSpec(
        num_scalar_prefetch=1,
        grid=(n_rows // bm,),
        in_specs=[
            pl.BlockSpec((bm, D_MODEL), lambda i, be: (i, 0)),
            pl.BlockSpec((None, D_MODEL, EXPERT_HIDDEN), lambda i, be: (be[i], 0, 0)),
            pl.BlockSpec((None, D_MODEL, EXPERT_HIDDEN), lambda i, be: (be[i], 0, 0)),
            pl.BlockSpec((None, EXPERT_HIDDEN, D_MODEL), lambda i, be: (be[i], 0, 0)),
        ],
        out_specs=pl.BlockSpec((bm, D_MODEL), lambda i, be: (i, 0)),
        scratch_shapes=[
            pltpu.VMEM((D_MODEL, EXPERT_HIDDEN), BF16),
            pltpu.VMEM((D_MODEL, EXPERT_HIDDEN), BF16),
            pltpu.VMEM((EXPERT_HIDDEN, D_MODEL), BF16),
        ],
    )
    return pl.pallas_call(
        _experts_kernel,
        grid_spec=grid_spec,
        out_shape=jax.ShapeDtypeStruct((n_rows, D_MODEL), F32),
        compiler_params=_params(1),
        name="experts",
    )(block_expert, x_rows, w_gate, w_up, w_down)


def _dispatch_plan(expert_id):
    bm = EXPERT_BLOCK
    e_flat = expert_id.reshape(-1)
    n_assign = e_flat.shape[0]
    onehot = (e_flat[:, None] == jnp.arange(N_EXPERTS, dtype=jnp.int32)[None, :]).astype(jnp.int32)
    csum = jnp.cumsum(onehot, axis=0)
    rank = jnp.sum(csum * onehot, axis=1) - 1
    counts = csum[-1]
    padded = (counts + bm - 1) // bm * bm
    pend = jnp.cumsum(padded)
    pstart = pend - padded
    dest = pstart[e_flat] + rank
    buf_len = n_assign + N_EXPERTS * bm
    block_start = jnp.arange(buf_len // bm, dtype=jnp.int32) * bm
    block_expert = jnp.minimum(jnp.searchsorted(pend, block_start, side="right"),
                               N_EXPERTS - 1).astype(jnp.int32)
    return dest.astype(jnp.int32), block_expert, buf_len


def kernel(x, norm1_g, w_in, q_norm_g, k_norm_g, w_attn_out, w_pool_mix, pool_scale, w_pool_out,
           w_o, norm2_g, w_router_group, b_router_group, w_router_expert, b_router_expert,
           w_exp_gate, w_exp_up, w_exp_down):
    batch, seq, d = x.shape
    assert (seq, d) == (SEQ, D_MODEL)
    n_tok = batch * seq
    depth = w_in.shape[0]
    half = ATTN_WIDTH // 2
    hsum = (jnp.arange(half)[:, None] // HEAD_DIM == jnp.arange(half)[None, :] // HEAD_DIM).astype(BF16)

    xf = x.reshape(n_tok, d)
    for layer in range(depth):
        qg = jnp.tile(q_norm_g[layer], N_HEADS)[None, :]
        kg = jnp.tile(k_norm_g[layer], N_HEADS)[None, :]
        q, k, v, u, gates = _inproj(xf, norm1_g[layer][None, :], w_in[layer].astype(BF16), qg, kg, hsum)
        attn = _attn(q.reshape(batch, seq, ATTN_WIDTH), k.reshape(batch, seq, ATTN_WIDTH),
                     v.reshape(batch, seq, ATTN_WIDTH), batch).reshape(n_tok, ATTN_WIDTH)

        w_router = jnp.concatenate(
            [w_router_group[layer],
             w_router_expert[layer].transpose(1, 0, 2).reshape(d, N_EXPERTS)], axis=1)
        w_router = jnp.pad(w_router, ((0, 0), (0, LANES - w_router.shape[1])))
        b_router = jnp.concatenate([b_router_group[layer], b_router_expert[layer].reshape(-1)])
        b_router = jnp.pad(b_router, (0, LANES - b_router.shape[0]))[None, :]
        x_new, h2, route_i, route_f = _post(
            xf, attn, u, gates, w_attn_out[layer].astype(BF16), w_pool_mix[layer].astype(BF16),
            pool_scale[layer][None, :], w_pool_out[layer].astype(BF16), w_o[layer].astype(BF16),
            norm2_g[layer][None, :], w_router, b_router)

        expert_id = route_i[:, :TOP_K]
        gate = route_f[:, :TOP_K]
        dest, block_expert, buf_len = _dispatch_plan(expert_id)
        tok = jnp.repeat(jnp.arange(n_tok, dtype=jnp.int32), TOP_K)
        buf_tok = jnp.full((buf_len,), n_tok, jnp.int32).at[dest].set(tok)
        x_rows = jnp.concatenate([h2, jnp.zeros((1, d), h2.dtype)], axis=0)[buf_tok]
        y_rows = _experts(block_expert, x_rows, w_exp_gate[layer], w_exp_up[layer], w_exp_down[layer])
        y = jnp.sum(y_rows[dest].reshape(n_tok, TOP_K, d) * gate[:, :, None], axis=1)
        xf = x_new + y
    return xf.reshape(batch, seq, d)
```

```python
import functools

import jax
import jax.numpy as jnp
from jax import lax
from jax.experimental import pallas as pl
from jax.experimental.pallas import tpu as pltpu

D_MODEL = 1024
SEQ = 4096
N_HEADS = 8
HEAD_DIM = 64
ATTN_WIDTH = N_HEADS * HEAD_DIM
DILATIONS = (1, 4, 16)
ATTN_BLOCK = 128
POOL_WINDOWS = (2, 4, 8, 16)
POOL_GROUP_DIM = 128
POOL_WIDTH = len(POOL_WINDOWS) * POOL_GROUP_DIM
POOL_HALO = 16
IN_PROJ_WIDTH = 3 * ATTN_WIDTH + POOL_WIDTH + 2 * D_MODEL
N_EXPERT_GROUPS = 4
EXPERTS_PER_GROUP = 8
N_EXPERTS = N_EXPERT_GROUPS * EXPERTS_PER_GROUP
TOP_K = 2
EXPERT_HIDDEN = 512
RMS_EPS = 1e-6
NEG_INF = -1e30

LANES = 128
ROW_TILE = 256
EXPERT_BLOCK = 256
VMEM_LIMIT = 48 * 1024 * 1024

F32 = jnp.float32
BF16 = jnp.bfloat16


def _params(n_axes):
    return pltpu.CompilerParams(
        dimension_semantics=("arbitrary",) * n_axes, vmem_limit_bytes=VMEM_LIMIT)


def _inproj_kernel(x_ref, g_ref, w_ref, qg_ref, kg_ref, hsum_ref,
                   q_ref, k_ref, v_ref, u_ref, gate_ref):
    x = x_ref[...]
    ms = jnp.mean(x * x, axis=-1, keepdims=True)
    h = (x * lax.rsqrt(ms + RMS_EPS) * g_ref[...]).astype(BF16)

    def proj(lo, hi):
        return jnp.dot(h, w_ref[:, lo:hi], preferred_element_type=F32)

    def head_norm(t, gain):
        sq = (t * t).astype(BF16)
        half = ATTN_WIDTH // 2
        ssq = jnp.concatenate(
            [jnp.dot(sq[:, j * half:(j + 1) * half], hsum_ref[...], preferred_element_type=F32)
             for j in range(2)], axis=-1)
        return t * lax.rsqrt(ssq * (1.0 / HEAD_DIM) + RMS_EPS) * gain

    w = ATTN_WIDTH
    q_ref[...] = head_norm(proj(0, w), qg_ref[...])
    k_ref[...] = head_norm(proj(w, 2 * w), kg_ref[...])
    v_ref[...] = proj(2 * w, 3 * w)
    u_ref[...] = proj(3 * w, 3 * w + POOL_WIDTH)
    base = 3 * w + POOL_WIDTH
    for j in range(2 * D_MODEL // 512):
        gate_ref[:, j * 512:(j + 1) * 512] = jax.nn.sigmoid(
            proj(base + j * 512, base + (j + 1) * 512)).astype(BF16)


def _inproj(x, g1, w_in, qg, kg, hsum):
    n = x.shape[0]
    row = lambda i: (i, 0)
    const = lambda i: (0, 0)
    return pl.pallas_call(
        _inproj_kernel,
        grid=(n // ROW_TILE,),
        in_specs=[
            pl.BlockSpec((ROW_TILE, D_MODEL), row),
            pl.BlockSpec((1, D_MODEL), const),
            pl.BlockSpec((D_MODEL, IN_PROJ_WIDTH), const),
            pl.BlockSpec((1, ATTN_WIDTH), const),
            pl.BlockSpec((1, ATTN_WIDTH), const),
            pl.BlockSpec((ATTN_WIDTH // 2, ATTN_WIDTH // 2), const),
        ],
        out_specs=[
            pl.BlockSpec((ROW_TILE, ATTN_WIDTH), row),
            pl.BlockSpec((ROW_TILE, ATTN_WIDTH), row),
            pl.BlockSpec((ROW_TILE, ATTN_WIDTH), row),
            pl.BlockSpec((ROW_TILE, POOL_WIDTH), row),
            pl.BlockSpec((ROW_TILE, 2 * D_MODEL), row),
        ],
        out_shape=[
            jax.ShapeDtypeStruct((n, ATTN_WIDTH), F32),
            jax.ShapeDtypeStruct((n, ATTN_WIDTH), F32),
            jax.ShapeDtypeStruct((n, ATTN_WIDTH), F32),
            jax.ShapeDtypeStruct((n, POOL_WIDTH), F32),
            jax.ShapeDtypeStruct((n, 2 * D_MODEL), BF16),
        ],
        compiler_params=_params(1),
        name="inproj",
    )(x, g1, w_in, qg, kg, hsum)


def _attn_kernel(q_ref, k_ref, v_ref, out_ref, o0, o1, o2, l0, l1, l2):
    blk = ATTN_BLOCK
    o_scr = (o0, o1, o2)
    l_scr = (l0, l1, l2)
    head_a = lax.broadcasted_iota(jnp.int32, (blk, LANES), 1) < HEAD_DIM
    qi = lax.broadcasted_iota(jnp.int32, (2 * blk, 2 * blk), 0) & (blk - 1)
    kj = lax.broadcasted_iota(jnp.int32, (2 * blk, 2 * blk), 1)
    q_minus_k = qi - kj
    ones_cols = jnp.ones((2 * blk, LANES), BF16)

    for g, d in enumerate(DILATIONS):
        n_blk = SEQ // (blk * d)
        log_n_blk = n_blk.bit_length() - 1

        def rows(start, size, d=d):
            return pl.ds(start, size) if d == 1 else pl.ds(start, size, stride=d)

        def body(idx, carry, d=d, g=g, n_blk=n_blk, log_n_blk=log_n_blk, rows=rows):
            c = lax.shift_right_logical(idx, log_n_blk)
            nb = idx & (n_blk - 1)
            kb = jnp.maximum(nb - 1, 0)
            q_start = nb * (blk * d) + c
            k_start = kb * (blk * d) + c
            off = (nb - kb) * blk

            q2 = q_ref[rows(q_start, blk), :] * (HEAD_DIM ** -0.5)
            k2 = k_ref[rows(k_start, 2 * blk), :].astype(BF16)
            v2 = v_ref[rows(k_start, 2 * blk), :].astype(BF16)
            qs = jnp.concatenate(
                [jnp.where(head_a, q2, 0.0), jnp.where(head_a, 0.0, q2)], axis=0).astype(BF16)
            s = lax.dot_general(qs, k2, (((1,), (1,)), ((), ())), preferred_element_type=F32)
            dist = q_minus_k + off
            s = jnp.where((dist >= 0) & (dist <= blk), s, NEG_INF)
            m = jnp.max(s, axis=-1, keepdims=True)
            e = jnp.exp(s - m).astype(BF16)
            r = jnp.dot(e, jnp.concatenate([v2, ones_cols], axis=1), preferred_element_type=F32)
            denom = r[:, LANES:]
            o = r[:, :LANES] / denom
            lse = m + jnp.log(denom)
            o_scr[g][rows(q_start, blk), :] = jnp.where(head_a, o[:blk], o[blk:])
            l_scr[g][rows(q_start, blk), :] = jnp.where(head_a, lse[:blk], lse[blk:])
            return carry

        lax.fori_loop(0, SEQ // blk, body, 0)

    chunk = 512

    def mix(i, carry):
        r = pl.ds(pl.multiple_of(i * chunk, chunk), chunk)
        la, lb, lc = l0[r, :], l1[r, :], l2[r, :]
        m = jnp.maximum(jnp.maximum(la, lb), lc)
        wa, wb, wc = jnp.exp(la - m), jnp.exp(lb - m), jnp.exp(lc - m)
        acc = wa * o0[r, :] + wb * o1[r, :] + wc * o2[r, :]
        out_ref[r, :] = (acc / (wa + wb + wc)).astype(out_ref.dtype)
        return carry

    lax.fori_loop(0, SEQ // chunk, mix, 0)


def _attn(q, k, v, batch):
    spec = pl.BlockSpec((None, SEQ, LANES), lambda b, hp: (b, 0, hp))
    return pl.pallas_call(
        _attn_kernel,
        grid=(batch, ATTN_WIDTH // LANES),
        in_specs=[spec, spec, spec],
        out_specs=spec,
        out_shape=jax.ShapeDtypeStruct((batch, SEQ, ATTN_WIDTH), BF16),
        scratch_shapes=[pltpu.VMEM((SEQ, LANES), F32) for _ in range(6)],
        compiler_params=_params(2),
        name="dilated_attn",
    )(q, k, v)


def _pack_bf16_pairs(x):
    c = x.shape[1] // 2
    bits = pltpu.bitcast(x.astype(BF16).astype(F32), jnp.uint32)
    return lax.shift_right_logical(bits[:, :c], jnp.uint32(16)) | bits[:, c:]


def _unpack_bf16_pairs(w):
    lo = pltpu.bitcast(lax.shift_left(w, jnp.uint32(16)), F32)
    hi = pltpu.bitcast(w & jnp.uint32(0xFFFF0000), F32)
    return jnp.concatenate([lo, hi], axis=1).astype(BF16)


def _post_kernel(x_ref, attn_ref, u_ref, halo_ref, gate_ref, wao_ref, wmix_ref, pscale_ref,
                 wpo_ref, wo_ref, g2_ref, wr_ref, br_ref,
                 xo_ref, h2_ref, ri_ref, rf_ref, cnt_ref):
    tm = x_ref.shape[0]

    @pl.when(pl.program_id(0) == 0)
    def _():
        cnt_ref[...] = jnp.zeros_like(cnt_ref)

    pos0 = lax.rem(pl.program_id(0) * tm, SEQ)
    pos = pos0 + lax.broadcasted_iota(jnp.int32, (tm, 1), 0)
    u = u_ref[...]
    halo = halo_ref[...] * (pos0 > 0).astype(F32)

    mixed = []
    for gi, w in enumerate(POOL_WINDOWS):
        lo = gi * POOL_GROUP_DIM
        ug = u[:, lo:lo + POOL_GROUP_DIM]
        ext = jnp.concatenate([halo[:, lo:lo + POOL_GROUP_DIM], ug], axis=0)
        shift = 1
        while shift < w:
            ext = ext + pltpu.roll(ext, shift, 0)
            shift *= 2
        cnt = jnp.minimum(pos + 1, w).astype(F32)
        pooled = ext[POOL_HALO:] / cnt - ug
        mixed.append(jnp.dot(pooled.astype(BF16), wmix_ref[gi], preferred_element_type=F32))
    pool_out = (jnp.concatenate(mixed, axis=-1) * pscale_ref[...]).astype(BF16)

    y_a = jnp.dot(attn_ref[...], wao_ref[...], preferred_element_type=F32)
    y_p = jnp.dot(pool_out, wpo_ref[...], preferred_element_type=F32)
    gates = gate_ref[...]
    merged = gates[:, :D_MODEL].astype(F32) * y_a + gates[:, D_MODEL:].astype(F32) * y_p
    x_new = x_ref[...] + jnp.dot(merged.astype(BF16), wo_ref[...], preferred_element_type=F32)
    xo_ref[...] = x_new

    ms = jnp.mean(x_new * x_new, axis=-1, keepdims=True)
    h2 = x_new * lax.rsqrt(ms + RMS_EPS) * g2_ref[...]
    h2_ref[...] = _pack_bf16_pairs(h2)

    logits = jnp.dot(h2, wr_ref[...], preferred_element_type=F32,
                     precision=lax.Precision.HIGHEST) + br_ref[...]
    lane = lax.broadcasted_iota(jnp.int32, (tm, LANES), 1)
    big = jnp.int32(LANES)

    def first_max(vals, mask):
        vmax = jnp.max(jnp.where(mask, vals, -jnp.inf), axis=-1, keepdims=True)
        idx = jnp.min(jnp.where(mask & (vals == vmax), lane, big), axis=-1, keepdims=True)
        return vmax, idx

    gmask = lane < N_EXPERT_GROUPS
    gmax, g_sel = first_max(logits, gmask)
    g_gate = 1.0 / jnp.sum(jnp.where(gmask, jnp.exp(logits - gmax), 0.0), axis=-1, keepdims=True)

    e_lo = N_EXPERT_GROUPS + g_sel * EXPERTS_PER_GROUP
    emask = (lane >= e_lo) & (lane < e_lo + EXPERTS_PER_GROUP)
    emax, _ = first_max(logits, emask)
    e_exp = jnp.where(emask, jnp.exp(logits - emax), 0.0)
    prob = e_exp / jnp.sum(e_exp, axis=-1, keepdims=True)
    p1, i1 = first_max(prob, emask)
    p2, i2 = first_max(prob, emask & (lane != i1))
    scale = g_gate / (p1 + p2)

    chosen = (lane == i1) | (lane == i2)
    earlier = (lax.broadcasted_iota(jnp.int32, (tm, tm), 1)
               < lax.broadcasted_iota(jnp.int32, (tm, tm), 0))
    before = jnp.dot(jnp.where(earlier, 1.0, 0.0).astype(BF16),
                     jnp.where(chosen, 1.0, 0.0).astype(BF16), preferred_element_type=F32)
    slot = before + cnt_ref[0:1, :]
    r1 = jnp.sum(jnp.where(lane == i1, slot, 0.0), axis=-1, keepdims=True).astype(jnp.int32)
    r2 = jnp.sum(jnp.where(lane == i2, slot, 0.0), axis=-1, keepdims=True).astype(jnp.int32)
    cnt_ref[...] = cnt_ref[...] + jnp.sum(jnp.where(chosen, 1.0, 0.0), axis=0, keepdims=True)

    ri_ref[...] = jnp.where(lane == 0, i1 - N_EXPERT_GROUPS,
                            jnp.where(lane == 1, i2 - N_EXPERT_GROUPS,
                                      jnp.where(lane == 2, r1, jnp.where(lane == 3, r2, 0))))
    rf_ref[...] = jnp.where(lane == 0, p1 * scale, jnp.where(lane == 1, p2 * scale, 0.0))


def _post(x, attn, u, gates, wao, wmix, pscale, wpo, wo, g2, wr, br):
    n = x.shape[0]
    tm = ROW_TILE
    row = lambda i: (i, 0)
    const = lambda i: (0, 0)
    halo_blocks = tm // POOL_HALO
    return pl.pallas_call(
        _post_kernel,
        grid=(n // tm,),
        in_specs=[
            pl.BlockSpec((tm, D_MODEL), row),
            pl.BlockSpec((tm, ATTN_WIDTH), row),
            pl.BlockSpec((tm, POOL_WIDTH), row),
            pl.BlockSpec((POOL_HALO, POOL_WIDTH),
                         lambda i: (jnp.maximum(i * halo_blocks - 1, 0), 0)),
            pl.BlockSpec((tm, 2 * D_MODEL), row),
            pl.BlockSpec((ATTN_WIDTH, D_MODEL), const),
            pl.BlockSpec((len(POOL_WINDOWS), POOL_GROUP_DIM, POOL_GROUP_DIM), lambda i: (0, 0, 0)),
            pl.BlockSpec((1, POOL_WIDTH), const),
            pl.BlockSpec((POOL_WIDTH, D_MODEL), const),
            pl.BlockSpec((D_MODEL, D_MODEL), const),
            pl.BlockSpec((1, D_MODEL), const),
            pl.BlockSpec((D_MODEL, LANES), const),
            pl.BlockSpec((1, LANES), const),
        ],
        out_specs=[
            pl.BlockSpec((tm, D_MODEL), row),
            pl.BlockSpec((tm, D_MODEL // 2), row),
            pl.BlockSpec((tm, LANES), row),
            pl.BlockSpec((tm, LANES), row),
            pl.BlockSpec((8, LANES), const),
        ],
        out_shape=[
            jax.ShapeDtypeStruct((n, D_MODEL), F32),
            jax.ShapeDtypeStruct((n, D_MODEL // 2), jnp.uint32),
            jax.ShapeDtypeStruct((n, LANES), jnp.int32),
            jax.ShapeDtypeStruct((n, LANES), F32),
            jax.ShapeDtypeStruct((8, LANES), F32),
        ],
        compiler_params=_params(1),
        name="post_attn_router",
    )(x, attn, u, u, gates, wao, wmix, pscale, wpo, wo, g2, wr, br)


def _experts_kernel(be_ref, nused_ref, x_ref, wg_ref, wu_ref, wd_ref, y_ref, wg_bf, wu_bf, wd_bf):
    i = pl.program_id(0)
    changed = (i == 0) | (be_ref[i] != be_ref[jnp.maximum(i - 1, 0)])
    used = i < nused_ref[0]

    @pl.when(changed & used)
    def _():
        wg_bf[...] = wg_ref[...].astype(BF16)
        wu_bf[...] = wu_ref[...].astype(BF16)
        wd_bf[...] = wd_ref[...].astype(BF16)

    @pl.when(used)
    def _():
        x = _unpack_bf16_pairs(x_ref[...])
        a = jnp.dot(x, wg_bf[...], preferred_element_type=F32)
        b = jnp.dot(x, wu_bf[...], preferred_element_type=F32)
        mid = (a * jax.nn.sigmoid(a) * b).astype(BF16)
        y_ref[...] = jnp.dot(mid, wd_bf[...], preferred_element_type=F32)

    @pl.when(jnp.logical_not(used))
    def _():
        y_ref[...] = jnp.zeros_like(y_ref)


def _experts(block_expert, n_used, x_rows, w_gate, w_up, w_down):
    n_rows = x_rows.shape[0]
    bm = EXPERT_BLOCK
    xmap = lambda i, be, nu: (jnp.minimum(i, nu[0] - 1), 0)
    wmap = lambda i, be, nu: (be[i], 0, 0)
    grid_spec = pltpu.PrefetchScalarGridSpec(
        num_scalar_prefetch=2,
        grid=(n_rows // bm,),
        in_specs=[
            pl.BlockSpec((bm, D_MODEL // 2), xmap),
            pl.BlockSpec((None, D_MODEL, EXPERT_HIDDEN), wmap),
            pl.BlockSpec((None, D_MODEL, EXPERT_HIDDEN), wmap),
            pl.BlockSpec((None, EXPERT_HIDDEN, D_MODEL), wmap),
        ],
        out_specs=pl.BlockSpec((bm, D_MODEL), lambda i, be, nu: (i, 0)),
        scratch_shapes=[
            pltpu.VMEM((D_MODEL, EXPERT_HIDDEN), BF16),
            pltpu.VMEM((D_MODEL, EXPERT_HIDDEN), BF16),
            pltpu.VMEM((EXPERT_HIDDEN, D_MODEL), BF16),
        ],
    )
    return pl.pallas_call(
        _experts_kernel,
        grid_spec=grid_spec,
        out_shape=jax.ShapeDtypeStruct((n_rows, D_MODEL), F32),
        compiler_params=_params(1),
        name="experts",
    )(block_expert, n_used, x_rows, w_gate, w_up, w_down)


def _dispatch_kernel(dest_ref, h_ref, init_ref, rows_ref, sem):
    del init_ref
    tm = h_ref.shape[0]
    base = pl.program_id(0) * (tm * TOP_K)

    def row_copy(r, d):
        return pltpu.make_async_copy(h_ref.at[pl.ds(r, 1)], rows_ref.at[pl.ds(d, 1)], sem)

    def issue(r, carry):
        for k in range(TOP_K):
            row_copy(r, dest_ref[base + TOP_K * r + k]).start()
        return carry

    def drain(r, carry):
        for k in range(TOP_K):
            row_copy(r, dest_ref[base + TOP_K * r + k]).wait()
        return carry

    lax.fori_loop(0, tm, issue, 0, unroll=8)
    lax.fori_loop(0, tm, drain, 0, unroll=8)


def _dispatch(dest, h2_packed, buf_len):
    n, width = h2_packed.shape
    tm = ROW_TILE
    grid_spec = pltpu.PrefetchScalarGridSpec(
        num_scalar_prefetch=1,
        grid=(n // tm,),
        in_specs=[
            pl.BlockSpec((tm, width), lambda i, dest: (i, 0)),
            pl.BlockSpec(memory_space=pl.ANY),
        ],
        out_specs=pl.BlockSpec(memory_space=pl.ANY),
        scratch_shapes=[pltpu.SemaphoreType.DMA(())],
    )
    return pl.pallas_call(
        _dispatch_kernel,
        grid_spec=grid_spec,
        out_shape=jax.ShapeDtypeStruct((buf_len, width), h2_packed.dtype),
        input_output_aliases={2: 0},
        compiler_params=_params(1),
        name="moe_dispatch",
    )(dest, h2_packed, jnp.zeros((buf_len, width), h2_packed.dtype))


def _combine_kernel(dest_ref, x_ref, gate_ref, y_ref, out_ref, ybuf, sems):
    tm = x_ref.shape[0]
    i = pl.program_id(0)
    cur = lax.rem(i, 2)

    def row_copy(step, slot, r, k):
        d = dest_ref[step * (tm * TOP_K) + TOP_K * r + k]
        return pltpu.make_async_copy(y_ref.at[pl.ds(d, 1)], ybuf.at[slot, k, pl.ds(r, 1)],
                                     sems.at[slot])

    def fetch(step, slot):
        def issue(r, carry):
            for k in range(TOP_K):
                row_copy(step, slot, r, k).start()
            return carry
        lax.fori_loop(0, tm, issue, 0, unroll=8)

    @pl.when(i == 0)
    def _():
        fetch(0, 0)

    @pl.when(i + 1 < pl.num_programs(0))
    def _():
        fetch(i + 1, 1 - cur)

    def drain(r, carry):
        for k in range(TOP_K):
            row_copy(i, cur, r, k).wait()
        return carry
    lax.fori_loop(0, tm, drain, 0, unroll=8)

    g = gate_ref[...]
    out_ref[...] = (x_ref[...] + g[:, 0:1] * ybuf[cur, 0] + g[:, 1:2] * ybuf[cur, 1])


def _combine(dest, x_new, gates, y_rows):
    n, d = x_new.shape
    tm = ROW_TILE
    grid_spec = pltpu.PrefetchScalarGridSpec(
        num_scalar_prefetch=1,
        grid=(n // tm,),
        in_specs=[
            pl.BlockSpec((tm, d), lambda i, dest: (i, 0)),
            pl.BlockSpec((tm, LANES), lambda i, dest: (i, 0)),
            pl.BlockSpec(memory_space=pl.ANY),
        ],
        out_specs=pl.BlockSpec((tm, d), lambda i, dest: (i, 0)),
        scratch_shapes=[
            pltpu.VMEM((2, TOP_K, tm, d), F32),
            pltpu.SemaphoreType.DMA((2,)),
        ],
    )
    return pl.pallas_call(
        _combine_kernel,
        grid_spec=grid_spec,
        out_shape=jax.ShapeDtypeStruct((n, d), F32),
        compiler_params=_params(1),
        name="moe_combine",
    )(dest, x_new, gates, y_rows)


def _dispatch_plan(expert_id, slot, counts):
    bm = EXPERT_BLOCK
    n_assign = expert_id.size
    padded = (counts + bm - 1) // bm * bm
    pend = jnp.cumsum(padded)
    pstart = pend - padded
    dest = (pstart[expert_id] + slot).reshape(-1).astype(jnp.int32)
    buf_len = n_assign + N_EXPERTS * bm
    block_start = jnp.arange(buf_len // bm, dtype=jnp.int32) * bm
    n_used = (pend[-1:] // bm).astype(jnp.int32)
    block_expert = jnp.searchsorted(pend, jnp.minimum(block_start, pend[-1] - bm),
                                    side="right").astype(jnp.int32)
    return dest, block_expert, n_used, buf_len


def kernel(x, norm1_g, w_in, q_norm_g, k_norm_g, w_attn_out, w_pool_mix, pool_scale, w_pool_out,
           w_o, norm2_g, w_router_group, b_router_group, w_router_expert, b_router_expert,
           w_exp_gate, w_exp_up, w_exp_down):
    batch, seq, d = x.shape
    assert (seq, d) == (SEQ, D_MODEL)
    n_tok = batch * seq
    depth = w_in.shape[0]
    half = ATTN_WIDTH // 2
    hsum = (jnp.arange(half)[:, None] // HEAD_DIM == jnp.arange(half)[None, :] // HEAD_DIM).astype(BF16)

    xf = x.reshape(n_tok, d)
    for layer in range(depth):
        qg = jnp.tile(q_norm_g[layer], N_HEADS)[None, :]
        kg = jnp.tile(k_norm_g[layer], N_HEADS)[None, :]
        q, k, v, u, gates = _inproj(xf, norm1_g[layer][None, :], w_in[layer].astype(BF16), qg, kg, hsum)
        attn = _attn(q.reshape(batch, seq, ATTN_WIDTH), k.reshape(batch, seq, ATTN_WIDTH),
                     v.reshape(batch, seq, ATTN_WIDTH), batch).reshape(n_tok, ATTN_WIDTH)

        w_router = jnp.concatenate(
            [w_router_group[layer],
             w_router_expert[layer].transpose(1, 0, 2).reshape(d, N_EXPERTS)], axis=1)
        w_router = jnp.pad(w_router, ((0, 0), (0, LANES - w_router.shape[1])))
        b_router = jnp.concatenate([b_router_group[layer], b_router_expert[layer].reshape(-1)])
        b_router = jnp.pad(b_router, (0, LANES - b_router.shape[0]))[None, :]
        x_new, h2, route_i, route_f, counts = _post(
            xf, attn, u, gates, w_attn_out[layer].astype(BF16), w_pool_mix[layer].astype(BF16),
            pool_scale[layer][None, :], w_pool_out[layer].astype(BF16), w_o[layer].astype(BF16),
            norm2_g[layer][None, :], w_router, b_router)

        counts = counts[0, N_EXPERT_GROUPS:N_EXPERT_GROUPS + N_EXPERTS].astype(jnp.int32)
        dest, block_expert, n_used, buf_len = _dispatch_plan(
            route_i[:, :TOP_K], route_i[:, TOP_K:2 * TOP_K], counts)
        x_rows = _dispatch(dest, h2, buf_len)
        y_rows = _experts(block_expert, n_used, x_rows,
                          w_exp_gate[layer], w_exp_up[layer], w_exp_down[layer])
        xf = _combine(dest, x_new, route_f, y_rows)
    return xf.reshape(batch, seq, d)
```

```python
import functools

import jax
import jax.numpy as jnp
from jax import lax
from jax.experimental import pallas as pl
from jax.experimental.pallas import tpu as pltpu

D_MODEL = 1024
SEQ = 4096
N_HEADS = 8
HEAD_DIM = 64
ATTN_WIDTH = N_HEADS * HEAD_DIM
DILATIONS = (1, 4, 16)
ATTN_BLOCK = 128
POOL_WINDOWS = (2, 4, 8, 16)
POOL_GROUP_DIM = 128
POOL_WIDTH = len(POOL_WINDOWS) * POOL_GROUP_DIM
POOL_HALO = 16
IN_PROJ_WIDTH = 3 * ATTN_WIDTH + POOL_WIDTH + 2 * D_MODEL
N_EXPERT_GROUPS = 4
EXPERTS_PER_GROUP = 8
N_EXPERTS = N_EXPERT_GROUPS * EXPERTS_PER_GROUP
TOP_K = 2
EXPERT_HIDDEN = 512
RMS_EPS = 1e-6
NEG_INF = -1e30

LANES = 128
ROW_TILE = 256
EXPERT_BLOCK = 256
VMEM_LIMIT = 48 * 1024 * 1024

F32 = jnp.float32
BF16 = jnp.bfloat16


def _params(n_axes):
    return pltpu.CompilerParams(
        dimension_semantics=("arbitrary",) * n_axes, vmem_limit_bytes=VMEM_LIMIT)


def _inproj_kernel(x_ref, g_ref, w_ref, qg_ref, kg_ref, hsum_ref,
                   q_ref, k_ref, v_ref, u_ref, gate_ref):
    x = x_ref[...]
    ms = jnp.mean(x * x, axis=-1, keepdims=True)
    h = (x * lax.rsqrt(ms + RMS_EPS) * g_ref[...]).astype(BF16)

    def proj(lo, hi):
        return jnp.dot(h, w_ref[:, lo:hi], preferred_element_type=F32)

    def head_norm(t, gain):
        sq = (t * t).astype(BF16)
        half = ATTN_WIDTH // 2
        ssq = jnp.concatenate(
            [jnp.dot(sq[:, j * half:(j + 1) * half], hsum_ref[...], preferred_element_type=F32)
             for j in range(2)], axis=-1)
        return t * lax.rsqrt(ssq * (1.0 / HEAD_DIM) + RMS_EPS) * gain

    w = ATTN_WIDTH
    q_ref[...] = head_norm(proj(0, w), qg_ref[...])
    k_ref[...] = head_norm(proj(w, 2 * w), kg_ref[...])
    v_ref[...] = proj(2 * w, 3 * w)
    u_ref[...] = proj(3 * w, 3 * w + POOL_WIDTH)
    base = 3 * w + POOL_WIDTH
    for j in range(2 * D_MODEL // 512):
        gate_ref[:, j * 512:(j + 1) * 512] = jax.nn.sigmoid(
            proj(base + j * 512, base + (j + 1) * 512)).astype(BF16)


def _inproj(x, g1, w_in, qg, kg, hsum):
    n = x.shape[0]
    row = lambda i: (i, 0)
    const = lambda i: (0, 0)
    return pl.pallas_call(
        _inproj_kernel,
        grid=(n // ROW_TILE,),
        in_specs=[
            pl.BlockSpec((ROW_TILE, D_MODEL), row),
            pl.BlockSpec((1, D_MODEL), const),
            pl.BlockSpec((D_MODEL, IN_PROJ_WIDTH), const),
            pl.BlockSpec((1, ATTN_WIDTH), const),
            pl.BlockSpec((1, ATTN_WIDTH), const),
            pl.BlockSpec((ATTN_WIDTH // 2, ATTN_WIDTH // 2), const),
        ],
        out_specs=[
            pl.BlockSpec((ROW_TILE, ATTN_WIDTH), row),
            pl.BlockSpec((ROW_TILE, ATTN_WIDTH), row),
            pl.BlockSpec((ROW_TILE, ATTN_WIDTH), row),
            pl.BlockSpec((ROW_TILE, POOL_WIDTH), row),
            pl.BlockSpec((ROW_TILE, 2 * D_MODEL), row),
        ],
        out_shape=[
            jax.ShapeDtypeStruct((n, ATTN_WIDTH), F32),
            jax.ShapeDtypeStruct((n, ATTN_WIDTH), F32),
            jax.ShapeDtypeStruct((n, ATTN_WIDTH), F32),
            jax.ShapeDtypeStruct((n, POOL_WIDTH), F32),
            jax.ShapeDtypeStruct((n, 2 * D_MODEL), BF16),
        ],
        compiler_params=_params(1),
        name="inproj",
    )(x, g1, w_in, qg, kg, hsum)


def _attn_kernel(q_ref, k_ref, v_ref, out_ref, o0, o1, o2, l0, l1, l2):
    blk = ATTN_BLOCK
    o_scr = (o0, o1, o2)
    l_scr = (l0, l1, l2)
    head_a = lax.broadcasted_iota(jnp.int32, (blk, LANES), 1) < HEAD_DIM
    qi = lax.broadcasted_iota(jnp.int32, (2 * blk, 2 * blk), 0) & (blk - 1)
    kj = lax.broadcasted_iota(jnp.int32, (2 * blk, 2 * blk), 1)
    q_minus_k = qi - kj
    ones_cols = jnp.ones((2 * blk, LANES), BF16)

    for g, d in enumerate(DILATIONS):
        n_blk = SEQ // (blk * d)
        log_n_blk = n_blk.bit_length() - 1

        def rows(start, size, d=d):
            return pl.ds(start, size) if d == 1 else pl.ds(start, size, stride=d)

        def body(idx, carry, d=d, g=g, n_blk=n_blk, log_n_blk=log_n_blk, rows=rows):
            c = lax.shift_right_logical(idx, log_n_blk)
            nb = idx & (n_blk - 1)
            kb = jnp.maximum(nb - 1, 0)
            q_start = nb * (blk * d) + c
            k_start = kb * (blk * d) + c
            off = (nb - kb) * blk

            q2 = q_ref[rows(q_start, blk), :] * (HEAD_DIM ** -0.5)
            k2 = k_ref[rows(k_start, 2 * blk), :].astype(BF16)
            v2 = v_ref[rows(k_start, 2 * blk), :].astype(BF16)
            qs = jnp.concatenate(
                [jnp.where(head_a, q2, 0.0), jnp.where(head_a, 0.0, q2)], axis=0).astype(BF16)
            s = lax.dot_general(qs, k2, (((1,), (1,)), ((), ())), preferred_element_type=F32)
            dist = q_minus_k + off
            s = jnp.where((dist >= 0) & (dist <= blk), s, NEG_INF)
            m = jnp.max(s, axis=-1, keepdims=True)
            e = jnp.exp(s - m).astype(BF16)
            r = jnp.dot(e, jnp.concatenate([v2, ones_cols], axis=1), preferred_element_type=F32)
            denom = r[:, LANES:]
            o = r[:, :LANES] / denom
            lse = m + jnp.log(denom)
            o_scr[g][rows(q_start, blk), :] = jnp.where(head_a, o[:blk], o[blk:])
            l_scr[g][rows(q_start, blk), :] = jnp.where(head_a, lse[:blk], lse[blk:])
            return carry

        lax.fori_loop(0, SEQ // blk, body, 0, unroll=4)

    chunk = 512

    def mix(i, carry):
        r = pl.ds(pl.multiple_of(i * chunk, chunk), chunk)
        la, lb, lc = l0[r, :], l1[r, :], l2[r, :]
        m = jnp.maximum(jnp.maximum(la, lb), lc)
        wa, wb, wc = jnp.exp(la - m), jnp.exp(lb - m), jnp.exp(lc - m)
        acc = wa * o0[r, :] + wb * o1[r, :] + wc * o2[r, :]
        out_ref[r, :] = (acc / (wa + wb + wc)).astype(out_ref.dtype)
        return carry

    lax.fori_loop(0, SEQ // chunk, mix, 0)


def _attn(q, k, v, batch):
    spec = pl.BlockSpec((None, SEQ, LANES), lambda b, hp: (b, 0, hp))
    return pl.pallas_call(
        _attn_kernel,
        grid=(batch, ATTN_WIDTH // LANES),
        in_specs=[spec, spec, spec],
        out_specs=spec,
        out_shape=jax.ShapeDtypeStruct((batch, SEQ, ATTN_WIDTH), BF16),
        scratch_shapes=[pltpu.VMEM((SEQ, LANES), F32) for _ in range(6)],
        compiler_params=_params(2),
        name="dilated_attn",
    )(q, k, v)


def _pack_bf16_pairs(x):
    c = x.shape[1] // 2
    bits = pltpu.bitcast(x.astype(BF16).astype(F32), jnp.uint32)
    return lax.shift_right_logical(bits[:, :c], jnp.uint32(16)) | bits[:, c:]


def _unpack_bf16_pairs(w):
    lo = pltpu.bitcast(lax.shift_left(w, jnp.uint32(16)), F32)
    hi = pltpu.bitcast(w & jnp.uint32(0xFFFF0000), F32)
    return jnp.concatenate([lo, hi], axis=1).astype(BF16)


def _post_kernel(x_ref, attn_ref, u_ref, halo_ref, gate_ref, wao_ref, wmix_ref, pscale_ref,
                 wpo_ref, wo_ref, g2_ref, wr_ref, br_ref,
                 xo_ref, h2_ref, ri_ref, rf_ref, cnt_ref):
    tm = x_ref.shape[0]

    @pl.when(pl.program_id(0) == 0)
    def _():
        cnt_ref[...] = jnp.zeros_like(cnt_ref)

    pos0 = lax.rem(pl.program_id(0) * tm, SEQ)
    pos = pos0 + lax.broadcasted_iota(jnp.int32, (tm, 1), 0)
    u = u_ref[...]
    halo = halo_ref[...] * (pos0 > 0).astype(F32)

    mixed = []
    for gi, w in enumerate(POOL_WINDOWS):
        lo = gi * POOL_GROUP_DIM
        ug = u[:, lo:lo + POOL_GROUP_DIM]
        ext = jnp.concatenate([halo[:, lo:lo + POOL_GROUP_DIM], ug], axis=0)
        shift = 1
        while shift < w:
            ext = ext + pltpu.roll(ext, shift, 0)
            shift *= 2
        cnt = jnp.minimum(pos + 1, w).astype(F32)
        pooled = ext[POOL_HALO:] / cnt - ug
        mixed.append(jnp.dot(pooled.astype(BF16), wmix_ref[gi], preferred_element_type=F32))
    pool_out = (jnp.concatenate(mixed, axis=-1) * pscale_ref[...]).astype(BF16)

    y_a = jnp.dot(attn_ref[...], wao_ref[...], preferred_element_type=F32)
    y_p = jnp.dot(pool_out, wpo_ref[...], preferred_element_type=F32)
    gates = gate_ref[...]
    merged = gates[:, :D_MODEL].astype(F32) * y_a + gates[:, D_MODEL:].astype(F32) * y_p
    x_new = x_ref[...] + jnp.dot(merged.astype(BF16), wo_ref[...], preferred_element_type=F32)
    xo_ref[...] = x_new

    ms = jnp.mean(x_new * x_new, axis=-1, keepdims=True)
    h2 = x_new * lax.rsqrt(ms + RMS_EPS) * g2_ref[...]
    h2_ref[...] = _pack_bf16_pairs(h2)

    logits = jnp.dot(h2, wr_ref[...], preferred_element_type=F32,
                     precision=lax.Precision.HIGHEST) + br_ref[...]
    lane = lax.broadcasted_iota(jnp.int32, (tm, LANES), 1)
    big = jnp.int32(LANES)

    def first_max(vals, mask):
        vmax = jnp.max(jnp.where(mask, vals, -jnp.inf), axis=-1, keepdims=True)
        idx = jnp.min(jnp.where(mask & (vals == vmax), lane, big), axis=-1, keepdims=True)
        return vmax, idx

    gmask = lane < N_EXPERT_GROUPS
    gmax, g_sel = first_max(logits, gmask)
    g_gate = 1.0 / jnp.sum(jnp.where(gmask, jnp.exp(logits - gmax), 0.0), axis=-1, keepdims=True)

    e_lo = N_EXPERT_GROUPS + g_sel * EXPERTS_PER_GROUP
    emask = (lane >= e_lo) & (lane < e_lo + EXPERTS_PER_GROUP)
    emax, _ = first_max(logits, emask)
    e_exp = jnp.where(emask, jnp.exp(logits - emax), 0.0)
    prob = e_exp / jnp.sum(e_exp, axis=-1, keepdims=True)
    p1, i1 = first_max(prob, emask)
    p2, i2 = first_max(prob, emask & (lane != i1))
    scale = g_gate / (p1 + p2)

    chosen = (lane == i1) | (lane == i2)
    earlier = (lax.broadcasted_iota(jnp.int32, (tm, tm), 1)
               < lax.broadcasted_iota(jnp.int32, (tm, tm), 0))
    before = jnp.dot(jnp.where(earlier, 1.0, 0.0).astype(BF16),
                     jnp.where(chosen, 1.0, 0.0).astype(BF16), preferred_element_type=F32)
    slot = before + cnt_ref[0:1, :]
    r1 = jnp.sum(jnp.where(lane == i1, slot, 0.0), axis=-1, keepdims=True).astype(jnp.int32)
    r2 = jnp.sum(jnp.where(lane == i2, slot, 0.0), axis=-1, keepdims=True).astype(jnp.int32)
    cnt_ref[...] = cnt_ref[...] + jnp.sum(jnp.where(chosen, 1.0, 0.0), axis=0, keepdims=True)

    ri_ref[...] = jnp.where(lane == 0, i1 - N_EXPERT_GROUPS,
                            jnp.where(lane == 1, i2 - N_EXPERT_GROUPS,
                                      jnp.where(lane == 2, r1, jnp.where(lane == 3, r2, 0))))
    rf_ref[...] = jnp.where(lane == 0, p1 * scale, jnp.where(lane == 1, p2 * scale, 0.0))


def _post(x, attn, u, gates, wao, wmix, pscale, wpo, wo, g2, wr, br):
    n = x.shape[0]
    tm = ROW_TILE
    row = lambda i: (i, 0)
    const = lambda i: (0, 0)
    halo_blocks = tm // POOL_HALO
    return pl.pallas_call(
        _post_kernel,
        grid=(n // tm,),
        in_specs=[
            pl.BlockSpec((tm, D_MODEL), row),
            pl.BlockSpec((tm, ATTN_WIDTH), row),
            pl.BlockSpec((tm, POOL_WIDTH), row),
            pl.BlockSpec((POOL_HALO, POOL_WIDTH),
                         lambda i: (jnp.maximum(i * halo_blocks - 1, 0), 0)),
            pl.BlockSpec((tm, 2 * D_MODEL), row),
            pl.BlockSpec((ATTN_WIDTH, D_MODEL), const),
            pl.BlockSpec((len(POOL_WINDOWS), POOL_GROUP_DIM, POOL_GROUP_DIM), lambda i: (0, 0, 0)),
            pl.BlockSpec((1, POOL_WIDTH), const),
            pl.BlockSpec((POOL_WIDTH, D_MODEL), const),
            pl.BlockSpec((D_MODEL, D_MODEL), const),
            pl.BlockSpec((1, D_MODEL), const),
            pl.BlockSpec((D_MODEL, LANES), const),
            pl.BlockSpec((1, LANES), const),
        ],
        out_specs=[
            pl.BlockSpec((tm, D_MODEL), row),
            pl.BlockSpec((tm, D_MODEL // 2), row),
            pl.BlockSpec((tm, LANES), row),
            pl.BlockSpec((tm, LANES), row),
            pl.BlockSpec((8, LANES), const),
        ],
        out_shape=[
            jax.ShapeDtypeStruct((n, D_MODEL), F32),
            jax.ShapeDtypeStruct((n, D_MODEL // 2), jnp.uint32),
            jax.ShapeDtypeStruct((n, LANES), jnp.int32),
            jax.ShapeDtypeStruct((n, LANES), F32),
            jax.ShapeDtypeStruct((8, LANES), F32),
        ],
        compiler_params=_params(1),
        name="post_attn_router",
    )(x, attn, u, u, gates, wao, wmix, pscale, wpo, wo, g2, wr, br)


def _experts_kernel(be_ref, nused_ref, x_ref, wg_ref, wu_ref, wd_ref, y_ref, wg_bf, wu_bf, wd_bf):
    i = pl.program_id(0)
    changed = (i == 0) | (be_ref[i] != be_ref[jnp.maximum(i - 1, 0)])
    used = i < nused_ref[0]

    @pl.when(changed & used)
    def _():
        wg_bf[...] = wg_ref[...].astype(BF16)
        wu_bf[...] = wu_ref[...].astype(BF16)
        wd_bf[...] = wd_ref[...].astype(BF16)

    @pl.when(used)
    def _():
        x = _unpack_bf16_pairs(x_ref[...])
        a = jnp.dot(x, wg_bf[...], preferred_element_type=F32)
        b = jnp.dot(x, wu_bf[...], preferred_element_type=F32)
        mid = (a * jax.nn.sigmoid(a) * b).astype(BF16)
        y_ref[...] = jnp.dot(mid, wd_bf[...], preferred_element_type=F32)

    @pl.when(jnp.logical_not(used))
    def _():
        y_ref[...] = jnp.zeros_like(y_ref)


def _experts(block_expert, n_used, x_rows, w_gate, w_up, w_down, layer):
    n_rows = x_rows.shape[0]
    bm = EXPERT_BLOCK
    xmap = lambda i, be, nu: (jnp.maximum(jnp.minimum(i, nu[0] - 1), 0), 0)
    wmap = lambda i, be, nu: (layer, be[i], 0, 0)
    grid_spec = pltpu.PrefetchScalarGridSpec(
        num_scalar_prefetch=2,
        grid=(n_rows // bm,),
        in_specs=[
            pl.BlockSpec((bm, D_MODEL // 2), xmap),
            pl.BlockSpec((None, None, D_MODEL, EXPERT_HIDDEN), wmap),
            pl.BlockSpec((None, None, D_MODEL, EXPERT_HIDDEN), wmap),
            pl.BlockSpec((None, None, EXPERT_HIDDEN, D_MODEL), wmap),
        ],
        out_specs=pl.BlockSpec((bm, D_MODEL), lambda i, be, nu: (i, 0)),
        scratch_shapes=[
            pltpu.VMEM((D_MODEL, EXPERT_HIDDEN), BF16),
            pltpu.VMEM((D_MODEL, EXPERT_HIDDEN), BF16),
            pltpu.VMEM((EXPERT_HIDDEN, D_MODEL), BF16),
        ],
    )
    return pl.pallas_call(
        _experts_kernel,
        grid_spec=grid_spec,
        out_shape=jax.ShapeDtypeStruct((n_rows, D_MODEL), F32),
        compiler_params=_params(1),
        name="experts",
    )(block_expert, n_used, x_rows, w_gate, w_up, w_down)


def _dispatch_kernel(dest_ref, h_ref, init_ref, rows_ref, sem):
    del init_ref
    tm = h_ref.shape[0]
    base = pl.program_id(0) * (tm * TOP_K)

    def row_copy(r, d):
        return pltpu.make_async_copy(h_ref.at[pl.ds(r, 1)], rows_ref.at[pl.ds(d, 1)], sem)

    def issue(r, carry):
        for k in range(TOP_K):
            row_copy(r, dest_ref[base + TOP_K * r + k]).start()
        return carry

    def drain(r, carry):
        for k in range(TOP_K):
            row_copy(r, dest_ref[base + TOP_K * r + k]).wait()
        return carry

    lax.fori_loop(0, tm, issue, 0, unroll=8)
    lax.fori_loop(0, tm, drain, 0, unroll=8)


def _dispatch(dest, h2_packed, buf_len):
    n, width = h2_packed.shape
    tm = ROW_TILE
    grid_spec = pltpu.PrefetchScalarGridSpec(
        num_scalar_prefetch=1,
        grid=(n // tm,),
        in_specs=[
            pl.BlockSpec((tm, width), lambda i, dest: (i, 0)),
            pl.BlockSpec(memory_space=pl.ANY),
        ],
        out_specs=pl.BlockSpec(memory_space=pl.ANY),
        scratch_shapes=[pltpu.SemaphoreType.DMA(())],
    )
    return pl.pallas_call(
        _dispatch_kernel,
        grid_spec=grid_spec,
        out_shape=jax.ShapeDtypeStruct((buf_len, width), h2_packed.dtype),
        input_output_aliases={2: 0},
        compiler_params=_params(1),
        name="moe_dispatch",
    )(dest, h2_packed, jnp.zeros((buf_len, width), h2_packed.dtype))


def _combine_kernel(dest_ref, x_ref, gate_ref, y_ref, out_ref, ybuf, sems):
    tm = x_ref.shape[0]
    i = pl.program_id(0)
    cur = lax.rem(i, 2)

    def row_copy(step, slot, r, k):
        d = dest_ref[step * (tm * TOP_K) + TOP_K * r + k]
        return pltpu.make_async_copy(y_ref.at[pl.ds(d, 1)], ybuf.at[slot, k, pl.ds(r, 1)],
                                     sems.at[slot])

    def fetch(step, slot):
        def issue(r, carry):
            for k in range(TOP_K):
                row_copy(step, slot, r, k).start()
            return carry
        lax.fori_loop(0, tm, issue, 0, unroll=8)

    @pl.when(i == 0)
    def _():
        fetch(0, 0)

    @pl.when(i + 1 < pl.num_programs(0))
    def _():
        fetch(i + 1, 1 - cur)

    def drain(r, carry):
        for k in range(TOP_K):
            row_copy(i, cur, r, k).wait()
        return carry
    lax.fori_loop(0, tm, drain, 0, unroll=8)

    g = gate_ref[...]
    out_ref[...] = (x_ref[...] + g[:, 0:1] * ybuf[cur, 0] + g[:, 1:2] * ybuf[cur, 1])


def _combine(dest, x_new, gates, y_rows):
    n, d = x_new.shape
    tm = ROW_TILE
    grid_spec = pltpu.PrefetchScalarGridSpec(
        num_scalar_prefetch=1,
        grid=(n // tm,),
        in_specs=[
            pl.BlockSpec((tm, d), lambda i, dest: (i, 0)),
            pl.BlockSpec((tm, LANES), lambda i, dest: (i, 0)),
            pl.BlockSpec(memory_space=pl.ANY),
        ],
        out_specs=pl.BlockSpec((tm, d), lambda i, dest: (i, 0)),
        scratch_shapes=[
            pltpu.VMEM((2, TOP_K, tm, d), F32),
            pltpu.SemaphoreType.DMA((2,)),
        ],
    )
    return pl.pallas_call(
        _combine_kernel,
        grid_spec=grid_spec,
        out_shape=jax.ShapeDtypeStruct((n, d), F32),
        compiler_params=_params(1),
        name="moe_combine",
    )(dest, x_new, gates, y_rows)


def _dispatch_plan(expert_id, slot, counts):
    bm = EXPERT_BLOCK
    n_assign = expert_id.size
    padded = (counts + bm - 1) // bm * bm
    pend = jnp.cumsum(padded)
    pstart = pend - padded
    experts = jnp.arange(N_EXPERTS, dtype=jnp.int32)
    seg_start = jnp.sum(jnp.where(expert_id[..., None] == experts, pstart, 0), axis=-1)
    dest = (seg_start + slot).reshape(-1).astype(jnp.int32)
    buf_len = n_assign + N_EXPERTS * bm
    block_start = jnp.arange(buf_len // bm, dtype=jnp.int32) * bm
    n_used = (pend[-1:] // bm).astype(jnp.int32)
    block_start = jnp.minimum(block_start, pend[-1] - bm)
    block_expert = jnp.sum((pend[None, :] <= block_start[:, None]).astype(jnp.int32), axis=1)
    return dest, jnp.minimum(block_expert, N_EXPERTS - 1), n_used, buf_len


def kernel(x, norm1_g, w_in, q_norm_g, k_norm_g, w_attn_out, w_pool_mix, pool_scale, w_pool_out,
           w_o, norm2_g, w_router_group, b_router_group, w_router_expert, b_router_expert,
           w_exp_gate, w_exp_up, w_exp_down):
    batch, seq, d = x.shape
    assert (seq, d) == (SEQ, D_MODEL)
    n_tok = batch * seq
    depth = w_in.shape[0]
    half = ATTN_WIDTH // 2
    hsum = (jnp.arange(half)[:, None] // HEAD_DIM == jnp.arange(half)[None, :] // HEAD_DIM).astype(BF16)

    xf = x.reshape(n_tok, d)
    for layer in range(depth):
        qg = jnp.tile(q_norm_g[layer], N_HEADS)[None, :]
        kg = jnp.tile(k_norm_g[layer], N_HEADS)[None, :]
        q, k, v, u, gates = _inproj(xf, norm1_g[layer][None, :], w_in[layer].astype(BF16), qg, kg, hsum)
        attn = _attn(q.reshape(batch, seq, ATTN_WIDTH), k.reshape(batch, seq, ATTN_WIDTH),
                     v.reshape(batch, seq, ATTN_WIDTH), batch).reshape(n_tok, ATTN_WIDTH)

        w_router = jnp.concatenate(
            [w_router_group[layer],
             w_router_expert[layer].transpose(1, 0, 2).reshape(d, N_EXPERTS)], axis=1)
        w_router = jnp.pad(w_router, ((0, 0), (0, LANES - w_router.shape[1])))
        b_router = jnp.concatenate([b_router_group[layer], b_router_expert[layer].reshape(-1)])
        b_router = jnp.pad(b_router, (0, LANES - b_router.shape[0]))[None, :]
        x_new, h2, route_i, route_f, counts = _post(
            xf, attn, u, gates, w_attn_out[layer].astype(BF16), w_pool_mix[layer].astype(BF16),
            pool_scale[layer][None, :], w_pool_out[layer].astype(BF16), w_o[layer].astype(BF16),
            norm2_g[layer][None, :], w_router, b_router)

        counts = counts[0, N_EXPERT_GROUPS:N_EXPERT_GROUPS + N_EXPERTS].astype(jnp.int32)
        dest, block_expert, n_used, buf_len = _dispatch_plan(
            route_i[:, :TOP_K], route_i[:, TOP_K:2 * TOP_K], counts)
        x_rows = _dispatch(dest, h2, buf_len)
        y_rows = _experts(block_expert, n_used, x_rows, w_exp_gate, w_exp_up, w_exp_down, layer)
        xf = _combine(dest, x_new, route_f, y_rows)
    return xf.reshape(batch, seq, d)
```

```python
import functools

import jax
import jax.numpy as jnp
from jax import lax
from jax.experimental import pallas as pl
from jax.experimental.pallas import tpu as pltpu

D_MODEL = 1024
SEQ = 4096
N_HEADS = 8
HEAD_DIM = 64
ATTN_WIDTH = N_HEADS * HEAD_DIM
DILATIONS = (1, 4, 16)
ATTN_BLOCK = 128
POOL_WINDOWS = (2, 4, 8, 16)
POOL_GROUP_DIM = 128
POOL_WIDTH = len(POOL_WINDOWS) * POOL_GROUP_DIM
POOL_HALO = 16
IN_PROJ_WIDTH = 3 * ATTN_WIDTH + POOL_WIDTH + 2 * D_MODEL
N_EXPERT_GROUPS = 4
EXPERTS_PER_GROUP = 8
N_EXPERTS = N_EXPERT_GROUPS * EXPERTS_PER_GROUP
TOP_K = 2
EXPERT_HIDDEN = 512
RMS_EPS = 1e-6
NEG_INF = -1e30

LANES = 128
ROW_TILE = 256
EXPERT_BLOCK = 256
VMEM_LIMIT = 48 * 1024 * 1024

F32 = jnp.float32
BF16 = jnp.bfloat16


def _params(n_axes):
    return pltpu.CompilerParams(
        dimension_semantics=("arbitrary",) * n_axes, vmem_limit_bytes=VMEM_LIMIT)


def _inproj_kernel(x_ref, g_ref, w_ref, qg_ref, kg_ref, hsum_ref,
                   q_ref, k_ref, v_ref, u_ref, gate_ref):
    x = x_ref[...]
    ms = jnp.mean(x * x, axis=-1, keepdims=True)
    h = (x * lax.rsqrt(ms + RMS_EPS) * g_ref[...]).astype(BF16)

    def proj(lo, hi):
        return jnp.dot(h, w_ref[:, lo:hi], preferred_element_type=F32)

    def head_norm(t, gain):
        sq = (t * t).astype(BF16)
        half = ATTN_WIDTH // 2
        ssq = jnp.concatenate(
            [jnp.dot(sq[:, j * half:(j + 1) * half], hsum_ref[...], preferred_element_type=F32)
             for j in range(2)], axis=-1)
        return t * lax.rsqrt(ssq * (1.0 / HEAD_DIM) + RMS_EPS) * gain

    w = ATTN_WIDTH
    q_ref[...] = head_norm(proj(0, w), qg_ref[...])
    k_ref[...] = head_norm(proj(w, 2 * w), kg_ref[...])
    v_ref[...] = proj(2 * w, 3 * w)
    u_ref[...] = proj(3 * w, 3 * w + POOL_WIDTH)
    base = 3 * w + POOL_WIDTH
    for j in range(2 * D_MODEL // 512):
        gate_ref[:, j * 512:(j + 1) * 512] = jax.nn.sigmoid(
            proj(base + j * 512, base + (j + 1) * 512)).astype(BF16)


def _inproj(x, g1, w_in, qg, kg, hsum):
    n = x.shape[0]
    row = lambda i: (i, 0)
    const = lambda i: (0, 0)
    return pl.pallas_call(
        _inproj_kernel,
        grid=(n // ROW_TILE,),
        in_specs=[
            pl.BlockSpec((ROW_TILE, D_MODEL), row),
            pl.BlockSpec((1, D_MODEL), const),
            pl.BlockSpec((D_MODEL, IN_PROJ_WIDTH), const),
            pl.BlockSpec((1, ATTN_WIDTH), const),
            pl.BlockSpec((1, ATTN_WIDTH), const),
            pl.BlockSpec((ATTN_WIDTH // 2, ATTN_WIDTH // 2), const),
        ],
        out_specs=[
            pl.BlockSpec((ROW_TILE, ATTN_WIDTH), row),
            pl.BlockSpec((ROW_TILE, ATTN_WIDTH), row),
            pl.BlockSpec((ROW_TILE, ATTN_WIDTH), row),
            pl.BlockSpec((ROW_TILE, POOL_WIDTH), row),
            pl.BlockSpec((ROW_TILE, 2 * D_MODEL), row),
        ],
        out_shape=[
            jax.ShapeDtypeStruct((n, ATTN_WIDTH), F32),
            jax.ShapeDtypeStruct((n, ATTN_WIDTH), F32),
            jax.ShapeDtypeStruct((n, ATTN_WIDTH), F32),
            jax.ShapeDtypeStruct((n, POOL_WIDTH), F32),
            jax.ShapeDtypeStruct((n, 2 * D_MODEL), BF16),
        ],
        compiler_params=_params(1),
        name="inproj",
    )(x, g1, w_in, qg, kg, hsum)


def _attn_kernel(q_ref, k_ref, v_ref, out_ref, o0, o1, o2, l0, l1, l2):
    blk = ATTN_BLOCK
    o_scr = (o0, o1, o2)
    l_scr = (l0, l1, l2)
    head_a = lax.broadcasted_iota(jnp.int32, (blk, LANES), 1) < HEAD_DIM
    qi = lax.broadcasted_iota(jnp.int32, (2 * blk, 2 * blk), 0) & (blk - 1)
    kj = lax.broadcasted_iota(jnp.int32, (2 * blk, 2 * blk), 1)
    q_minus_k = qi - kj
    ones_cols = jnp.ones((2 * blk, LANES), BF16)

    for g, d in enumerate(DILATIONS):
        n_blk = SEQ // (blk * d)
        log_n_blk = n_blk.bit_length() - 1

        def rows(start, size, d=d):
            return pl.ds(start, size) if d == 1 else pl.ds(start, size, stride=d)

        def body(idx, carry, d=d, g=g, n_blk=n_blk, log_n_blk=log_n_blk, rows=rows):
            c = lax.shift_right_logical(idx, log_n_blk)
            nb = idx & (n_blk - 1)
            kb = jnp.maximum(nb - 1, 0)
            q_start = nb * (blk * d) + c
            k_start = kb * (blk * d) + c
            off = (nb - kb) * blk

            q2 = q_ref[rows(q_start, blk), :] * (HEAD_DIM ** -0.5)
            k2 = k_ref[rows(k_start, 2 * blk), :].astype(BF16)
            v2 = v_ref[rows(k_start, 2 * blk), :].astype(BF16)
            qs = jnp.concatenate(
                [jnp.where(head_a, q2, 0.0), jnp.where(head_a, 0.0, q2)], axis=0).astype(BF16)
            s = lax.dot_general(qs, k2, (((1,), (1,)), ((), ())), preferred_element_type=F32)
            dist = q_minus_k + off
            s = jnp.where((dist >= 0) & (dist <= blk), s, NEG_INF)
            m = jnp.max(s, axis=-1, keepdims=True)
            e = jnp.exp(s - m).astype(BF16)
            r = jnp.dot(e, jnp.concatenate([v2, ones_cols], axis=1), preferred_element_type=F32)
            denom = r[:, LANES:]
            o = r[:, :LANES] / denom
            lse = m + jnp.log(denom)
            o_scr[g][rows(q_start, blk), :] = jnp.where(head_a, o[:blk], o[blk:])
            l_scr[g][rows(q_start, blk), :] = jnp.where(head_a, lse[:blk], lse[blk:])
            return carry

        lax.fori_loop(0, SEQ // blk, body, 0, unroll=4)

    chunk = 512

    def mix(i, carry):
        r = pl.ds(pl.multiple_of(i * chunk, chunk), chunk)
        la, lb, lc = l0[r, :], l1[r, :], l2[r, :]
        m = jnp.maximum(jnp.maximum(la, lb), lc)
        wa, wb, wc = jnp.exp(la - m), jnp.exp(lb - m), jnp.exp(lc - m)
        acc = wa * o0[r, :] + wb * o1[r, :] + wc * o2[r, :]
        out_ref[r, :] = (acc / (wa + wb + wc)).astype(out_ref.dtype)
        return carry

    lax.fori_loop(0, SEQ // chunk, mix, 0)


def _attn(q, k, v, batch):
    spec = pl.BlockSpec((None, SEQ, LANES), lambda b, hp: (b, 0, hp))
    return pl.pallas_call(
        _attn_kernel,
        grid=(batch, ATTN_WIDTH // LANES),
        in_specs=[spec, spec, spec],
        out_specs=spec,
        out_shape=jax.ShapeDtypeStruct((batch, SEQ, ATTN_WIDTH), BF16),
        scratch_shapes=[pltpu.VMEM((SEQ, LANES), F32) for _ in range(6)],
        compiler_params=_params(2),
        name="dilated_attn",
    )(q, k, v)


def _pack_bf16_pairs(x):
    c = x.shape[1] // 2
    bits = pltpu.bitcast(x.astype(BF16).astype(F32), jnp.uint32)
    return lax.shift_right_logical(bits[:, :c], jnp.uint32(16)) | bits[:, c:]


def _unpack_bf16_pairs(w):
    lo = pltpu.bitcast(lax.shift_left(w, jnp.uint32(16)), F32)
    hi = pltpu.bitcast(w & jnp.uint32(0xFFFF0000), F32)
    return jnp.concatenate([lo, hi], axis=1).astype(BF16)


def _post_kernel(x_ref, attn_ref, u_ref, halo_ref, gate_ref, wao_ref, wmix_ref, pscale_ref,
                 wpo_ref, wo_ref, g2_ref, wr_ref, br_ref,
                 xo_ref, h2_ref, ri_ref, rf_ref, cnt_ref):
    tm = x_ref.shape[0]

    @pl.when(pl.program_id(0) == 0)
    def _():
        cnt_ref[...] = jnp.zeros_like(cnt_ref)

    pos0 = lax.rem(pl.program_id(0) * tm, SEQ)
    pos = pos0 + lax.broadcasted_iota(jnp.int32, (tm, 1), 0)
    u = u_ref[...]
    halo = halo_ref[...] * (pos0 > 0).astype(F32)

    mixed = []
    for gi, w in enumerate(POOL_WINDOWS):
        lo = gi * POOL_GROUP_DIM
        ug = u[:, lo:lo + POOL_GROUP_DIM]
        ext = jnp.concatenate([halo[:, lo:lo + POOL_GROUP_DIM], ug], axis=0)
        shift = 1
        while shift < w:
            ext = ext + pltpu.roll(ext, shift, 0)
            shift *= 2
        cnt = jnp.minimum(pos + 1, w).astype(F32)
        pooled = ext[POOL_HALO:] / cnt - ug
        mixed.append(jnp.dot(pooled.astype(BF16), wmix_ref[gi], preferred_element_type=F32))
    pool_out = (jnp.concatenate(mixed, axis=-1) * pscale_ref[...]).astype(BF16)

    y_a = jnp.dot(attn_ref[...], wao_ref[...], preferred_element_type=F32)
    y_p = jnp.dot(pool_out, wpo_ref[...], preferred_element_type=F32)
    gates = gate_ref[...]
    merged = gates[:, :D_MODEL].astype(F32) * y_a + gates[:, D_MODEL:].astype(F32) * y_p
    x_new = x_ref[...] + jnp.dot(merged.astype(BF16), wo_ref[...], preferred_element_type=F32)
    xo_ref[...] = x_new

    ms = jnp.mean(x_new * x_new, axis=-1, keepdims=True)
    h2 = x_new * lax.rsqrt(ms + RMS_EPS) * g2_ref[...]
    h_hi = h2.astype(BF16)
    h2_ref[...] = _pack_bf16_pairs(h2)

    h_lo = (h2 - h_hi.astype(F32)).astype(BF16)
    nt = (((1,), (1,)), ((), ()))
    both = lax.dot_general(wr_ref[...], h_hi, nt, preferred_element_type=F32)
    cross = lax.dot_general(wr_ref[:ROUTER_ROWS, :], h_lo, nt, preferred_element_type=F32)
    logits = both[:ROUTER_ROWS] + both[ROUTER_ROWS:] + cross + jnp.concatenate(
        [br_ref[...]] * (tm // LANES), axis=1)

    sub = lax.broadcasted_iota(jnp.int32, (8, tm), 0)
    npg = EXPERTS_PER_GROUP

    def first_max(vals):
        vmax = jnp.max(vals, axis=0, keepdims=True)
        return vmax, jnp.min(jnp.where(vals == vmax, sub, npg), axis=0, keepdims=True)

    def of_group(parts, g_sel):
        out = parts[-1]
        for g in range(N_EXPERT_GROUPS - 2, -1, -1):
            out = jnp.where(g_sel == g, parts[g], out)
        return out

    glog = jnp.where(sub < N_EXPERT_GROUPS, logits[0:8], -jnp.inf)
    gmax, g_sel = first_max(glog)
    g_gate = 1.0 / jnp.sum(jnp.exp(glog - gmax), axis=0, keepdims=True)

    elog = of_group([logits[npg * (g + 1):npg * (g + 2)] for g in range(N_EXPERT_GROUPS)], g_sel)
    e_exp = jnp.exp(elog - jnp.max(elog, axis=0, keepdims=True))
    prob = e_exp / jnp.sum(e_exp, axis=0, keepdims=True)
    p1, i1 = first_max(prob)
    p2, i2 = first_max(jnp.where(sub == i1, -1.0, prob))
    scale = g_gate / (p1 + p2)

    chosen = (sub == i1) | (sub == i2)
    onehot = jnp.concatenate(
        [jnp.where(chosen & (g_sel == g), 1.0, 0.0) for g in range(N_EXPERT_GROUPS)],
        axis=0).astype(BF16)
    t_row = lax.broadcasted_iota(jnp.int32, (tm, tm), 0)
    t_col = lax.broadcasted_iota(jnp.int32, (tm, tm), 1)
    before = jnp.dot(onehot, jnp.where(t_row < t_col, 1.0, 0.0).astype(BF16),
                     preferred_element_type=F32)
    total = jnp.dot(onehot, jnp.ones((tm, LANES), BF16), preferred_element_type=F32)
    seen = cnt_ref[...]
    slot = before + jnp.concatenate([seen] * (tm // LANES), axis=1)
    cnt_ref[...] = seen + total
    slot = of_group([slot[npg * g:npg * (g + 1)] for g in range(N_EXPERT_GROUPS)], g_sel)
    r1 = jnp.sum(jnp.where(sub == i1, slot, 0.0), axis=0, keepdims=True).astype(jnp.int32)
    r2 = jnp.sum(jnp.where(sub == i2, slot, 0.0), axis=0, keepdims=True).astype(jnp.int32)

    e_base = g_sel * npg
    ri_ref[...] = jnp.where(sub == 0, e_base + i1,
                            jnp.where(sub == 1, e_base + i2,
                                      jnp.where(sub == 2, r1, jnp.where(sub == 3, r2, 0))))
    rf_ref[...] = jnp.where(sub == 0, p1 * scale, jnp.where(sub == 1, p2 * scale, 0.0))


ROUTER_ROWS = 8 + N_EXPERTS + 8


def _router_operands(w_group, b_group, w_expert, b_expert):
    d = w_group.shape[0]
    wt = jnp.zeros((ROUTER_ROWS, d), F32)
    wt = wt.at[:N_EXPERT_GROUPS].set(w_group.T)
    wt = wt.at[8:8 + N_EXPERTS].set(w_expert.transpose(0, 2, 1).reshape(N_EXPERTS, d))
    hi = wt.astype(BF16)
    lo = (wt - hi.astype(F32)).astype(BF16)
    b = jnp.zeros((ROUTER_ROWS,), F32).at[:N_EXPERT_GROUPS].set(b_group)
    b = b.at[8:8 + N_EXPERTS].set(b_expert.reshape(-1))
    return jnp.concatenate([hi, lo], axis=0), jnp.broadcast_to(b[:, None], (ROUTER_ROWS, LANES))


def _post(x, attn, u, gates, wao, wmix, pscale, wpo, wo, g2, wr, br):
    n = x.shape[0]
    tm = ROW_TILE
    row = lambda i: (i, 0)
    const = lambda i: (0, 0)
    halo_blocks = tm // POOL_HALO
    return pl.pallas_call(
        _post_kernel,
        grid=(n // tm,),
        in_specs=[
            pl.BlockSpec((tm, D_MODEL), row),
            pl.BlockSpec((tm, ATTN_WIDTH), row),
            pl.BlockSpec((tm, POOL_WIDTH), row),
            pl.BlockSpec((POOL_HALO, POOL_WIDTH),
                         lambda i: (jnp.maximum(i * halo_blocks - 1, 0), 0)),
            pl.BlockSpec((tm, 2 * D_MODEL), row),
            pl.BlockSpec((ATTN_WIDTH, D_MODEL), const),
            pl.BlockSpec((len(POOL_WINDOWS), POOL_GROUP_DIM, POOL_GROUP_DIM), lambda i: (0, 0, 0)),
            pl.BlockSpec((1, POOL_WIDTH), const),
            pl.BlockSpec((POOL_WIDTH, D_MODEL), const),
            pl.BlockSpec((D_MODEL, D_MODEL), const),
            pl.BlockSpec((1, D_MODEL), const),
            pl.BlockSpec((2 * ROUTER_ROWS, D_MODEL), const),
            pl.BlockSpec((ROUTER_ROWS, LANES), const),
        ],
        out_specs=[
            pl.BlockSpec((tm, D_MODEL), row),
            pl.BlockSpec((tm, D_MODEL // 2), row),
            pl.BlockSpec((8, tm), lambda i: (0, i)),
            pl.BlockSpec((8, tm), lambda i: (0, i)),
            pl.BlockSpec((N_EXPERTS, LANES), const),
        ],
        out_shape=[
            jax.ShapeDtypeStruct((n, D_MODEL), F32),
            jax.ShapeDtypeStruct((n, D_MODEL // 2), jnp.uint32),
            jax.ShapeDtypeStruct((8, n), jnp.int32),
            jax.ShapeDtypeStruct((8, n), F32),
            jax.ShapeDtypeStruct((N_EXPERTS, LANES), F32),
        ],
        compiler_params=_params(1),
        name="post_attn_router",
    )(x, attn, u, u, gates, wao, wmix, pscale, wpo, wo, g2, wr, br)


def _experts_kernel(be_ref, nused_ref, x_ref, wg_ref, wu_ref, wd_ref, y_ref, wg_bf, wu_bf, wd_bf):
    i = pl.program_id(0)
    changed = (i == 0) | (be_ref[i] != be_ref[jnp.maximum(i - 1, 0)])
    used = i < nused_ref[0]

    @pl.when(changed & used)
    def _():
        wg_bf[...] = wg_ref[...].astype(BF16)
        wu_bf[...] = wu_ref[...].astype(BF16)
        wd_bf[...] = wd_ref[...].astype(BF16)

    @pl.when(used)
    def _():
        x = _unpack_bf16_pairs(x_ref[...])
        a = jnp.dot(x, wg_bf[...], preferred_element_type=F32)
        b = jnp.dot(x, wu_bf[...], preferred_element_type=F32)
        mid = (a * jax.nn.sigmoid(a) * b).astype(BF16)
        y_ref[...] = jnp.dot(mid, wd_bf[...], preferred_element_type=F32)

    @pl.when(jnp.logical_not(used))
    def _():
        y_ref[...] = jnp.zeros_like(y_ref)


def _experts(block_expert, n_used, x_rows, w_gate, w_up, w_down, layer):
    n_rows = x_rows.shape[0]
    bm = EXPERT_BLOCK
    xmap = lambda i, be, nu: (jnp.maximum(jnp.minimum(i, nu[0] - 1), 0), 0)
    wmap = lambda i, be, nu: (layer, be[i], 0, 0)
    grid_spec = pltpu.PrefetchScalarGridSpec(
        num_scalar_prefetch=2,
        grid=(n_rows // bm,),
        in_specs=[
            pl.BlockSpec((bm, D_MODEL // 2), xmap),
            pl.BlockSpec((None, None, D_MODEL, EXPERT_HIDDEN), wmap),
            pl.BlockSpec((None, None, D_MODEL, EXPERT_HIDDEN), wmap),
            pl.BlockSpec((None, None, EXPERT_HIDDEN, D_MODEL), wmap),
        ],
        out_specs=pl.BlockSpec((bm, D_MODEL), lambda i, be, nu: (i, 0)),
        scratch_shapes=[
            pltpu.VMEM((D_MODEL, EXPERT_HIDDEN), BF16),
            pltpu.VMEM((D_MODEL, EXPERT_HIDDEN), BF16),
            pltpu.VMEM((EXPERT_HIDDEN, D_MODEL), BF16),
        ],
    )
    return pl.pallas_call(
        _experts_kernel,
        grid_spec=grid_spec,
        out_shape=jax.ShapeDtypeStruct((n_rows, D_MODEL), F32),
        compiler_params=_params(1),
        name="experts",
    )(block_expert, n_used, x_rows, w_gate, w_up, w_down)


def _dispatch_kernel(dest_ref, h_ref, init_ref, rows_ref, sem):
    del init_ref
    tm = h_ref.shape[0]
    base = pl.program_id(0) * tm
    n_tok = pl.num_programs(0) * tm

    def row_copy(r, d):
        return pltpu.make_async_copy(h_ref.at[pl.ds(r, 1)], rows_ref.at[pl.ds(d, 1)], sem)

    def issue(r, carry):
        for k in range(TOP_K):
            row_copy(r, dest_ref[k * n_tok + base + r]).start()
        return carry

    def drain(r, carry):
        for k in range(TOP_K):
            row_copy(r, dest_ref[k * n_tok + base + r]).wait()
        return carry

    lax.fori_loop(0, tm, issue, 0, unroll=8)
    lax.fori_loop(0, tm, drain, 0, unroll=8)


def _dispatch(dest, h2_packed, buf_len):
    n, width = h2_packed.shape
    tm = ROW_TILE
    grid_spec = pltpu.PrefetchScalarGridSpec(
        num_scalar_prefetch=1,
        grid=(n // tm,),
        in_specs=[
            pl.BlockSpec((tm, width), lambda i, dest: (i, 0)),
            pl.BlockSpec(memory_space=pl.ANY),
        ],
        out_specs=pl.BlockSpec(memory_space=pl.ANY),
        scratch_shapes=[pltpu.SemaphoreType.DMA(())],
    )
    return pl.pallas_call(
        _dispatch_kernel,
        grid_spec=grid_spec,
        out_shape=jax.ShapeDtypeStruct((buf_len, width), h2_packed.dtype),
        input_output_aliases={2: 0},
        compiler_params=_params(1),
        name="moe_dispatch",
    )(dest, h2_packed, jnp.zeros((buf_len, width), h2_packed.dtype))


def _combine_kernel(dest_ref, x_ref, gate_ref, y_ref, out_ref, ybuf, sems):
    tm = x_ref.shape[0]
    i = pl.program_id(0)
    cur = lax.rem(i, 2)

    n_tok = pl.num_programs(0) * tm

    def row_copy(step, slot, r, k):
        d = dest_ref[k * n_tok + step * tm + r]
        return pltpu.make_async_copy(y_ref.at[pl.ds(d, 1)], ybuf.at[slot, k, pl.ds(r, 1)],
                                     sems.at[slot])

    def fetch(step, slot):
        def issue(r, carry):
            for k in range(TOP_K):
                row_copy(step, slot, r, k).start()
            return carry
        lax.fori_loop(0, tm, issue, 0, unroll=8)

    @pl.when(i == 0)
    def _():
        fetch(0, 0)

    @pl.when(i + 1 < pl.num_programs(0))
    def _():
        fetch(i + 1, 1 - cur)

    def drain(r, carry):
        for k in range(TOP_K):
            row_copy(i, cur, r, k).wait()
        return carry
    lax.fori_loop(0, tm, drain, 0, unroll=8)

    g = gate_ref[...]
    out_ref[...] = (x_ref[...] + g[:, 0:1] * ybuf[cur, 0] + g[:, 1:2] * ybuf[cur, 1])


def _combine(dest, x_new, gates, y_rows):
    n, d = x_new.shape
    tm = ROW_TILE
    grid_spec = pltpu.PrefetchScalarGridSpec(
        num_scalar_prefetch=1,
        grid=(n // tm,),
        in_specs=[
            pl.BlockSpec((tm, d), lambda i, dest: (i, 0)),
            pl.BlockSpec((tm, LANES), lambda i, dest: (i, 0)),
            pl.BlockSpec(memory_space=pl.ANY),
        ],
        out_specs=pl.BlockSpec((tm, d), lambda i, dest: (i, 0)),
        scratch_shapes=[
            pltpu.VMEM((2, TOP_K, tm, d), F32),
            pltpu.SemaphoreType.DMA((2,)),
        ],
    )
    return pl.pallas_call(
        _combine_kernel,
        grid_spec=grid_spec,
        out_shape=jax.ShapeDtypeStruct((n, d), F32),
        compiler_params=_params(1),
        name="moe_combine",
    )(dest, x_new, gates, y_rows)


def _dispatch_plan(expert_id, slot, counts):
    bm = EXPERT_BLOCK
    n_assign = expert_id.size
    padded = (counts + bm - 1) // bm * bm
    pend = jnp.cumsum(padded)
    pstart = pend - padded
    experts = jnp.arange(N_EXPERTS, dtype=jnp.int32)
    seg_start = jnp.sum(jnp.where(expert_id[..., None] == experts, pstart, 0), axis=-1)
    dest = (seg_start + slot).reshape(-1).astype(jnp.int32)
    buf_len = n_assign + N_EXPERTS * bm
    block_start = jnp.arange(buf_len // bm, dtype=jnp.int32) * bm
    n_used = (pend[-1:] // bm).astype(jnp.int32)
    block_start = jnp.minimum(block_start, pend[-1] - bm)
    block_expert = jnp.sum((pend[None, :] <= block_start[:, None]).astype(jnp.int32), axis=1)
    return dest, jnp.minimum(block_expert, N_EXPERTS - 1), n_used, buf_len


def kernel(x, norm1_g, w_in, q_norm_g, k_norm_g, w_attn_out, w_pool_mix, pool_scale, w_pool_out,
           w_o, norm2_g, w_router_group, b_router_group, w_router_expert, b_router_expert,
           w_exp_gate, w_exp_up, w_exp_down):
    batch, seq, d = x.shape
    assert (seq, d) == (SEQ, D_MODEL)
    n_tok = batch * seq
    depth = w_in.shape[0]
    half = ATTN_WIDTH // 2
    hsum = (jnp.arange(half)[:, None] // HEAD_DIM == jnp.arange(half)[None, :] // HEAD_DIM).astype(BF16)

    xf = x.reshape(n_tok, d)
    for layer in range(depth):
        qg = jnp.tile(q_norm_g[layer], N_HEADS)[None, :]
        kg = jnp.tile(k_norm_g[layer], N_HEADS)[None, :]
        q, k, v, u, gates = _inproj(xf, norm1_g[layer][None, :], w_in[layer].astype(BF16), qg, kg, hsum)
        attn = _attn(q.reshape(batch, seq, ATTN_WIDTH), k.reshape(batch, seq, ATTN_WIDTH),
                     v.reshape(batch, seq, ATTN_WIDTH), batch).reshape(n_tok, ATTN_WIDTH)

        w_router, b_router = _router_operands(
            w_router_group[layer], b_router_group[layer], w_router_expert[layer], b_router_expert[layer])
        x_new, h2, route_i, route_f, counts = _post(
            xf, attn, u, gates, w_attn_out[layer].astype(BF16), w_pool_mix[layer].astype(BF16),
            pool_scale[layer][None, :], w_pool_out[layer].astype(BF16), w_o[layer].astype(BF16),
            norm2_g[layer][None, :], w_router, b_router)

        dest, block_expert, n_used, buf_len = _dispatch_plan(
            route_i[:TOP_K], route_i[TOP_K:2 * TOP_K], counts[:, 0].astype(jnp.int32))
        x_rows = _dispatch(dest, h2, buf_len)
        y_rows = _experts(block_expert, n_used, x_rows, w_exp_gate, w_exp_up, w_exp_down, layer)
        gate_rows = jnp.pad(route_f[:TOP_K].T, ((0, 0), (0, LANES - TOP_K)))
        xf = _combine(dest, x_new, gate_rows, y_rows)
    return xf.reshape(batch, seq, d)
```

```python
import jax
import jax.numpy as jnp
import numpy as np
from jax import lax
from jax.experimental import pallas as pl
from jax.experimental.pallas import tpu as pltpu

D_MODEL = 1024
SEQ = 4096
N_HEADS = 8
HEAD_DIM = 64
ATTN_WIDTH = N_HEADS * HEAD_DIM
DILATIONS = (1, 4, 16)
ATTN_BLOCK = 128
POOL_WINDOWS = (2, 4, 8, 16)
POOL_GROUP_DIM = 128
POOL_WIDTH = len(POOL_WINDOWS) * POOL_GROUP_DIM
POOL_HALO = 16
IN_PROJ_WIDTH = 3 * ATTN_WIDTH + POOL_WIDTH + 2 * D_MODEL
N_EXPERT_GROUPS = 4
EXPERTS_PER_GROUP = 8
N_EXPERTS = N_EXPERT_GROUPS * EXPERTS_PER_GROUP
TOP_K = 2
EXPERT_HIDDEN = 512
RMS_EPS = 1e-6
NEG_INF = -1e30

LANES = 128
ROW_TILE = 256
EXPERT_BLOCK = 256
VMEM_LIMIT = 48 * 1024 * 1024
ATTN_VMEM_LIMIT = 56 * 1024 * 1024

F32 = jnp.float32
BF16 = jnp.bfloat16


def _params(n_axes):
    return pltpu.CompilerParams(
        dimension_semantics=("arbitrary",) * n_axes, vmem_limit_bytes=VMEM_LIMIT)


def _inproj_kernel(x_ref, g_ref, w_ref, qg_ref, kg_ref, hsum_ref,
                   q_ref, k_ref, v_ref, u_ref, gate_ref):
    x = x_ref[...]
    ms = jnp.mean(x * x, axis=-1, keepdims=True)
    h = (x * lax.rsqrt(ms + RMS_EPS) * g_ref[...]).astype(BF16)

    def proj(lo, hi):
        return jnp.dot(h, w_ref[:, lo:hi], preferred_element_type=F32)

    def head_norm(t, gain):
        sq = (t * t).astype(BF16)
        half = ATTN_WIDTH // 2
        ssq = jnp.concatenate(
            [jnp.dot(sq[:, j * half:(j + 1) * half], hsum_ref[...], preferred_element_type=F32)
             for j in range(2)], axis=-1)
        return t * lax.rsqrt(ssq * (1.0 / HEAD_DIM) + RMS_EPS) * gain

    w = ATTN_WIDTH
    q_ref[...] = head_norm(proj(0, w), qg_ref[...])
    k_ref[...] = head_norm(proj(w, 2 * w), kg_ref[...])
    v_ref[...] = proj(2 * w, 3 * w)
    u_ref[...] = proj(3 * w, 3 * w + POOL_WIDTH)
    base = 3 * w + POOL_WIDTH
    for j in range(2 * D_MODEL // 512):
        gate_ref[:, j * 512:(j + 1) * 512] = jax.nn.sigmoid(
            proj(base + j * 512, base + (j + 1) * 512)).astype(BF16)


def _inproj(x, g1, w_in, qg, kg, hsum):
    n = x.shape[0]
    row = lambda i: (i, 0)
    const = lambda i: (0, 0)
    return pl.pallas_call(
        _inproj_kernel,
        grid=(n // ROW_TILE,),
        in_specs=[
            pl.BlockSpec((ROW_TILE, D_MODEL), row),
            pl.BlockSpec((1, D_MODEL), const),
            pl.BlockSpec((D_MODEL, IN_PROJ_WIDTH), const),
            pl.BlockSpec((1, ATTN_WIDTH), const),
            pl.BlockSpec((1, ATTN_WIDTH), const),
            pl.BlockSpec((ATTN_WIDTH // 2, ATTN_WIDTH // 2), const),
        ],
        out_specs=[
            pl.BlockSpec((ROW_TILE, ATTN_WIDTH), row),
            pl.BlockSpec((ROW_TILE, ATTN_WIDTH), row),
            pl.BlockSpec((ROW_TILE, ATTN_WIDTH), row),
            pl.BlockSpec((ROW_TILE, POOL_WIDTH), row),
            pl.BlockSpec((ROW_TILE, 2 * D_MODEL), row),
        ],
        out_shape=[
            jax.ShapeDtypeStruct((n, ATTN_WIDTH), F32),
            jax.ShapeDtypeStruct((n, ATTN_WIDTH), F32),
            jax.ShapeDtypeStruct((n, ATTN_WIDTH), F32),
            jax.ShapeDtypeStruct((n, POOL_WIDTH), F32),
            jax.ShapeDtypeStruct((n, 2 * D_MODEL), BF16),
        ],
        compiler_params=_params(1),
        name="inproj",
    )(x, g1, w_in, qg, kg, hsum)


QUAD = 4
QUAD_ROWS = SEQ // QUAD
LOG2E = 1.4426950408889634


def _attn_bias():
    blk = ATTN_BLOCK
    r = np.arange(2 * blk) % blk
    c = np.arange(2 * blk)
    per_q, per_k = blk // QUAD, 2 * blk // QUAD
    tq = QUAD * (r % per_q) + r // per_q
    tk = QUAD * (c % per_k) + c // per_k
    masks = []
    for tq_, tk_ in ((tq, tk), (r, c)):
        first = tq_[:, None] - tk_[None, :]
        later = first + blk
        masks += [first >= 0, (later >= 0) & (later <= blk)]
    return np.where(np.stack(masks), 0.0, NEG_INF).astype(np.float32)


def _attn_kernel(q_ref, k_ref, v_ref, bias_ref, out_ref, q4, k4, v4,
                 o0, o1, o2, m0, m1, m2, d0, d1, d2):
    blk = ATTN_BLOCK
    head_a = lax.broadcasted_iota(jnp.int32, (blk, LANES), 1) < HEAD_DIM
    ones_cols = jnp.ones((2 * blk, LANES), BF16)

    for c in range(QUAD):
        dst = pl.ds(c * QUAD_ROWS, QUAD_ROWS)
        src = pl.ds(c, QUAD_ROWS, stride=QUAD)
        q4[dst, :] = q_ref[src, :] * (HEAD_DIM ** -0.5 * LOG2E)
        k4[dst, :] = k_ref[src, :]
        v4[dst, :] = v_ref[src, :]

    def attend(q2, k2, v2, bias):
        qs = jnp.concatenate(
            [jnp.where(head_a, q2, 0.0), jnp.where(head_a, 0.0, q2)], axis=0).astype(BF16)
        s = lax.dot_general(qs, k2.astype(BF16), (((1,), (1,)), ((), ())),
                            preferred_element_type=F32) + bias
        m = jnp.max(s, axis=-1, keepdims=True)
        e = jnp.exp2(s - m).astype(BF16)
        r = jnp.dot(e, jnp.concatenate([v2.astype(BF16), ones_cols], axis=1),
                    preferred_element_type=F32)
        mb = jnp.broadcast_to(m, (2 * blk, LANES))
        return (jnp.where(head_a, r[:blk, :LANES], r[blk:, :LANES]),
                jnp.where(head_a, mb[:blk], mb[blk:]),
                jnp.where(head_a, r[:blk, LANES:], r[blk:, LANES:]))

    def load(ref, pieces):
        return jnp.concatenate([ref[p, :] for p in pieces], axis=0)

    def store(refs, pieces, vals):
        for ref, val in zip(refs, vals):
            row = 0
            for p in pieces:
                ref[p, :] = val[row:row + p.size]
                row += p.size

    def body1(nb, carry):
        kb = jnp.maximum(nb - 1, 0)
        per_q, per_k = blk // QUAD, 2 * blk // QUAD
        qp = [pl.ds(pl.multiple_of(c * QUAD_ROWS + nb * per_q, per_q), per_q) for c in range(QUAD)]
        kp = [pl.ds(pl.multiple_of(c * QUAD_ROWS + kb * per_q, per_q), per_k) for c in range(QUAD)]
        res = attend(load(q4, qp), load(k4, kp), load(v4, kp), bias_ref[jnp.minimum(nb, 1)])
        store((o0, m0, d0), qp, res)
        return carry

    lax.fori_loop(0, SEQ // blk, body1, 0, unroll=16)

    n_blk4 = QUAD_ROWS // blk

    def body4(idx, carry):
        nb = idx & (n_blk4 - 1)
        base = (idx - nb) * blk
        kb = jnp.maximum(nb - 1, 0)
        qp = [pl.ds(pl.multiple_of(base + nb * blk, blk), blk)]
        kp = [pl.ds(pl.multiple_of(base + kb * blk, blk), 2 * blk)]
        res = attend(load(q4, qp), load(k4, kp), load(v4, kp), bias_ref[2 + jnp.minimum(nb, 1)])
        store((o1, m1, d1), qp, res)
        return carry

    lax.fori_loop(0, SEQ // blk, body4, 0, unroll=16)

    def body16(idx, carry):
        start = (idx & (QUAD - 1)) * QUAD_ROWS + lax.shift_right_logical(idx, 2)
        kp = [pl.ds(start, 2 * blk, stride=QUAD)]
        k2, v2 = load(k4, kp), load(v4, kp)
        for nb in range(2):
            qp = [pl.ds(start + nb * blk * QUAD, blk, stride=QUAD)]
            store((o2, m2, d2), qp, attend(load(q4, qp), k2, v2, bias_ref[2 + nb]))
        return carry

    lax.fori_loop(0, SEQ // (2 * blk), body16, 0, unroll=8)

    chunk = 512

    def mix(i, carry):
        r = pl.ds(pl.multiple_of(i * chunk, chunk), chunk)
        ma, mb, mc = m0[r, :], m1[r, :], m2[r, :]
        m = jnp.maximum(jnp.maximum(ma, mb), mc)
        wa, wb, wc = jnp.exp2(ma - m), jnp.exp2(mb - m), jnp.exp2(mc - m)
        acc = wa * o0[r, :] + wb * o1[r, :] + wc * o2[r, :]
        den = wa * d0[r, :] + wb * d1[r, :] + wc * d2[r, :]
        per_class = QUAD_ROWS // chunk
        c = i // per_class
        n0 = (i - c * per_class) * chunk
        out_ref[pl.ds(QUAD * n0 + c, chunk, stride=QUAD), :] = acc / den
        return carry

    lax.fori_loop(0, SEQ // chunk, mix, 0)


def _attn(q, k, v, batch):
    spec = pl.BlockSpec((None, SEQ, LANES), lambda b, hp: (b, 0, hp))
    bias = _attn_bias()
    return pl.pallas_call(
        _attn_kernel,
        grid=(batch, ATTN_WIDTH // LANES),
        in_specs=[spec, spec, spec, pl.BlockSpec(bias.shape, lambda b, hp: (0, 0, 0))],
        out_specs=spec,
        out_shape=jax.ShapeDtypeStruct((batch, SEQ, ATTN_WIDTH), F32),
        scratch_shapes=[pltpu.VMEM((SEQ, LANES), F32) for _ in range(12)],
        compiler_params=pltpu.CompilerParams(
            dimension_semantics=("arbitrary", "arbitrary"), vmem_limit_bytes=ATTN_VMEM_LIMIT),
        name="dilated_attn",
    )(q, k, v, jnp.asarray(bias))


def _pack_bf16_pairs(x):
    c = x.shape[1] // 2
    bits = pltpu.bitcast(x.astype(BF16).astype(F32), jnp.uint32)
    return lax.shift_right_logical(bits[:, :c], jnp.uint32(16)) | bits[:, c:]


def _unpack_bf16_pairs(w):
    lo = pltpu.bitcast(lax.shift_left(w, jnp.uint32(16)), F32)
    hi = pltpu.bitcast(w & jnp.uint32(0xFFFF0000), F32)
    return jnp.concatenate([lo, hi], axis=1).astype(BF16)


def _post_kernel(x_ref, attn_ref, u_ref, halo_ref, gate_ref, wao_ref, wmix_ref, pscale_ref,
                 wpo_ref, wo_ref, g2_ref, wr_ref, br_ref,
                 xo_ref, h2_ref, ri_ref, rf_ref, cnt_ref):
    tm = x_ref.shape[0]

    @pl.when(pl.program_id(0) == 0)
    def _():
        cnt_ref[...] = jnp.zeros_like(cnt_ref)

    pos0 = lax.rem(pl.program_id(0) * tm, SEQ)
    pos = pos0 + lax.broadcasted_iota(jnp.int32, (tm, 1), 0)
    u = u_ref[...]
    halo = halo_ref[...] * (pos0 > 0).astype(F32)

    mixed = []
    for gi, w in enumerate(POOL_WINDOWS):
        lo = gi * POOL_GROUP_DIM
        ug = u[:, lo:lo + POOL_GROUP_DIM]
        ext = jnp.concatenate([halo[:, lo:lo + POOL_GROUP_DIM], ug], axis=0)
        shift = 1
        while shift < w:
            ext = ext + pltpu.roll(ext, shift, 0)
            shift *= 2
        cnt = jnp.minimum(pos + 1, w).astype(F32)
        pooled = ext[POOL_HALO:] / cnt - ug
        mixed.append(jnp.dot(pooled.astype(BF16), wmix_ref[gi], preferred_element_type=F32))
    pool_out = (jnp.concatenate(mixed, axis=-1) * pscale_ref[...]).astype(BF16)

    y_a = jnp.dot(attn_ref[...].astype(BF16), wao_ref[...], preferred_element_type=F32)
    y_p = jnp.dot(pool_out, wpo_ref[...], preferred_element_type=F32)
    gates = gate_ref[...]
    merged = gates[:, :D_MODEL].astype(F32) * y_a + gates[:, D_MODEL:].astype(F32) * y_p
    x_new = x_ref[...] + jnp.dot(merged.astype(BF16), wo_ref[...], preferred_element_type=F32)
    xo_ref[...] = x_new

    ms = jnp.mean(x_new * x_new, axis=-1, keepdims=True)
    h2 = x_new * lax.rsqrt(ms + RMS_EPS) * g2_ref[...]
    h_hi = h2.astype(BF16)
    h2_ref[...] = _pack_bf16_pairs(h2)

    h_lo = (h2 - h_hi.astype(F32)).astype(BF16)
    nt = (((1,), (1,)), ((), ()))
    both = lax.dot_general(wr_ref[...], h_hi, nt, preferred_element_type=F32)
    cross = lax.dot_general(wr_ref[:ROUTER_ROWS, :], h_lo, nt, preferred_element_type=F32)
    logits = both[:ROUTER_ROWS] + both[ROUTER_ROWS:] + cross + jnp.concatenate(
        [br_ref[...]] * (tm // LANES), axis=1)

    sub = lax.broadcasted_iota(jnp.int32, (8, tm), 0)
    npg = EXPERTS_PER_GROUP

    def first_max(vals):
        vmax = jnp.max(vals, axis=0, keepdims=True)
        return vmax, jnp.min(jnp.where(vals == vmax, sub, npg), axis=0, keepdims=True)

    def of_group(parts, g_sel):
        out = parts[-1]
        for g in range(N_EXPERT_GROUPS - 2, -1, -1):
            out = jnp.where(g_sel == g, parts[g], out)
        return out

    glog = jnp.where(sub < N_EXPERT_GROUPS, logits[0:8], -jnp.inf)
    gmax, g_sel = first_max(glog)
    g_gate = 1.0 / jnp.sum(jnp.exp(glog - gmax), axis=0, keepdims=True)

    elog = of_group([logits[npg * (g + 1):npg * (g + 2)] for g in range(N_EXPERT_GROUPS)], g_sel)
    e_exp = jnp.exp(elog - jnp.max(elog, axis=0, keepdims=True))
    prob = e_exp / jnp.sum(e_exp, axis=0, keepdims=True)
    p1, i1 = first_max(prob)
    p2, i2 = first_max(jnp.where(sub == i1, -1.0, prob))
    scale = g_gate / (p1 + p2)

    chosen = (sub == i1) | (sub == i2)
    onehot = jnp.concatenate(
        [jnp.where(chosen & (g_sel == g), 1.0, 0.0) for g in range(N_EXPERT_GROUPS)],
        axis=0).astype(BF16)
    t_row = lax.broadcasted_iota(jnp.int32, (tm, tm), 0)
    t_col = lax.broadcasted_iota(jnp.int32, (tm, tm), 1)
    before = jnp.dot(onehot, jnp.where(t_row < t_col, 1.0, 0.0).astype(BF16),
                     preferred_element_type=F32)
    total = jnp.dot(onehot, jnp.ones((tm, LANES), BF16), preferred_element_type=F32)
    seen = cnt_ref[...]
    slot = before + jnp.concatenate([seen] * (tm // LANES), axis=1)
    cnt_ref[...] = seen + total
    slot = of_group([slot[npg * g:npg * (g + 1)] for g in range(N_EXPERT_GROUPS)], g_sel)
    r1 = jnp.sum(jnp.where(sub == i1, slot, 0.0), axis=0, keepdims=True).astype(jnp.int32)
    r2 = jnp.sum(jnp.where(sub == i2, slot, 0.0), axis=0, keepdims=True).astype(jnp.int32)

    e_base = g_sel * npg
    ri_ref[...] = jnp.where(sub == 0, e_base + i1,
                            jnp.where(sub == 1, e_base + i2,
                                      jnp.where(sub == 2, r1, jnp.where(sub == 3, r2, 0))))
    rf_ref[...] = jnp.where(sub == 0, p1 * scale, jnp.where(sub == 1, p2 * scale, 0.0))


ROUTER_ROWS = 8 + N_EXPERTS + 8


def _router_operands(w_group, b_group, w_expert, b_expert):
    d = w_group.shape[0]
    wt = jnp.zeros((ROUTER_ROWS, d), F32)
    wt = wt.at[:N_EXPERT_GROUPS].set(w_group.T)
    wt = wt.at[8:8 + N_EXPERTS].set(w_expert.transpose(0, 2, 1).reshape(N_EXPERTS, d))
    hi = wt.astype(BF16)
    lo = (wt - hi.astype(F32)).astype(BF16)
    b = jnp.zeros((ROUTER_ROWS,), F32).at[:N_EXPERT_GROUPS].set(b_group)
    b = b.at[8:8 + N_EXPERTS].set(b_expert.reshape(-1))
    return jnp.concatenate([hi, lo], axis=0), jnp.broadcast_to(b[:, None], (ROUTER_ROWS, LANES))


def _post(x, attn, u, gates, wao, wmix, pscale, wpo, wo, g2, wr, br):
    n = x.shape[0]
    tm = ROW_TILE
    row = lambda i: (i, 0)
    const = lambda i: (0, 0)
    halo_blocks = tm // POOL_HALO
    return pl.pallas_call(
        _post_kernel,
        grid=(n // tm,),
        in_specs=[
            pl.BlockSpec((tm, D_MODEL), row),
            pl.BlockSpec((tm, ATTN_WIDTH), row),
            pl.BlockSpec((tm, POOL_WIDTH), row),
            pl.BlockSpec((POOL_HALO, POOL_WIDTH),
                         lambda i: (jnp.maximum(i * halo_blocks - 1, 0), 0)),
            pl.BlockSpec((tm, 2 * D_MODEL), row),
            pl.BlockSpec((ATTN_WIDTH, D_MODEL), const),
            pl.BlockSpec((len(POOL_WINDOWS), POOL_GROUP_DIM, POOL_GROUP_DIM), lambda i: (0, 0, 0)),
            pl.BlockSpec((1, POOL_WIDTH), const),
            pl.BlockSpec((POOL_WIDTH, D_MODEL), const),
            pl.BlockSpec((D_MODEL, D_MODEL), const),
            pl.BlockSpec((1, D_MODEL), const),
            pl.BlockSpec((2 * ROUTER_ROWS, D_MODEL), const),
            pl.BlockSpec((ROUTER_ROWS, LANES), const),
        ],
        out_specs=[
            pl.BlockSpec((tm, D_MODEL), row),
            pl.BlockSpec((tm, D_MODEL // 2), row),
            pl.BlockSpec((8, tm), lambda i: (0, i)),
            pl.BlockSpec((8, tm), lambda i: (0, i)),
            pl.BlockSpec((N_EXPERTS, LANES), const),
        ],
        out_shape=[
            jax.ShapeDtypeStruct((n, D_MODEL), F32),
            jax.ShapeDtypeStruct((n, D_MODEL // 2), jnp.uint32),
            jax.ShapeDtypeStruct((8, n), jnp.int32),
            jax.ShapeDtypeStruct((8, n), F32),
            jax.ShapeDtypeStruct((N_EXPERTS, LANES), F32),
        ],
        compiler_params=_params(1),
        name="post_attn_router",
    )(x, attn, u, u, gates, wao, wmix, pscale, wpo, wo, g2, wr, br)


def _experts_kernel(be_ref, nused_ref, x_ref, wg_ref, wu_ref, wd_ref, y_ref, wg_bf, wu_bf, wd_bf):
    i = pl.program_id(0)
    changed = (i == 0) | (be_ref[i] != be_ref[jnp.maximum(i - 1, 0)])
    used = i < nused_ref[0]

    @pl.when(changed & used)
    def _():
        wg_bf[...] = wg_ref[...].astype(BF16)
        wu_bf[...] = wu_ref[...].astype(BF16)
        wd_bf[...] = wd_ref[...].astype(BF16)

    @pl.when(used)
    def _():
        x = _unpack_bf16_pairs(x_ref[...])
        a = jnp.dot(x, wg_bf[...], preferred_element_type=F32)
        b = jnp.dot(x, wu_bf[...], preferred_element_type=F32)
        mid = (a * jax.nn.sigmoid(a) * b).astype(BF16)
        y_ref[...] = jnp.dot(mid, wd_bf[...], preferred_element_type=F32)

    @pl.when(jnp.logical_not(used))
    def _():
        y_ref[...] = jnp.zeros_like(y_ref)


def _experts(block_expert, n_used, x_rows, w_gate, w_up, w_down, layer):
    n_rows = x_rows.shape[0]
    bm = EXPERT_BLOCK
    xmap = lambda i, be, nu: (jnp.maximum(jnp.minimum(i, nu[0] - 1), 0), 0)
    wmap = lambda i, be, nu: (layer, be[i], 0, 0)
    grid_spec = pltpu.PrefetchScalarGridSpec(
        num_scalar_prefetch=2,
        grid=(n_rows // bm,),
        in_specs=[
            pl.BlockSpec((bm, D_MODEL // 2), xmap),
            pl.BlockSpec((None, None, D_MODEL, EXPERT_HIDDEN), wmap),
            pl.BlockSpec((None, None, D_MODEL, EXPERT_HIDDEN), wmap),
            pl.BlockSpec((None, None, EXPERT_HIDDEN, D_MODEL), wmap),
        ],
        out_specs=pl.BlockSpec((bm, D_MODEL), lambda i, be, nu: (i, 0)),
        scratch_shapes=[
            pltpu.VMEM((D_MODEL, EXPERT_HIDDEN), BF16),
            pltpu.VMEM((D_MODEL, EXPERT_HIDDEN), BF16),
            pltpu.VMEM((EXPERT_HIDDEN, D_MODEL), BF16),
        ],
    )
    return pl.pallas_call(
        _experts_kernel,
        grid_spec=grid_spec,
        out_shape=jax.ShapeDtypeStruct((n_rows, D_MODEL), F32),
        compiler_params=_params(1),
        name="experts",
    )(block_expert, n_used, x_rows, w_gate, w_up, w_down)


def _dispatch_kernel(dest_ref, h_ref, init_ref, rows_ref, sem):
    del init_ref
    tm = h_ref.shape[0]
    base = pl.program_id(0) * tm
    n_tok = pl.num_programs(0) * tm

    def row_copy(r, d):
        return pltpu.make_async_copy(h_ref.at[pl.ds(r, 1)], rows_ref.at[pl.ds(d, 1)], sem)

    def issue(r, carry):
        for k in range(TOP_K):
            row_copy(r, dest_ref[k * n_tok + base + r]).start()
        return carry

    def drain(r, carry):
        for k in range(TOP_K):
            row_copy(r, dest_ref[k * n_tok + base + r]).wait()
        return carry

    lax.fori_loop(0, tm, issue, 0, unroll=8)
    lax.fori_loop(0, tm, drain, 0, unroll=8)


def _dispatch(dest, h2_packed, buf_len):
    n, width = h2_packed.shape
    tm = ROW_TILE
    grid_spec = pltpu.PrefetchScalarGridSpec(
        num_scalar_prefetch=1,
        grid=(n // tm,),
        in_specs=[
            pl.BlockSpec((tm, width), lambda i, dest: (i, 0)),
            pl.BlockSpec(memory_space=pl.ANY),
        ],
        out_specs=pl.BlockSpec(memory_space=pl.ANY),
        scratch_shapes=[pltpu.SemaphoreType.DMA(())],
    )
    return pl.pallas_call(
        _dispatch_kernel,
        grid_spec=grid_spec,
        out_shape=jax.ShapeDtypeStruct((buf_len, width), h2_packed.dtype),
        input_output_aliases={2: 0},
        compiler_params=_params(1),
        name="moe_dispatch",
    )(dest, h2_packed, jnp.zeros((buf_len, width), h2_packed.dtype))


def _combine_kernel(dest_ref, x_ref, gate_ref, y_ref, out_ref, ybuf, sems):
    tm = x_ref.shape[0]
    i = pl.program_id(0)
    cur = lax.rem(i, 2)

    n_tok = pl.num_programs(0) * tm

    def row_copy(step, slot, r, k):
        d = dest_ref[k * n_tok + step * tm + r]
        return pltpu.make_async_copy(y_ref.at[pl.ds(d, 1)], ybuf.at[slot, k, pl.ds(r, 1)],
                                     sems.at[slot])

    def fetch(step, slot):
        def issue(r, carry):
            for k in range(TOP_K):
                row_copy(step, slot, r, k).start()
            return carry
        lax.fori_loop(0, tm, issue, 0, unroll=8)

    @pl.when(i == 0)
    def _():
        fetch(0, 0)

    @pl.when(i + 1 < pl.num_programs(0))
    def _():
        fetch(i + 1, 1 - cur)

    def drain(r, carry):
        for k in range(TOP_K):
            row_copy(i, cur, r, k).wait()
        return carry
    lax.fori_loop(0, tm, drain, 0, unroll=8)

    g = gate_ref[...]
    out_ref[...] = (x_ref[...] + g[:, 0:1] * ybuf[cur, 0] + g[:, 1:2] * ybuf[cur, 1])


def _combine(dest, x_new, gates, y_rows):
    n, d = x_new.shape
    tm = ROW_TILE
    grid_spec = pltpu.PrefetchScalarGridSpec(
        num_scalar_prefetch=1,
        grid=(n // tm,),
        in_specs=[
            pl.BlockSpec((tm, d), lambda i, dest: (i, 0)),
            pl.BlockSpec((tm, LANES), lambda i, dest: (i, 0)),
            pl.BlockSpec(memory_space=pl.ANY),
        ],
        out_specs=pl.BlockSpec((tm, d), lambda i, dest: (i, 0)),
        scratch_shapes=[
            pltpu.VMEM((2, TOP_K, tm, d), F32),
            pltpu.SemaphoreType.DMA((2,)),
        ],
    )
    return pl.pallas_call(
        _combine_kernel,
        grid_spec=grid_spec,
        out_shape=jax.ShapeDtypeStruct((n, d), F32),
        compiler_params=_params(1),
        name="moe_combine",
    )(dest, x_new, gates, y_rows)


def _dispatch_plan(expert_id, slot, counts):
    bm = EXPERT_BLOCK
    n_assign = expert_id.size
    padded = (counts + bm - 1) // bm * bm
    pend = jnp.cumsum(padded)
    pstart = pend - padded
    experts = jnp.arange(N_EXPERTS, dtype=jnp.int32)
    seg_start = jnp.sum(jnp.where(expert_id[..., None] == experts, pstart, 0), axis=-1)
    dest = (seg_start + slot).reshape(-1).astype(jnp.int32)
    buf_len = n_assign + N_EXPERTS * bm
    block_start = jnp.arange(buf_len // bm, dtype=jnp.int32) * bm
    n_used = (pend[-1:] // bm).astype(jnp.int32)
    block_start = jnp.minimum(block_start, pend[-1] - bm)
    block_expert = jnp.sum((pend[None, :] <= block_start[:, None]).astype(jnp.int32), axis=1)
    return dest, jnp.minimum(block_expert, N_EXPERTS - 1), n_used, buf_len


def kernel(x, norm1_g, w_in, q_norm_g, k_norm_g, w_attn_out, w_pool_mix, pool_scale, w_pool_out,
           w_o, norm2_g, w_router_group, b_router_group, w_router_expert, b_router_expert,
           w_exp_gate, w_exp_up, w_exp_down):
    batch, seq, d = x.shape
    assert (seq, d) == (SEQ, D_MODEL)
    n_tok = batch * seq
    depth = w_in.shape[0]
    half = ATTN_WIDTH // 2
    hsum = (jnp.arange(half)[:, None] // HEAD_DIM == jnp.arange(half)[None, :] // HEAD_DIM).astype(BF16)

    xf = x.reshape(n_tok, d)
    for layer in range(depth):
        qg = jnp.tile(q_norm_g[layer], N_HEADS)[None, :]
        kg = jnp.tile(k_norm_g[layer], N_HEADS)[None, :]
        q, k, v, u, gates = _inproj(xf, norm1_g[layer][None, :], w_in[layer].astype(BF16), qg, kg, hsum)
        attn = _attn(q.reshape(batch, seq, ATTN_WIDTH), k.reshape(batch, seq, ATTN_WIDTH),
                     v.reshape(batch, seq, ATTN_WIDTH), batch).reshape(n_tok, ATTN_WIDTH)

        w_router, b_router = _router_operands(
            w_router_group[layer], b_router_group[layer], w_router_expert[layer], b_router_expert[layer])
        x_new, h2, route_i, route_f, counts = _post(
            xf, attn, u, gates, w_attn_out[layer].astype(BF16), w_pool_mix[layer].astype(BF16),
            pool_scale[layer][None, :], w_pool_out[layer].astype(BF16), w_o[layer].astype(BF16),
            norm2_g[layer][None, :], w_router, b_router)

        dest, block_expert, n_used, buf_len = _dispatch_plan(
            route_i[:TOP_K], route_i[TOP_K:2 * TOP_K], counts[:, 0].astype(jnp.int32))
        x_rows = _dispatch(dest, h2, buf_len)
        y_rows = _experts(block_expert, n_used, x_rows, w_exp_gate, w_exp_up, w_exp_down, layer)
        gate_rows = jnp.pad(route_f[:TOP_K].T, ((0, 0), (0, LANES - TOP_K)))
        xf = _combine(dest, x_new, gate_rows, y_rows)
    return xf.reshape(batch, seq, d)
```

```python
import jax
import jax.numpy as jnp
import numpy as np
from jax import lax
from jax.experimental import pallas as pl
from jax.experimental.pallas import tpu as pltpu

D_MODEL = 1024
SEQ = 4096
N_HEADS = 8
HEAD_DIM = 64
ATTN_WIDTH = N_HEADS * HEAD_DIM
DILATIONS = (1, 4, 16)
ATTN_BLOCK = 128
POOL_WINDOWS = (2, 4, 8, 16)
POOL_GROUP_DIM = 128
POOL_WIDTH = len(POOL_WINDOWS) * POOL_GROUP_DIM
POOL_HALO = 16
IN_PROJ_WIDTH = 3 * ATTN_WIDTH + POOL_WIDTH + 2 * D_MODEL
N_EXPERT_GROUPS = 4
EXPERTS_PER_GROUP = 8
N_EXPERTS = N_EXPERT_GROUPS * EXPERTS_PER_GROUP
TOP_K = 2
EXPERT_HIDDEN = 512
RMS_EPS = 1e-6
NEG_INF = -1e30

LANES = 128
ROW_TILE = 256
EXPERT_BLOCK = 256
VMEM_LIMIT = 48 * 1024 * 1024
ATTN_VMEM_LIMIT = 56 * 1024 * 1024
EXPERTS_VMEM_LIMIT = 60 * 1024 * 1024

F32 = jnp.float32
BF16 = jnp.bfloat16


def _params(n_axes):
    return pltpu.CompilerParams(
        dimension_semantics=("arbitrary",) * n_axes, vmem_limit_bytes=VMEM_LIMIT)


def _inproj_kernel(x_ref, g_ref, w_ref, qg_ref, kg_ref, hsum_ref,
                   q_ref, k_ref, v_ref, u_ref, gate_ref):
    x = x_ref[...]
    ms = jnp.mean(x * x, axis=-1, keepdims=True)
    h = (x * lax.rsqrt(ms + RMS_EPS) * g_ref[...]).astype(BF16)

    def proj(lo, hi):
        return jnp.dot(h, w_ref[:, lo:hi], preferred_element_type=F32)

    def head_norm(t, gain):
        sq = (t * t).astype(BF16)
        half = ATTN_WIDTH // 2
        ssq = jnp.concatenate(
            [jnp.dot(sq[:, j * half:(j + 1) * half], hsum_ref[...], preferred_element_type=F32)
             for j in range(2)], axis=-1)
        return t * lax.rsqrt(ssq * (1.0 / HEAD_DIM) + RMS_EPS) * gain

    w = ATTN_WIDTH
    q_ref[...] = head_norm(proj(0, w), qg_ref[...])
    k_ref[...] = head_norm(proj(w, 2 * w), kg_ref[...])
    v_ref[...] = proj(2 * w, 3 * w)
    u_ref[...] = proj(3 * w, 3 * w + POOL_WIDTH)
    base = 3 * w + POOL_WIDTH
    for j in range(2 * D_MODEL // 512):
        gate_ref[:, j * 512:(j + 1) * 512] = jax.nn.sigmoid(
            proj(base + j * 512, base + (j + 1) * 512)).astype(BF16)


def _inproj(x, g1, w_in, qg, kg, hsum):
    n = x.shape[0]
    row = lambda i: (i, 0)
    const = lambda i: (0, 0)
    return pl.pallas_call(
        _inproj_kernel,
        grid=(n // ROW_TILE,),
        in_specs=[
            pl.BlockSpec((ROW_TILE, D_MODEL), row),
            pl.BlockSpec((1, D_MODEL), const),
            pl.BlockSpec((D_MODEL, IN_PROJ_WIDTH), const),
            pl.BlockSpec((1, ATTN_WIDTH), const),
            pl.BlockSpec((1, ATTN_WIDTH), const),
            pl.BlockSpec((ATTN_WIDTH // 2, ATTN_WIDTH // 2), const),
        ],
        out_specs=[
            pl.BlockSpec((ROW_TILE, ATTN_WIDTH), row),
            pl.BlockSpec((ROW_TILE, ATTN_WIDTH), row),
            pl.BlockSpec((ROW_TILE, ATTN_WIDTH), row),
            pl.BlockSpec((ROW_TILE, POOL_WIDTH), row),
            pl.BlockSpec((ROW_TILE, 2 * D_MODEL), row),
        ],
        out_shape=[
            jax.ShapeDtypeStruct((n, ATTN_WIDTH), F32),
            jax.ShapeDtypeStruct((n, ATTN_WIDTH), F32),
            jax.ShapeDtypeStruct((n, ATTN_WIDTH), F32),
            jax.ShapeDtypeStruct((n, POOL_WIDTH), F32),
            jax.ShapeDtypeStruct((n, 2 * D_MODEL), BF16),
        ],
        compiler_params=_params(1),
        name="inproj",
    )(x, g1, w_in, qg, kg, hsum)


QUAD = 4
QUAD_ROWS = SEQ // QUAD
LOG2E = 1.4426950408889634


def _attn_bias():
    blk = ATTN_BLOCK
    r = np.arange(2 * blk) % blk
    c = np.arange(2 * blk)
    per_q, per_k = blk // QUAD, 2 * blk // QUAD
    tq = QUAD * (r % per_q) + r // per_q
    tk = QUAD * (c % per_k) + c // per_k
    masks = []
    for tq_, tk_ in ((tq, tk), (r, c)):
        first = tq_[:, None] - tk_[None, :]
        later = first + blk
        masks += [first >= 0, (later >= 0) & (later <= blk)]
    return np.where(np.stack(masks), 0.0, NEG_INF).astype(np.float32)


def _attn_kernel(q_ref, k_ref, v_ref, bias_ref, out_ref, q4, k4, v4,
                 o0, o1, o2, m0, m1, m2, d0, d1, d2):
    blk = ATTN_BLOCK
    head_a = lax.broadcasted_iota(jnp.int32, (blk, LANES), 1) < HEAD_DIM
    ones_cols = jnp.ones((2 * blk, LANES), BF16)

    for c in range(QUAD):
        dst = pl.ds(c * QUAD_ROWS, QUAD_ROWS)
        src = pl.ds(c, QUAD_ROWS, stride=QUAD)
        q4[dst, :] = q_ref[src, :] * (HEAD_DIM ** -0.5 * LOG2E)
        k4[dst, :] = k_ref[src, :]
        v4[dst, :] = v_ref[src, :]

    def attend(q2, k2, v2, bias):
        qs = jnp.concatenate(
            [jnp.where(head_a, q2, 0.0), jnp.where(head_a, 0.0, q2)], axis=0).astype(BF16)
        s = lax.dot_general(qs, k2.astype(BF16), (((1,), (1,)), ((), ())),
                            preferred_element_type=F32) + bias
        m = jnp.max(s, axis=-1, keepdims=True)
        e = jnp.exp2(s - m).astype(BF16)
        r = jnp.dot(e, jnp.concatenate([v2.astype(BF16), ones_cols], axis=1),
                    preferred_element_type=F32)
        mb = jnp.broadcast_to(m, (2 * blk, LANES))
        return (jnp.where(head_a, r[:blk, :LANES], r[blk:, :LANES]),
                jnp.where(head_a, mb[:blk], mb[blk:]),
                jnp.where(head_a, r[:blk, LANES:], r[blk:, LANES:]))

    def load(ref, pieces):
        return jnp.concatenate([ref[p, :] for p in pieces], axis=0)

    def store(refs, pieces, vals):
        for ref, val in zip(refs, vals):
            row = 0
            for p in pieces:
                ref[p, :] = val[row:row + p.size]
                row += p.size

    def body1(nb, carry):
        kb = jnp.maximum(nb - 1, 0)
        per_q, per_k = blk // QUAD, 2 * blk // QUAD
        qp = [pl.ds(pl.multiple_of(c * QUAD_ROWS + nb * per_q, per_q), per_q) for c in range(QUAD)]
        kp = [pl.ds(pl.multiple_of(c * QUAD_ROWS + kb * per_q, per_q), per_k) for c in range(QUAD)]
        res = attend(load(q4, qp), load(k4, kp), load(v4, kp), bias_ref[jnp.minimum(nb, 1)])
        store((o0, m0, d0), qp, res)
        return carry

    lax.fori_loop(0, SEQ // blk, body1, 0, unroll=16)

    n_blk4 = QUAD_ROWS // blk

    def body4(idx, carry):
        nb = idx & (n_blk4 - 1)
        base = (idx - nb) * blk
        kb = jnp.maximum(nb - 1, 0)
        qp = [pl.ds(pl.multiple_of(base + nb * blk, blk), blk)]
        kp = [pl.ds(pl.multiple_of(base + kb * blk, blk), 2 * blk)]
        res = attend(load(q4, qp), load(k4, kp), load(v4, kp), bias_ref[2 + jnp.minimum(nb, 1)])
        store((o1, m1, d1), qp, res)
        return carry

    lax.fori_loop(0, SEQ // blk, body4, 0, unroll=16)

    def body16(idx, carry):
        start = (idx & (QUAD - 1)) * QUAD_ROWS + lax.shift_right_logical(idx, 2)
        kp = [pl.ds(start, 2 * blk, stride=QUAD)]
        k2, v2 = load(k4, kp), load(v4, kp)
        for nb in range(2):
            qp = [pl.ds(start + nb * blk * QUAD, blk, stride=QUAD)]
            store((o2, m2, d2), qp, attend(load(q4, qp), k2, v2, bias_ref[2 + nb]))
        return carry

    lax.fori_loop(0, SEQ // (2 * blk), body16, 0, unroll=8)

    chunk = 512

    def mix(i, carry):
        r = pl.ds(pl.multiple_of(i * chunk, chunk), chunk)
        ma, mb, mc = m0[r, :], m1[r, :], m2[r, :]
        m = jnp.maximum(jnp.maximum(ma, mb), mc)
        wa, wb, wc = jnp.exp2(ma - m), jnp.exp2(mb - m), jnp.exp2(mc - m)
        acc = wa * o0[r, :] + wb * o1[r, :] + wc * o2[r, :]
        den = wa * d0[r, :] + wb * d1[r, :] + wc * d2[r, :]
        per_class = QUAD_ROWS // chunk
        c = i // per_class
        n0 = (i - c * per_class) * chunk
        out_ref[pl.ds(QUAD * n0 + c, chunk, stride=QUAD), :] = acc / den
        return carry

    lax.fori_loop(0, SEQ // chunk, mix, 0)


def _attn(q, k, v, batch):
    spec = pl.BlockSpec((None, SEQ, LANES), lambda b, hp: (b, 0, hp))
    bias = _attn_bias()
    return pl.pallas_call(
        _attn_kernel,
        grid=(batch, ATTN_WIDTH // LANES),
        in_specs=[spec, spec, spec, pl.BlockSpec(bias.shape, lambda b, hp: (0, 0, 0))],
        out_specs=spec,
        out_shape=jax.ShapeDtypeStruct((batch, SEQ, ATTN_WIDTH), F32),
        scratch_shapes=[pltpu.VMEM((SEQ, LANES), F32) for _ in range(12)],
        compiler_params=pltpu.CompilerParams(
            dimension_semantics=("arbitrary", "arbitrary"), vmem_limit_bytes=ATTN_VMEM_LIMIT),
        name="dilated_attn",
    )(q, k, v, jnp.asarray(bias))


def _pack_bf16_pairs(x):
    c = x.shape[1] // 2
    bits = pltpu.bitcast(x.astype(BF16).astype(F32), jnp.uint32)
    return lax.shift_right_logical(bits[:, :c], jnp.uint32(16)) | bits[:, c:]


def _unpack_bf16_pairs(w):
    lo = pltpu.bitcast(lax.shift_left(w, jnp.uint32(16)), F32)
    hi = pltpu.bitcast(w & jnp.uint32(0xFFFF0000), F32)
    return jnp.concatenate([lo, hi], axis=1).astype(BF16)


def _post_kernel(x_ref, attn_ref, u_ref, halo_ref, gate_ref, wao_ref, wmix_ref, pscale_ref,
                 wpo_ref, wo_ref, g2_ref, wr_ref, br_ref,
                 xo_ref, h2_ref, ri_ref, rf_ref, cnt_ref):
    tm = x_ref.shape[0]

    @pl.when(pl.program_id(0) == 0)
    def _():
        cnt_ref[...] = jnp.zeros_like(cnt_ref)

    pos0 = lax.rem(pl.program_id(0) * tm, SEQ)
    pos = pos0 + lax.broadcasted_iota(jnp.int32, (tm, 1), 0)
    u = u_ref[...]
    halo = halo_ref[...] * (pos0 > 0).astype(F32)

    mixed = []
    for gi, w in enumerate(POOL_WINDOWS):
        lo = gi * POOL_GROUP_DIM
        ug = u[:, lo:lo + POOL_GROUP_DIM]
        ext = jnp.concatenate([halo[:, lo:lo + POOL_GROUP_DIM], ug], axis=0)
        shift = 1
        while shift < w:
            ext = ext + pltpu.roll(ext, shift, 0)
            shift *= 2
        cnt = jnp.minimum(pos + 1, w).astype(F32)
        pooled = ext[POOL_HALO:] / cnt - ug
        mixed.append(jnp.dot(pooled.astype(BF16), wmix_ref[gi], preferred_element_type=F32))
    pool_out = (jnp.concatenate(mixed, axis=-1) * pscale_ref[...]).astype(BF16)

    y_a = jnp.dot(attn_ref[...].astype(BF16), wao_ref[...], preferred_element_type=F32)
    y_p = jnp.dot(pool_out, wpo_ref[...], preferred_element_type=F32)
    gates = gate_ref[...]
    merged = gates[:, :D_MODEL].astype(F32) * y_a + gates[:, D_MODEL:].astype(F32) * y_p
    x_new = x_ref[...] + jnp.dot(merged.astype(BF16), wo_ref[...], preferred_element_type=F32)
    xo_ref[...] = x_new

    ms = jnp.mean(x_new * x_new, axis=-1, keepdims=True)
    h2 = x_new * lax.rsqrt(ms + RMS_EPS) * g2_ref[...]
    h_hi = h2.astype(BF16)
    h2_ref[...] = _pack_bf16_pairs(h2)

    h_lo = (h2 - h_hi.astype(F32)).astype(BF16)
    nt = (((1,), (1,)), ((), ()))
    both = lax.dot_general(wr_ref[...], h_hi, nt, preferred_element_type=F32)
    cross = lax.dot_general(wr_ref[:ROUTER_ROWS, :], h_lo, nt, preferred_element_type=F32)
    logits = both[:ROUTER_ROWS] + both[ROUTER_ROWS:] + cross + jnp.concatenate(
        [br_ref[...]] * (tm // LANES), axis=1)

    sub = lax.broadcasted_iota(jnp.int32, (8, tm), 0)
    npg = EXPERTS_PER_GROUP

    def first_max(vals):
        vmax = jnp.max(vals, axis=0, keepdims=True)
        return vmax, jnp.min(jnp.where(vals == vmax, sub, npg), axis=0, keepdims=True)

    def of_group(parts, g_sel):
        out = parts[-1]
        for g in range(N_EXPERT_GROUPS - 2, -1, -1):
            out = jnp.where(g_sel == g, parts[g], out)
        return out

    glog = jnp.where(sub < N_EXPERT_GROUPS, logits[0:8], -jnp.inf)
    gmax, g_sel = first_max(glog)
    g_gate = 1.0 / jnp.sum(jnp.exp(glog - gmax), axis=0, keepdims=True)

    elog = of_group([logits[npg * (g + 1):npg * (g + 2)] for g in range(N_EXPERT_GROUPS)], g_sel)
    e_exp = jnp.exp(elog - jnp.max(elog, axis=0, keepdims=True))
    prob = e_exp / jnp.sum(e_exp, axis=0, keepdims=True)
    p1, i1 = first_max(prob)
    p2, i2 = first_max(jnp.where(sub == i1, -1.0, prob))
    scale = g_gate / (p1 + p2)

    chosen = (sub == i1) | (sub == i2)
    onehot = jnp.concatenate(
        [jnp.where(chosen & (g_sel == g), 1.0, 0.0) for g in range(N_EXPERT_GROUPS)],
        axis=0).astype(BF16)
    t_row = lax.broadcasted_iota(jnp.int32, (tm, tm), 0)
    t_col = lax.broadcasted_iota(jnp.int32, (tm, tm), 1)
    before = jnp.dot(onehot, jnp.where(t_row < t_col, 1.0, 0.0).astype(BF16),
                     preferred_element_type=F32)
    total = jnp.dot(onehot, jnp.ones((tm, LANES), BF16), preferred_element_type=F32)
    seen = cnt_ref[...]
    slot = before + jnp.concatenate([seen] * (tm // LANES), axis=1)
    cnt_ref[...] = seen + total
    slot = of_group([slot[npg * g:npg * (g + 1)] for g in range(N_EXPERT_GROUPS)], g_sel)
    r1 = jnp.sum(jnp.where(sub == i1, slot, 0.0), axis=0, keepdims=True).astype(jnp.int32)
    r2 = jnp.sum(jnp.where(sub == i2, slot, 0.0), axis=0, keepdims=True).astype(jnp.int32)

    e_base = g_sel * npg
    ri_ref[...] = jnp.where(sub == 0, e_base + i1,
                            jnp.where(sub == 1, e_base + i2,
                                      jnp.where(sub == 2, r1, jnp.where(sub == 3, r2, 0))))
    rf_ref[...] = jnp.where(sub == 0, p1 * scale, jnp.where(sub == 1, p2 * scale, 0.0))


ROUTER_ROWS = 8 + N_EXPERTS + 8


def _router_operands(w_group, b_group, w_expert, b_expert):
    d = w_group.shape[0]
    wt = jnp.zeros((ROUTER_ROWS, d), F32)
    wt = wt.at[:N_EXPERT_GROUPS].set(w_group.T)
    wt = wt.at[8:8 + N_EXPERTS].set(w_expert.transpose(0, 2, 1).reshape(N_EXPERTS, d))
    hi = wt.astype(BF16)
    lo = (wt - hi.astype(F32)).astype(BF16)
    b = jnp.zeros((ROUTER_ROWS,), F32).at[:N_EXPERT_GROUPS].set(b_group)
    b = b.at[8:8 + N_EXPERTS].set(b_expert.reshape(-1))
    return jnp.concatenate([hi, lo], axis=0), jnp.broadcast_to(b[:, None], (ROUTER_ROWS, LANES))


def _post(x, attn, u, gates, wao, wmix, pscale, wpo, wo, g2, wr, br):
    n = x.shape[0]
    tm = ROW_TILE
    row = lambda i: (i, 0)
    const = lambda i: (0, 0)
    halo_blocks = tm // POOL_HALO
    return pl.pallas_call(
        _post_kernel,
        grid=(n // tm,),
        in_specs=[
            pl.BlockSpec((tm, D_MODEL), row),
            pl.BlockSpec((tm, ATTN_WIDTH), row),
            pl.BlockSpec((tm, POOL_WIDTH), row),
            pl.BlockSpec((POOL_HALO, POOL_WIDTH),
                         lambda i: (jnp.maximum(i * halo_blocks - 1, 0), 0)),
            pl.BlockSpec((tm, 2 * D_MODEL), row),
            pl.BlockSpec((ATTN_WIDTH, D_MODEL), const),
            pl.BlockSpec((len(POOL_WINDOWS), POOL_GROUP_DIM, POOL_GROUP_DIM), lambda i: (0, 0, 0)),
            pl.BlockSpec((1, POOL_WIDTH), const),
            pl.BlockSpec((POOL_WIDTH, D_MODEL), const),
            pl.BlockSpec((D_MODEL, D_MODEL), const),
            pl.BlockSpec((1, D_MODEL), const),
            pl.BlockSpec((2 * ROUTER_ROWS, D_MODEL), const),
            pl.BlockSpec((ROUTER_ROWS, LANES), const),
        ],
        out_specs=[
            pl.BlockSpec((tm, D_MODEL), row),
            pl.BlockSpec((tm, D_MODEL // 2), row),
            pl.BlockSpec((8, tm), lambda i: (0, i)),
            pl.BlockSpec((8, tm), lambda i: (0, i)),
            pl.BlockSpec((N_EXPERTS, LANES), const),
        ],
        out_shape=[
            jax.ShapeDtypeStruct((n, D_MODEL), F32),
            jax.ShapeDtypeStruct((n, D_MODEL // 2), jnp.uint32),
            jax.ShapeDtypeStruct((8, n), jnp.int32),
            jax.ShapeDtypeStruct((8, n), F32),
            jax.ShapeDtypeStruct((N_EXPERTS, LANES), F32),
        ],
        compiler_params=_params(1),
        name="post_attn_router",
    )(x, attn, u, u, gates, wao, wmix, pscale, wpo, wo, g2, wr, br)


ZERO_ROWS = 8


def _experts_kernel(be_ref, nused_ref, tok_ref, h_hbm, wg_ref, wu_ref, wd_ref, y_ref,
                    h_vmem, xs0, xs1, wg_bf, wu_bf, wd_bf, h_sem):
    i = pl.program_id(0)
    bm = xs0.shape[0]
    n_tok = h_hbm.shape[0]
    changed = (i == 0) | (be_ref[i] != be_ref[jnp.maximum(i - 1, 0)])
    used = i < nused_ref[0]
    cur = i & 1

    @pl.when(i == 0)
    def _():
        copy = pltpu.make_async_copy(h_hbm, h_vmem.at[pl.ds(0, n_tok)], h_sem)
        copy.start()
        h_vmem[pl.ds(n_tok, ZERO_ROWS), :] = jnp.zeros((ZERO_ROWS, h_vmem.shape[1]), h_vmem.dtype)
        copy.wait()

        def gather(j, carry):
            xs0[pl.ds(j, 1), :] = h_vmem[pl.ds(tok_ref[j], 1), :]
            return carry
        lax.fori_loop(0, bm, gather, 0, unroll=8)

    @pl.when(changed & used)
    def _():
        wg_bf[...] = wg_ref[...].astype(BF16)
        wu_bf[...] = wu_ref[...].astype(BF16)
        wd_bf[...] = wd_ref[...].astype(BF16)

    def block(x_ref, x_next_ref):
        nxt = jnp.where(i + 1 < nused_ref[0], i + 1, i) * bm
        for j in range(bm):
            x_next_ref[pl.ds(j, 1), :] = h_vmem[pl.ds(tok_ref[nxt + j], 1), :]

        x = _unpack_bf16_pairs(x_ref[...])
        a = jnp.dot(x, wg_bf[...], preferred_element_type=F32)
        b = jnp.dot(x, wu_bf[...], preferred_element_type=F32)
        mid = (a * jax.nn.sigmoid(a) * b).astype(BF16)
        y_ref[...] = _pack_bf16_pairs(jnp.dot(mid, wd_bf[...], preferred_element_type=F32))

    pl.when(used & (cur == 0))(lambda: block(xs0, xs1))
    pl.when(used & (cur == 1))(lambda: block(xs1, xs0))

    @pl.when(jnp.logical_not(used))
    def _():
        y_ref[...] = jnp.zeros_like(y_ref)


def _experts(block_expert, n_used, sorted_tok, h2_packed, w_gate, w_up, w_down, layer):
    n_tok, width = h2_packed.shape
    bm = EXPERT_BLOCK
    wmap = lambda i, be, nu, tok: (layer, be[i], 0, 0)
    grid_spec = pltpu.PrefetchScalarGridSpec(
        num_scalar_prefetch=3,
        grid=(sorted_tok.shape[0] // bm,),
        in_specs=[
            pl.BlockSpec(memory_space=pl.ANY),
            pl.BlockSpec((None, None, D_MODEL, EXPERT_HIDDEN), wmap),
            pl.BlockSpec((None, None, D_MODEL, EXPERT_HIDDEN), wmap),
            pl.BlockSpec((None, None, EXPERT_HIDDEN, D_MODEL), wmap),
        ],
        out_specs=pl.BlockSpec((bm, width), lambda i, be, nu, tok: (i, 0)),
        scratch_shapes=[
            pltpu.VMEM((n_tok + ZERO_ROWS, width), h2_packed.dtype),
            pltpu.VMEM((bm, width), h2_packed.dtype),
            pltpu.VMEM((bm, width), h2_packed.dtype),
            pltpu.VMEM((D_MODEL, EXPERT_HIDDEN), BF16),
            pltpu.VMEM((D_MODEL, EXPERT_HIDDEN), BF16),
            pltpu.VMEM((EXPERT_HIDDEN, D_MODEL), BF16),
            pltpu.SemaphoreType.DMA(()),
        ],
    )
    return pl.pallas_call(
        _experts_kernel,
        grid_spec=grid_spec,
        out_shape=jax.ShapeDtypeStruct((sorted_tok.shape[0], width), h2_packed.dtype),
        compiler_params=pltpu.CompilerParams(
            dimension_semantics=("arbitrary",), vmem_limit_bytes=EXPERTS_VMEM_LIMIT),
        name="experts",
    )(block_expert, n_used, sorted_tok, h2_packed, w_gate, w_up, w_down)


def _sorted_tokens_kernel(dest_ref, pad_lo_ref, pad_hi_ref, tok_ref):
    n_tok = dest_ref.shape[0] // TOP_K

    def fill(p, carry):
        tok_ref[p] = n_tok
        return carry

    def place(t, carry):
        for k in range(TOP_K):
            tok_ref[dest_ref[k * n_tok + t]] = t
        return carry

    for s in range(pad_lo_ref.shape[0]):
        lax.fori_loop(pad_lo_ref[s], pad_hi_ref[s], fill, 0)
    lax.fori_loop(0, n_tok, place, 0, unroll=8)


def _sorted_tokens(dest, pad_lo, pad_hi, buf_len):
    smem = pl.BlockSpec(memory_space=pltpu.SMEM)
    return pl.pallas_call(
        _sorted_tokens_kernel,
        in_specs=[smem, smem, smem],
        out_specs=smem,
        out_shape=jax.ShapeDtypeStruct((buf_len,), jnp.int32),
        name="moe_sorted_tokens",
    )(dest, pad_lo, pad_hi)


def _combine_kernel(dest_ref, x_ref, gate_ref, y_ref, out_ref, ybuf, sems):
    tm = x_ref.shape[0]
    i = pl.program_id(0)
    cur = lax.rem(i, 2)

    n_tok = pl.num_programs(0) * tm

    def row_copy(step, slot, r, k):
        d = dest_ref[k * n_tok + step * tm + r]
        return pltpu.make_async_copy(y_ref.at[pl.ds(d, 1)], ybuf.at[slot, k, pl.ds(r, 1)],
                                     sems.at[slot])

    def fetch(step, slot):
        def issue(r, carry):
            for k in range(TOP_K):
                row_copy(step, slot, r, k).start()
            return carry
        lax.fori_loop(0, tm, issue, 0, unroll=8)

    @pl.when(i == 0)
    def _():
        fetch(0, 0)

    @pl.when(i + 1 < pl.num_programs(0))
    def _():
        fetch(i + 1, 1 - cur)

    def drain(r, carry):
        for k in range(TOP_K):
            row_copy(i, cur, r, k).wait()
        return carry
    lax.fori_loop(0, tm, drain, 0, unroll=8)

    g = gate_ref[...]
    half = x_ref.shape[1] // 2
    for lo, unpack in ((0, lambda w: lax.shift_left(w, jnp.uint32(16))),
                       (half, lambda w: w & jnp.uint32(0xFFFF0000))):
        y0 = pltpu.bitcast(unpack(ybuf[cur, 0]), F32)
        y1 = pltpu.bitcast(unpack(ybuf[cur, 1]), F32)
        out_ref[:, lo:lo + half] = x_ref[:, lo:lo + half] + g[:, 0:1] * y0 + g[:, 1:2] * y1


def _combine(dest, x_new, gates, y_rows):
    n, d = x_new.shape
    tm = ROW_TILE
    grid_spec = pltpu.PrefetchScalarGridSpec(
        num_scalar_prefetch=1,
        grid=(n // tm,),
        in_specs=[
            pl.BlockSpec((tm, d), lambda i, dest: (i, 0)),
            pl.BlockSpec((tm, LANES), lambda i, dest: (i, 0)),
            pl.BlockSpec(memory_space=pl.ANY),
        ],
        out_specs=pl.BlockSpec((tm, d), lambda i, dest: (i, 0)),
        scratch_shapes=[
            pltpu.VMEM((2, TOP_K, tm, y_rows.shape[1]), y_rows.dtype),
            pltpu.SemaphoreType.DMA((2,)),
        ],
    )
    return pl.pallas_call(
        _combine_kernel,
        grid_spec=grid_spec,
        out_shape=jax.ShapeDtypeStruct((n, d), F32),
        compiler_params=_params(1),
        name="moe_combine",
    )(dest, x_new, gates, y_rows)


def _dispatch_plan(expert_id, slot, counts):
    bm = EXPERT_BLOCK
    n_assign = expert_id.size
    padded = (counts + bm - 1) // bm * bm
    pend = jnp.cumsum(padded)
    pstart = pend - padded
    experts = jnp.arange(N_EXPERTS, dtype=jnp.int32)
    seg_start = jnp.sum(jnp.where(expert_id[..., None] == experts, pstart, 0), axis=-1)
    dest = (seg_start + slot).reshape(-1).astype(jnp.int32)
    buf_len = n_assign + N_EXPERTS * bm
    block_start = jnp.arange(buf_len // bm, dtype=jnp.int32) * bm
    n_used = (pend[-1:] // bm).astype(jnp.int32)
    block_start = jnp.minimum(block_start, pend[-1] - bm)
    block_expert = jnp.sum((pend[None, :] <= block_start[:, None]).astype(jnp.int32), axis=1)
    pad_lo = jnp.concatenate([pstart + counts, pend[-1:]]).astype(jnp.int32)
    pad_hi = jnp.concatenate([pend, jnp.full((1,), buf_len, pend.dtype)]).astype(jnp.int32)
    return dest, jnp.minimum(block_expert, N_EXPERTS - 1), n_used, (pad_lo, pad_hi), buf_len


def kernel(x, norm1_g, w_in, q_norm_g, k_norm_g, w_attn_out, w_pool_mix, pool_scale, w_pool_out,
           w_o, norm2_g, w_router_group, b_router_group, w_router_expert, b_router_expert,
           w_exp_gate, w_exp_up, w_exp_down):
    batch, seq, d = x.shape
    assert (seq, d) == (SEQ, D_MODEL)
    n_tok = batch * seq
    depth = w_in.shape[0]
    half = ATTN_WIDTH // 2
    hsum = (jnp.arange(half)[:, None] // HEAD_DIM == jnp.arange(half)[None, :] // HEAD_DIM).astype(BF16)

    xf = x.reshape(n_tok, d)
    for layer in range(depth):
        qg = jnp.tile(q_norm_g[layer], N_HEADS)[None, :]
        kg = jnp.tile(k_norm_g[layer], N_HEADS)[None, :]
        q, k, v, u, gates = _inproj(xf, norm1_g[layer][None, :], w_in[layer].astype(BF16), qg, kg, hsum)
        attn = _attn(q.reshape(batch, seq, ATTN_WIDTH), k.reshape(batch, seq, ATTN_WIDTH),
                     v.reshape(batch, seq, ATTN_WIDTH), batch).reshape(n_tok, ATTN_WIDTH)

        w_router, b_router = _router_operands(
            w_router_group[layer], b_router_group[layer], w_router_expert[layer], b_router_expert[layer])
        x_new, h2, route_i, route_f, counts = _post(
            xf, attn, u, gates, w_attn_out[layer].astype(BF16), w_pool_mix[layer].astype(BF16),
            pool_scale[layer][None, :], w_pool_out[layer].astype(BF16), w_o[layer].astype(BF16),
            norm2_g[layer][None, :], w_router, b_router)

        dest, block_expert, n_used, pads, buf_len = _dispatch_plan(
            route_i[:TOP_K], route_i[TOP_K:2 * TOP_K], counts[:, 0].astype(jnp.int32))
        sorted_tok = _sorted_tokens(dest, *pads, buf_len)
        y_rows = _experts(block_expert, n_used, sorted_tok, h2,
                          w_exp_gate, w_exp_up, w_exp_down, layer)
        gate_rows = jnp.pad(route_f[:TOP_K].T, ((0, 0), (0, LANES - TOP_K)))
        xf = _combine(dest, x_new, gate_rows, y_rows)
    return xf.reshape(batch, seq, d)
```

```python
import functools

import jax
import jax.numpy as jnp
import numpy as np
from jax import lax
from jax.experimental import pallas as pl
from jax.experimental.pallas import tpu as pltpu

D_MODEL = 1024
SEQ = 4096
N_HEADS = 8
HEAD_DIM = 64
ATTN_WIDTH = N_HEADS * HEAD_DIM
DILATIONS = (1, 4, 16)
ATTN_BLOCK = 128
POOL_WINDOWS = (2, 4, 8, 16)
POOL_GROUP_DIM = 128
POOL_WIDTH = len(POOL_WINDOWS) * POOL_GROUP_DIM
POOL_HALO = 16
IN_PROJ_WIDTH = 3 * ATTN_WIDTH + POOL_WIDTH + 2 * D_MODEL
N_EXPERT_GROUPS = 4
EXPERTS_PER_GROUP = 8
N_EXPERTS = N_EXPERT_GROUPS * EXPERTS_PER_GROUP
TOP_K = 2
EXPERT_HIDDEN = 512
RMS_EPS = 1e-6
NEG_INF = -1e30

LANES = 128
ROW_TILE = 256
EXPERT_BLOCK = 256
VMEM_LIMIT = 48 * 1024 * 1024
ATTN_VMEM_LIMIT = 56 * 1024 * 1024
EXPERTS_VMEM_LIMIT = 56 * 1024 * 1024

F32 = jnp.float32
BF16 = jnp.bfloat16


def _params(n_axes):
    return pltpu.CompilerParams(
        dimension_semantics=("arbitrary",) * n_axes, vmem_limit_bytes=VMEM_LIMIT)


def _inproj_kernel(x_ref, g_ref, w_ref, qg_ref, kg_ref, hsum_ref,
                   q_ref, k_ref, v_ref, u_ref, gate_ref):
    x = x_ref[...]
    ms = jnp.mean(x * x, axis=-1, keepdims=True)
    h = (x * lax.rsqrt(ms + RMS_EPS) * g_ref[...]).astype(BF16)

    def proj(lo, hi):
        return jnp.dot(h, w_ref[:, lo:hi], preferred_element_type=F32)

    def head_norm(t, gain):
        sq = (t * t).astype(BF16)
        half = ATTN_WIDTH // 2
        ssq = jnp.concatenate(
            [jnp.dot(sq[:, j * half:(j + 1) * half], hsum_ref[...], preferred_element_type=F32)
             for j in range(2)], axis=-1)
        return t * lax.rsqrt(ssq * (1.0 / HEAD_DIM) + RMS_EPS) * gain

    w = ATTN_WIDTH
    q_ref[...] = head_norm(proj(0, w), qg_ref[...])
    k_ref[...] = head_norm(proj(w, 2 * w), kg_ref[...])
    v_ref[...] = proj(2 * w, 3 * w)
    u_ref[...] = proj(3 * w, 3 * w + POOL_WIDTH)
    base = 3 * w + POOL_WIDTH
    for j in range(2 * D_MODEL // 512):
        gate_ref[:, j * 512:(j + 1) * 512] = jax.nn.sigmoid(
            proj(base + j * 512, base + (j + 1) * 512)).astype(BF16)


def _inproj(x, g1, w_in, qg, kg, hsum):
    n = x.shape[0]
    row = lambda i: (i, 0)
    const = lambda i: (0, 0)
    return pl.pallas_call(
        _inproj_kernel,
        grid=(n // ROW_TILE,),
        in_specs=[
            pl.BlockSpec((ROW_TILE, D_MODEL), row),
            pl.BlockSpec((1, D_MODEL), const),
            pl.BlockSpec((D_MODEL, IN_PROJ_WIDTH), const),
            pl.BlockSpec((1, ATTN_WIDTH), const),
            pl.BlockSpec((1, ATTN_WIDTH), const),
            pl.BlockSpec((ATTN_WIDTH // 2, ATTN_WIDTH // 2), const),
        ],
        out_specs=[
            pl.BlockSpec((ROW_TILE, ATTN_WIDTH), row),
            pl.BlockSpec((ROW_TILE, ATTN_WIDTH), row),
            pl.BlockSpec((ROW_TILE, ATTN_WIDTH), row),
            pl.BlockSpec((ROW_TILE, POOL_WIDTH), row),
            pl.BlockSpec((ROW_TILE, 2 * D_MODEL), row),
        ],
        out_shape=[
            jax.ShapeDtypeStruct((n, ATTN_WIDTH), F32),
            jax.ShapeDtypeStruct((n, ATTN_WIDTH), F32),
            jax.ShapeDtypeStruct((n, ATTN_WIDTH), F32),
            jax.ShapeDtypeStruct((n, POOL_WIDTH), F32),
            jax.ShapeDtypeStruct((n, 2 * D_MODEL), BF16),
        ],
        compiler_params=_params(1),
        name="inproj",
    )(x, g1, w_in, qg, kg, hsum)


QUAD = 4
QUAD_ROWS = SEQ // QUAD
LOG2E = 1.4426950408889634


def _attn_bias():
    blk = ATTN_BLOCK
    r = np.arange(2 * blk) % blk
    c = np.arange(2 * blk)
    per_q, per_k = blk // QUAD, 2 * blk // QUAD
    tq = QUAD * (r % per_q) + r // per_q
    tk = QUAD * (c % per_k) + c // per_k
    masks = []
    for tq_, tk_ in ((tq, tk), (r, c)):
        first = tq_[:, None] - tk_[None, :]
        later = first + blk
        masks += [first >= 0, (later >= 0) & (later <= blk)]
    return np.where(np.stack(masks), 0.0, NEG_INF).astype(np.float32)


def _attn_kernel(q_ref, k_ref, v_ref, bias_ref, out_ref, q4, k4, v4,
                 o0, o1, o2, m0, m1, m2, d0, d1, d2):
    blk = ATTN_BLOCK
    head_a = lax.broadcasted_iota(jnp.int32, (blk, LANES), 1) < HEAD_DIM
    ones_cols = jnp.ones((2 * blk, LANES), BF16)

    for c in range(QUAD):
        dst = pl.ds(c * QUAD_ROWS, QUAD_ROWS)
        src = pl.ds(c, QUAD_ROWS, stride=QUAD)
        q4[dst, :] = q_ref[src, :] * (HEAD_DIM ** -0.5 * LOG2E)
        k4[dst, :] = k_ref[src, :]
        v4[dst, :] = v_ref[src, :]

    def attend(q2, k2, v2, bias):
        qs = jnp.concatenate(
            [jnp.where(head_a, q2, 0.0), jnp.where(head_a, 0.0, q2)], axis=0).astype(BF16)
        s = lax.dot_general(qs, k2.astype(BF16), (((1,), (1,)), ((), ())),
                            preferred_element_type=F32) + bias
        m = jnp.max(s, axis=-1, keepdims=True)
        e = jnp.exp2(s - m).astype(BF16)
        r = jnp.dot(e, jnp.concatenate([v2.astype(BF16), ones_cols], axis=1),
                    preferred_element_type=F32)
        mb = jnp.broadcast_to(m, (2 * blk, LANES))
        return (jnp.where(head_a, r[:blk, :LANES], r[blk:, :LANES]),
                jnp.where(head_a, mb[:blk], mb[blk:]),
                jnp.where(head_a, r[:blk, LANES:], r[blk:, LANES:]))

    def load(ref, pieces):
        return jnp.concatenate([ref[p, :] for p in pieces], axis=0)

    def store(refs, pieces, vals):
        for ref, val in zip(refs, vals):
            row = 0
            for p in pieces:
                ref[p, :] = val[row:row + p.size]
                row += p.size

    def body1(nb, carry):
        kb = jnp.maximum(nb - 1, 0)
        per_q, per_k = blk // QUAD, 2 * blk // QUAD
        qp = [pl.ds(pl.multiple_of(c * QUAD_ROWS + nb * per_q, per_q), per_q) for c in range(QUAD)]
        kp = [pl.ds(pl.multiple_of(c * QUAD_ROWS + kb * per_q, per_q), per_k) for c in range(QUAD)]
        res = attend(load(q4, qp), load(k4, kp), load(v4, kp), bias_ref[jnp.minimum(nb, 1)])
        store((o0, m0, d0), qp, res)
        return carry

    lax.fori_loop(0, SEQ // blk, body1, 0, unroll=32)

    n_blk4 = QUAD_ROWS // blk

    def body4(idx, carry):
        nb = idx & (n_blk4 - 1)
        base = (idx - nb) * blk
        kb = jnp.maximum(nb - 1, 0)
        qp = [pl.ds(pl.multiple_of(base + nb * blk, blk), blk)]
        kp = [pl.ds(pl.multiple_of(base + kb * blk, blk), 2 * blk)]
        res = attend(load(q4, qp), load(k4, kp), load(v4, kp), bias_ref[2 + jnp.minimum(nb, 1)])
        store((o1, m1, d1), qp, res)
        return carry

    lax.fori_loop(0, SEQ // blk, body4, 0, unroll=32)

    def body16(idx, carry):
        start = (idx & (QUAD - 1)) * QUAD_ROWS + lax.shift_right_logical(idx, 2)
        kp = [pl.ds(start, 2 * blk, stride=QUAD)]
        k2, v2 = load(k4, kp), load(v4, kp)
        for nb in range(2):
            qp = [pl.ds(start + nb * blk * QUAD, blk, stride=QUAD)]
            store((o2, m2, d2), qp, attend(load(q4, qp), k2, v2, bias_ref[2 + nb]))
        return carry

    lax.fori_loop(0, SEQ // (2 * blk), body16, 0, unroll=16)

    chunk = 512

    def mix(i, carry):
        r = pl.ds(pl.multiple_of(i * chunk, chunk), chunk)
        ma, mb, mc = m0[r, :], m1[r, :], m2[r, :]
        m = jnp.maximum(jnp.maximum(ma, mb), mc)
        wa, wb, wc = jnp.exp2(ma - m), jnp.exp2(mb - m), jnp.exp2(mc - m)
        acc = wa * o0[r, :] + wb * o1[r, :] + wc * o2[r, :]
        den = wa * d0[r, :] + wb * d1[r, :] + wc * d2[r, :]
        per_class = QUAD_ROWS // chunk
        c = i // per_class
        n0 = (i - c * per_class) * chunk
        out_ref[pl.ds(QUAD * n0 + c, chunk, stride=QUAD), :] = acc / den
        return carry

    lax.fori_loop(0, SEQ // chunk, mix, 0)


def _attn(q, k, v, batch):
    spec = pl.BlockSpec((None, SEQ, LANES), lambda b, hp: (b, 0, hp))
    bias = _attn_bias()
    return pl.pallas_call(
        _attn_kernel,
        grid=(batch, ATTN_WIDTH // LANES),
        in_specs=[spec, spec, spec, pl.BlockSpec(bias.shape, lambda b, hp: (0, 0, 0))],
        out_specs=spec,
        out_shape=jax.ShapeDtypeStruct((batch, SEQ, ATTN_WIDTH), F32),
        scratch_shapes=[pltpu.VMEM((SEQ, LANES), F32) for _ in range(12)],
        compiler_params=pltpu.CompilerParams(
            dimension_semantics=("arbitrary", "arbitrary"), vmem_limit_bytes=ATTN_VMEM_LIMIT),
        name="dilated_attn",
    )(q, k, v, jnp.asarray(bias))


def _pack_bf16_pairs(x):
    c = x.shape[1] // 2
    bits = pltpu.bitcast(x.astype(BF16).astype(F32), jnp.uint32)
    return lax.shift_right_logical(bits[:, :c], jnp.uint32(16)) | bits[:, c:]


def _unpack_bf16_pairs(w):
    lo = pltpu.bitcast(lax.shift_left(w, jnp.uint32(16)), F32)
    hi = pltpu.bitcast(w & jnp.uint32(0xFFFF0000), F32)
    return jnp.concatenate([lo, hi], axis=1).astype(BF16)


def _post_kernel(x_ref, attn_ref, u_ref, halo_ref, gate_ref, wao_ref, wmix_ref, pscale_ref,
                 wpo_ref, wo_ref, g2_ref, wr_ref, br_ref,
                 xo_ref, h2_ref, ri_ref, rf_ref, cnt_ref):
    tm = x_ref.shape[0]

    @pl.when(pl.program_id(0) == 0)
    def _():
        cnt_ref[...] = jnp.zeros_like(cnt_ref)

    pos0 = lax.rem(pl.program_id(0) * tm, SEQ)
    pos = pos0 + lax.broadcasted_iota(jnp.int32, (tm, 1), 0)
    u = u_ref[...]
    halo = halo_ref[...] * (pos0 > 0).astype(F32)

    mixed = []
    for gi, w in enumerate(POOL_WINDOWS):
        lo = gi * POOL_GROUP_DIM
        ug = u[:, lo:lo + POOL_GROUP_DIM]
        ext = jnp.concatenate([halo[:, lo:lo + POOL_GROUP_DIM], ug], axis=0)
        shift = 1
        while shift < w:
            ext = ext + pltpu.roll(ext, shift, 0)
            shift *= 2
        cnt = jnp.minimum(pos + 1, w).astype(F32)
        pooled = ext[POOL_HALO:] / cnt - ug
        mixed.append(jnp.dot(pooled.astype(BF16), wmix_ref[gi], preferred_element_type=F32))
    pool_out = (jnp.concatenate(mixed, axis=-1) * pscale_ref[...]).astype(BF16)

    y_a = jnp.dot(attn_ref[...].astype(BF16), wao_ref[...], preferred_element_type=F32)
    y_p = jnp.dot(pool_out, wpo_ref[...], preferred_element_type=F32)
    gates = gate_ref[...]
    merged = gates[:, :D_MODEL].astype(F32) * y_a + gates[:, D_MODEL:].astype(F32) * y_p
    x_new = x_ref[...] + jnp.dot(merged.astype(BF16), wo_ref[...], preferred_element_type=F32)
    xo_ref[...] = x_new

    ms = jnp.mean(x_new * x_new, axis=-1, keepdims=True)
    h2 = x_new * lax.rsqrt(ms + RMS_EPS) * g2_ref[...]
    h_hi = h2.astype(BF16)
    h2_ref[...] = _pack_bf16_pairs(h2)

    h_lo = (h2 - h_hi.astype(F32)).astype(BF16)
    nt = (((1,), (1,)), ((), ()))
    both = lax.dot_general(wr_ref[...], h_hi, nt, preferred_element_type=F32)
    cross = lax.dot_general(wr_ref[:ROUTER_ROWS, :], h_lo, nt, preferred_element_type=F32)
    logits = both[:ROUTER_ROWS] + both[ROUTER_ROWS:] + cross + jnp.concatenate(
        [br_ref[...]] * (tm // LANES), axis=1)

    sub = lax.broadcasted_iota(jnp.int32, (8, tm), 0)
    npg = EXPERTS_PER_GROUP

    def first_max(vals):
        vmax = jnp.max(vals, axis=0, keepdims=True)
        return vmax, jnp.min(jnp.where(vals == vmax, sub, npg), axis=0, keepdims=True)

    def of_group(parts, g_sel):
        out = parts[-1]
        for g in range(N_EXPERT_GROUPS - 2, -1, -1):
            out = jnp.where(g_sel == g, parts[g], out)
        return out

    glog = jnp.where(sub < N_EXPERT_GROUPS, logits[0:8], -jnp.inf)
    gmax, g_sel = first_max(glog)
    g_gate = 1.0 / jnp.sum(jnp.exp(glog - gmax), axis=0, keepdims=True)

    elog = of_group([logits[npg * (g + 1):npg * (g + 2)] for g in range(N_EXPERT_GROUPS)], g_sel)
    e_exp = jnp.exp(elog - jnp.max(elog, axis=0, keepdims=True))
    prob = e_exp / jnp.sum(e_exp, axis=0, keepdims=True)
    p1, i1 = first_max(prob)
    p2, i2 = first_max(jnp.where(sub == i1, -1.0, prob))
    scale = g_gate / (p1 + p2)

    chosen = (sub == i1) | (sub == i2)
    onehot = jnp.concatenate(
        [jnp.where(chosen & (g_sel == g), 1.0, 0.0) for g in range(N_EXPERT_GROUPS)],
        axis=0).astype(BF16)
    t_row = lax.broadcasted_iota(jnp.int32, (tm, tm), 0)
    t_col = lax.broadcasted_iota(jnp.int32, (tm, tm), 1)
    before = jnp.dot(onehot, jnp.where(t_row < t_col, 1.0, 0.0).astype(BF16),
                     preferred_element_type=F32)
    total = jnp.dot(onehot, jnp.ones((tm, LANES), BF16), preferred_element_type=F32)
    seen = cnt_ref[...]
    slot = before + jnp.concatenate([seen] * (tm // LANES), axis=1)
    cnt_ref[...] = seen + total
    slot = of_group([slot[npg * g:npg * (g + 1)] for g in range(N_EXPERT_GROUPS)], g_sel)
    r1 = jnp.sum(jnp.where(sub == i1, slot, 0.0), axis=0, keepdims=True).astype(jnp.int32)
    r2 = jnp.sum(jnp.where(sub == i2, slot, 0.0), axis=0, keepdims=True).astype(jnp.int32)

    e_base = g_sel * npg
    ri_ref[...] = jnp.where(sub == 0, e_base + i1,
                            jnp.where(sub == 1, e_base + i2,
                                      jnp.where(sub == 2, r1, jnp.where(sub == 3, r2, 0))))
    rf_ref[...] = jnp.where(sub == 0, p1 * scale, jnp.where(sub == 1, p2 * scale, 0.0))


ROUTER_ROWS = 8 + N_EXPERTS + 8


def _router_operands(w_group, b_group, w_expert, b_expert):
    d = w_group.shape[0]
    wt = jnp.zeros((ROUTER_ROWS, d), F32)
    wt = wt.at[:N_EXPERT_GROUPS].set(w_group.T)
    wt = wt.at[8:8 + N_EXPERTS].set(w_expert.transpose(0, 2, 1).reshape(N_EXPERTS, d))
    hi = wt.astype(BF16)
    lo = (wt - hi.astype(F32)).astype(BF16)
    b = jnp.zeros((ROUTER_ROWS,), F32).at[:N_EXPERT_GROUPS].set(b_group)
    b = b.at[8:8 + N_EXPERTS].set(b_expert.reshape(-1))
    return jnp.concatenate([hi, lo], axis=0), jnp.broadcast_to(b[:, None], (ROUTER_ROWS, LANES))


def _post(x, attn, u, gates, wao, wmix, pscale, wpo, wo, g2, wr, br):
    n = x.shape[0]
    tm = ROW_TILE
    row = lambda i: (i, 0)
    const = lambda i: (0, 0)
    halo_blocks = tm // POOL_HALO
    return pl.pallas_call(
        _post_kernel,
        grid=(n // tm,),
        in_specs=[
            pl.BlockSpec((tm, D_MODEL), row),
            pl.BlockSpec((tm, ATTN_WIDTH), row),
            pl.BlockSpec((tm, POOL_WIDTH), row),
            pl.BlockSpec((POOL_HALO, POOL_WIDTH),
                         lambda i: (jnp.maximum(i * halo_blocks - 1, 0), 0)),
            pl.BlockSpec((tm, 2 * D_MODEL), row),
            pl.BlockSpec((ATTN_WIDTH, D_MODEL), const),
            pl.BlockSpec((len(POOL_WINDOWS), POOL_GROUP_DIM, POOL_GROUP_DIM), lambda i: (0, 0, 0)),
            pl.BlockSpec((1, POOL_WIDTH), const),
            pl.BlockSpec((POOL_WIDTH, D_MODEL), const),
            pl.BlockSpec((D_MODEL, D_MODEL), const),
            pl.BlockSpec((1, D_MODEL), const),
            pl.BlockSpec((2 * ROUTER_ROWS, D_MODEL), const),
            pl.BlockSpec((ROUTER_ROWS, LANES), const),
        ],
        out_specs=[
            pl.BlockSpec((tm, D_MODEL), row),
            pl.BlockSpec((tm, D_MODEL // 2), row),
            pl.BlockSpec((8, tm), lambda i: (0, i)),
            pl.BlockSpec((8, tm), lambda i: (0, i)),
            pl.BlockSpec((N_EXPERTS, LANES), const),
        ],
        out_shape=[
            jax.ShapeDtypeStruct((n, D_MODEL), F32),
            jax.ShapeDtypeStruct((n, D_MODEL // 2), jnp.uint32),
            jax.ShapeDtypeStruct((8, n), jnp.int32),
            jax.ShapeDtypeStruct((8, n), F32),
            jax.ShapeDtypeStruct((N_EXPERTS, LANES), F32),
        ],
        compiler_params=_params(1),
        name="post_attn_router",
    )(x, attn, u, u, gates, wao, wmix, pscale, wpo, wo, g2, wr, br)


ZERO_ROWS = 8


def _experts_kernel(layer, be_ref, nused_ref, tok_ref, next_e_ref, h_hbm, wg_hbm, wu_hbm, wd_hbm,
                    y_ref, h_vmem, xs0, xs1, wg_f32, wu_f32, wd_f32, wg_bf, wu_bf, wd_bf,
                    h_sem, w_sems):
    i = pl.program_id(0)
    bm = xs0.shape[0]
    n_tok = h_hbm.shape[0]
    changed = (i == 0) | (be_ref[i] != be_ref[jnp.maximum(i - 1, 0)])
    used = i < nused_ref[0]
    cur = i & 1

    def weight_copies(e):
        return [pltpu.make_async_copy(src.at[layer, e], dst, w_sems.at[n])
                for n, (src, dst) in enumerate(
                    ((wg_hbm, wg_f32), (wu_hbm, wu_f32), (wd_hbm, wd_f32)))]

    @pl.when(i == 0)
    def _():
        copy = pltpu.make_async_copy(h_hbm, h_vmem.at[pl.ds(0, n_tok)], h_sem)
        copy.start()

        @pl.when(used)
        def _():
            for c in weight_copies(be_ref[0]):
                c.start()
        h_vmem[pl.ds(n_tok, ZERO_ROWS), :] = jnp.zeros((ZERO_ROWS, h_vmem.shape[1]), h_vmem.dtype)
        copy.wait()

        def gather(j, carry):
            xs0[pl.ds(j, 1), :] = h_vmem[pl.ds(tok_ref[j], 1), :]
            return carry
        lax.fori_loop(0, bm, gather, 0, unroll=8)

    @pl.when(changed & used)
    def _():
        for c in weight_copies(be_ref[i]):
            c.wait()
        wg_bf[...] = wg_f32[...].astype(BF16)
        wu_bf[...] = wu_f32[...].astype(BF16)
        wd_bf[...] = wd_f32[...].astype(BF16)

        @pl.when(next_e_ref[i] != be_ref[i])
        def _():
            for c in weight_copies(next_e_ref[i]):
                c.start()

    def block(x_ref, x_next_ref):
        nxt = jnp.where(i + 1 < nused_ref[0], i + 1, i) * bm
        for j in range(bm):
            x_next_ref[pl.ds(j, 1), :] = h_vmem[pl.ds(tok_ref[nxt + j], 1), :]

        x = _unpack_bf16_pairs(x_ref[...])
        a = jnp.dot(x, wg_bf[...], preferred_element_type=F32)
        b = jnp.dot(x, wu_bf[...], preferred_element_type=F32)
        mid = (a * jax.nn.sigmoid(a) * b).astype(BF16)
        y_ref[...] = _pack_bf16_pairs(jnp.dot(mid, wd_bf[...], preferred_element_type=F32))

    pl.when(used & (cur == 0))(lambda: block(xs0, xs1))
    pl.when(used & (cur == 1))(lambda: block(xs1, xs0))

    @pl.when(jnp.logical_not(used))
    def _():
        y_ref[...] = jnp.zeros_like(y_ref)


def _experts(block_expert, n_used, sorted_tok, next_expert, h2_packed, w_gate, w_up, w_down, layer):
    n_tok, width = h2_packed.shape
    bm = EXPERT_BLOCK
    any_space = pl.BlockSpec(memory_space=pl.ANY)
    grid_spec = pltpu.PrefetchScalarGridSpec(
        num_scalar_prefetch=4,
        grid=(sorted_tok.shape[0] // bm,),
        in_specs=[any_space, any_space, any_space, any_space],
        out_specs=pl.BlockSpec((bm, width), lambda i, *_: (i, 0)),
        scratch_shapes=[
            pltpu.VMEM((n_tok + ZERO_ROWS, width), h2_packed.dtype),
            pltpu.VMEM((bm, width), h2_packed.dtype),
            pltpu.VMEM((bm, width), h2_packed.dtype),
            pltpu.VMEM((D_MODEL, EXPERT_HIDDEN), F32),
            pltpu.VMEM((D_MODEL, EXPERT_HIDDEN), F32),
            pltpu.VMEM((EXPERT_HIDDEN, D_MODEL), F32),
            pltpu.VMEM((D_MODEL, EXPERT_HIDDEN), BF16),
            pltpu.VMEM((D_MODEL, EXPERT_HIDDEN), BF16),
            pltpu.VMEM((EXPERT_HIDDEN, D_MODEL), BF16),
            pltpu.SemaphoreType.DMA(()),
            pltpu.SemaphoreType.DMA((3,)),
        ],
    )
    return pl.pallas_call(
        functools.partial(_experts_kernel, layer),
        grid_spec=grid_spec,
        out_shape=jax.ShapeDtypeStruct((sorted_tok.shape[0], width), h2_packed.dtype),
        compiler_params=pltpu.CompilerParams(
            dimension_semantics=("arbitrary",), vmem_limit_bytes=EXPERTS_VMEM_LIMIT),
        name="experts",
    )(block_expert, n_used, sorted_tok, next_expert, h2_packed, w_gate, w_up, w_down)


def _sorted_tokens_kernel(dest_ref, pad_lo_ref, pad_hi_ref, tok_ref):
    n_tok = dest_ref.shape[0] // TOP_K

    def fill(p, carry):
        tok_ref[p] = n_tok
        return carry

    def place(t, carry):
        for k in range(TOP_K):
            tok_ref[dest_ref[k * n_tok + t]] = t
        return carry

    for s in range(pad_lo_ref.shape[0]):
        lax.fori_loop(pad_lo_ref[s], pad_hi_ref[s], fill, 0)
    lax.fori_loop(0, n_tok, place, 0, unroll=8)


def _sorted_tokens(dest, pad_lo, pad_hi, buf_len):
    smem = pl.BlockSpec(memory_space=pltpu.SMEM)
    return pl.pallas_call(
        _sorted_tokens_kernel,
        in_specs=[smem, smem, smem],
        out_specs=smem,
        out_shape=jax.ShapeDtypeStruct((buf_len,), jnp.int32),
        name="moe_sorted_tokens",
    )(dest, pad_lo, pad_hi)


def _combine_kernel(dest_ref, x_ref, gate_ref, y_ref, out_ref, ybuf, sems):
    tm = x_ref.shape[0]
    i = pl.program_id(0)
    cur = lax.rem(i, 2)

    n_tok = pl.num_programs(0) * tm

    def row_copy(step, slot, r, k):
        d = dest_ref[k * n_tok + step * tm + r]
        return pltpu.make_async_copy(y_ref.at[pl.ds(d, 1)], ybuf.at[slot, k, pl.ds(r, 1)],
                                     sems.at[slot])

    def fetch(step, slot):
        def issue(r, carry):
            for k in range(TOP_K):
                row_copy(step, slot, r, k).start()
            return carry
        lax.fori_loop(0, tm, issue, 0, unroll=8)

    @pl.when(i == 0)
    def _():
        fetch(0, 0)

    @pl.when(i + 1 < pl.num_programs(0))
    def _():
        fetch(i + 1, 1 - cur)

    def drain(r, carry):
        for k in range(TOP_K):
            row_copy(i, cur, r, k).wait()
        return carry
    lax.fori_loop(0, tm, drain, 0, unroll=8)

    g = gate_ref[...]
    half = x_ref.shape[1] // 2
    for lo, unpack in ((0, lambda w: lax.shift_left(w, jnp.uint32(16))),
                       (half, lambda w: w & jnp.uint32(0xFFFF0000))):
        y0 = pltpu.bitcast(unpack(ybuf[cur, 0]), F32)
        y1 = pltpu.bitcast(unpack(ybuf[cur, 1]), F32)
        out_ref[:, lo:lo + half] = x_ref[:, lo:lo + half] + g[:, 0:1] * y0 + g[:, 1:2] * y1


def _combine(dest, x_new, gates, y_rows):
    n, d = x_new.shape
    tm = ROW_TILE
    grid_spec = pltpu.PrefetchScalarGridSpec(
        num_scalar_prefetch=1,
        grid=(n // tm,),
        in_specs=[
            pl.BlockSpec((tm, d), lambda i, dest: (i, 0)),
            pl.BlockSpec((tm, LANES), lambda i, dest: (i, 0)),
            pl.BlockSpec(memory_space=pl.ANY),
        ],
        out_specs=pl.BlockSpec((tm, d), lambda i, dest: (i, 0)),
        scratch_shapes=[
            pltpu.VMEM((2, TOP_K, tm, y_rows.shape[1]), y_rows.dtype),
            pltpu.SemaphoreType.DMA((2,)),
        ],
    )
    return pl.pallas_call(
        _combine_kernel,
        grid_spec=grid_spec,
        out_shape=jax.ShapeDtypeStruct((n, d), F32),
        compiler_params=_params(1),
        name="moe_combine",
    )(dest, x_new, gates, y_rows)


def _dispatch_plan(expert_id, slot, counts):
    bm = EXPERT_BLOCK
    n_assign = expert_id.size
    padded = (counts + bm - 1) // bm * bm
    pend = jnp.cumsum(padded)
    pstart = pend - padded
    experts = jnp.arange(N_EXPERTS, dtype=jnp.int32)
    seg_start = jnp.sum(jnp.where(expert_id[..., None] == experts, pstart, 0), axis=-1)
    dest = (seg_start + slot).reshape(-1).astype(jnp.int32)
    buf_len = n_assign + N_EXPERTS * bm
    block_start = jnp.arange(buf_len // bm, dtype=jnp.int32) * bm
    n_used = (pend[-1:] // bm).astype(jnp.int32)
    block_start = jnp.minimum(block_start, pend[-1] - bm)
    block_expert = jnp.sum((pend[None, :] <= block_start[:, None]).astype(jnp.int32), axis=1)
    block_expert = jnp.minimum(block_expert, N_EXPERTS - 1)
    later = jnp.where(block_expert[None, :] > block_expert[:, None], block_expert[None, :], N_EXPERTS)
    next_expert = jnp.min(later, axis=1)
    next_expert = jnp.where(next_expert == N_EXPERTS, block_expert, next_expert).astype(jnp.int32)
    pad_lo = jnp.concatenate([pstart + counts, pend[-1:]]).astype(jnp.int32)
    pad_hi = jnp.concatenate([pend, jnp.full((1,), buf_len, pend.dtype)]).astype(jnp.int32)
    return dest, block_expert, next_expert, n_used, (pad_lo, pad_hi), buf_len


def kernel(x, norm1_g, w_in, q_norm_g, k_norm_g, w_attn_out, w_pool_mix, pool_scale, w_pool_out,
           w_o, norm2_g, w_router_group, b_router_group, w_router_expert, b_router_expert,
           w_exp_gate, w_exp_up, w_exp_down):
    batch, seq, d = x.shape
    assert (seq, d) == (SEQ, D_MODEL)
    n_tok = batch * seq
    depth = w_in.shape[0]
    half = ATTN_WIDTH // 2
    hsum = (jnp.arange(half)[:, None] // HEAD_DIM == jnp.arange(half)[None, :] // HEAD_DIM).astype(BF16)

    xf = x.reshape(n_tok, d)
    for layer in range(depth):
        qg = jnp.tile(q_norm_g[layer], N_HEADS)[None, :]
        kg = jnp.tile(k_norm_g[layer], N_HEADS)[None, :]
        q, k, v, u, gates = _inproj(xf, norm1_g[layer][None, :], w_in[layer].astype(BF16), qg, kg, hsum)
        attn = _attn(q.reshape(batch, seq, ATTN_WIDTH), k.reshape(batch, seq, ATTN_WIDTH),
                     v.reshape(batch, seq, ATTN_WIDTH), batch).reshape(n_tok, ATTN_WIDTH)

        w_router, b_router = _router_operands(
            w_router_group[layer], b_router_group[layer], w_router_expert[layer], b_router_expert[layer])
        x_new, h2, route_i, route_f, counts = _post(
            xf, attn, u, gates, w_attn_out[layer].astype(BF16), w_pool_mix[layer].astype(BF16),
            pool_scale[layer][None, :], w_pool_out[layer].astype(BF16), w_o[layer].astype(BF16),
            norm2_g[layer][None, :], w_router, b_router)

        dest, block_expert, next_expert, n_used, pads, buf_len = _dispatch_plan(
            route_i[:TOP_K], route_i[TOP_K:2 * TOP_K], counts[:, 0].astype(jnp.int32))
        sorted_tok = _sorted_tokens(dest, *pads, buf_len)
        y_rows = _experts(block_expert, n_used, sorted_tok, next_expert, h2,
                          w_exp_gate, w_exp_up, w_exp_down, layer)
        gate_rows = jnp.pad(route_f[:TOP_K].T, ((0, 0), (0, LANES - TOP_K)))
        xf = _combine(dest, x_new, gate_rows, y_rows)
    return xf.reshape(batch, seq, d)
```

```python
import functools

import jax
import jax.numpy as jnp
import numpy as np
from jax import lax
from jax.experimental import pallas as pl
from jax.experimental.pallas import tpu as pltpu

D_MODEL = 1024
SEQ = 4096
N_HEADS = 8
HEAD_DIM = 64
ATTN_WIDTH = N_HEADS * HEAD_DIM
DILATIONS = (1, 4, 16)
ATTN_BLOCK = 128
POOL_WINDOWS = (2, 4, 8, 16)
POOL_GROUP_DIM = 128
POOL_WIDTH = len(POOL_WINDOWS) * POOL_GROUP_DIM
POOL_HALO = 16
IN_PROJ_WIDTH = 3 * ATTN_WIDTH + POOL_WIDTH + 2 * D_MODEL
N_EXPERT_GROUPS = 4
EXPERTS_PER_GROUP = 8
N_EXPERTS = N_EXPERT_GROUPS * EXPERTS_PER_GROUP
TOP_K = 2
EXPERT_HIDDEN = 512
RMS_EPS = 1e-6
NEG_INF = -1e30

LANES = 128
ROW_TILE = 256
EXPERT_BLOCK = 256
VMEM_LIMIT = 48 * 1024 * 1024
ATTN_VMEM_LIMIT = 56 * 1024 * 1024
EXPERTS_VMEM_LIMIT = 56 * 1024 * 1024

F32 = jnp.float32
BF16 = jnp.bfloat16


def _params(n_axes):
    return pltpu.CompilerParams(
        dimension_semantics=("arbitrary",) * n_axes, vmem_limit_bytes=VMEM_LIMIT)


def _inproj_kernel(x_ref, g_ref, w_ref, qg_ref, kg_ref, hsum_ref,
                   q_ref, k_ref, v_ref, u_ref, gate_ref):
    x = x_ref[...]
    ms = jnp.mean(x * x, axis=-1, keepdims=True)
    h = (x * lax.rsqrt(ms + RMS_EPS) * g_ref[...]).astype(BF16)

    def proj(lo, hi):
        return jnp.dot(h, w_ref[:, lo:hi], preferred_element_type=F32)

    def head_norm(t, gain):
        sq = (t * t).astype(BF16)
        half = ATTN_WIDTH // 2
        ssq = jnp.concatenate(
            [jnp.dot(sq[:, j * half:(j + 1) * half], hsum_ref[...], preferred_element_type=F32)
             for j in range(2)], axis=-1)
        return t * lax.rsqrt(ssq * (1.0 / HEAD_DIM) + RMS_EPS) * gain

    w = ATTN_WIDTH
    q_ref[...] = head_norm(proj(0, w), qg_ref[...])
    k_ref[...] = head_norm(proj(w, 2 * w), kg_ref[...])
    v_ref[...] = proj(2 * w, 3 * w)
    u_ref[...] = proj(3 * w, 3 * w + POOL_WIDTH)
    base = 3 * w + POOL_WIDTH
    for j in range(2 * D_MODEL // 512):
        gate_ref[:, j * 512:(j + 1) * 512] = jax.nn.sigmoid(
            proj(base + j * 512, base + (j + 1) * 512)).astype(BF16)


def _inproj(x, g1, w_in, qg, kg, hsum):
    n = x.shape[0]
    row = lambda i: (i, 0)
    const = lambda i: (0, 0)
    return pl.pallas_call(
        _inproj_kernel,
        grid=(n // ROW_TILE,),
        in_specs=[
            pl.BlockSpec((ROW_TILE, D_MODEL), row),
            pl.BlockSpec((1, D_MODEL), const),
            pl.BlockSpec((D_MODEL, IN_PROJ_WIDTH), const),
            pl.BlockSpec((1, ATTN_WIDTH), const),
            pl.BlockSpec((1, ATTN_WIDTH), const),
            pl.BlockSpec((ATTN_WIDTH // 2, ATTN_WIDTH // 2), const),
        ],
        out_specs=[
            pl.BlockSpec((ROW_TILE, ATTN_WIDTH), row),
            pl.BlockSpec((ROW_TILE, ATTN_WIDTH), row),
            pl.BlockSpec((ROW_TILE, ATTN_WIDTH), row),
            pl.BlockSpec((ROW_TILE, POOL_WIDTH), row),
            pl.BlockSpec((ROW_TILE, 2 * D_MODEL), row),
        ],
        out_shape=[
            jax.ShapeDtypeStruct((n, ATTN_WIDTH), F32),
            jax.ShapeDtypeStruct((n, ATTN_WIDTH), F32),
            jax.ShapeDtypeStruct((n, ATTN_WIDTH), F32),
            jax.ShapeDtypeStruct((n, POOL_WIDTH), F32),
            jax.ShapeDtypeStruct((n, 2 * D_MODEL), BF16),
        ],
        compiler_params=_params(1),
        name="inproj",
    )(x, g1, w_in, qg, kg, hsum)


ATTN_UNROLL = 32
QUAD = 4
QUAD_ROWS = SEQ // QUAD
LOG2E = 1.4426950408889634


def _attn_bias():
    blk = ATTN_BLOCK
    r = np.arange(2 * blk) % blk
    c = np.arange(2 * blk)
    per_q, per_k = blk // QUAD, 2 * blk // QUAD
    tq = QUAD * (r % per_q) + r // per_q
    tk = QUAD * (c % per_k) + c // per_k
    masks = []
    for tq_, tk_ in ((tq, tk), (r, c)):
        first = tq_[:, None] - tk_[None, :]
        later = first + blk
        masks += [first >= 0, (later >= 0) & (later <= blk)]
    return np.where(np.stack(masks), 0.0, NEG_INF).astype(np.float32)


def _attn_kernel(q_ref, k_ref, v_ref, bias_ref, out_ref, q4, k4, v4,
                 o0, o1, o2, m0, m1, m2, d0, d1, d2):
    blk = ATTN_BLOCK
    head_a = lax.broadcasted_iota(jnp.int32, (blk, LANES), 1) < HEAD_DIM
    ones_cols = jnp.ones((2 * blk, LANES), BF16)

    for c in range(QUAD):
        dst = pl.ds(c * QUAD_ROWS, QUAD_ROWS)
        src = pl.ds(c, QUAD_ROWS, stride=QUAD)
        q4[dst, :] = q_ref[src, :] * (HEAD_DIM ** -0.5 * LOG2E)
        k4[dst, :] = k_ref[src, :]
        v4[dst, :] = v_ref[src, :]

    def attend(q2, k2, v2, bias):
        qs = jnp.concatenate(
            [jnp.where(head_a, q2, 0.0), jnp.where(head_a, 0.0, q2)], axis=0).astype(BF16)
        s = lax.dot_general(qs, k2.astype(BF16), (((1,), (1,)), ((), ())),
                            preferred_element_type=F32) + bias
        m = jnp.max(s, axis=-1, keepdims=True)
        e = jnp.exp2(s - m).astype(BF16)
        r = jnp.dot(e, jnp.concatenate([v2.astype(BF16), ones_cols], axis=1),
                    preferred_element_type=F32)
        mb = jnp.broadcast_to(m, (2 * blk, LANES))
        return (jnp.where(head_a, r[:blk, :LANES], r[blk:, :LANES]),
                jnp.where(head_a, mb[:blk], mb[blk:]),
                jnp.where(head_a, r[:blk, LANES:], r[blk:, LANES:]))

    def load(ref, pieces):
        return jnp.concatenate([ref[p, :] for p in pieces], axis=0)

    def store(refs, pieces, vals):
        for ref, val in zip(refs, vals):
            row = 0
            for p in pieces:
                ref[p, :] = val[row:row + p.size]
                row += p.size

    def body1(nb, carry):
        kb = jnp.maximum(nb - 1, 0)
        per_q, per_k = blk // QUAD, 2 * blk // QUAD
        qp = [pl.ds(pl.multiple_of(c * QUAD_ROWS + nb * per_q, per_q), per_q) for c in range(QUAD)]
        kp = [pl.ds(pl.multiple_of(c * QUAD_ROWS + kb * per_q, per_q), per_k) for c in range(QUAD)]
        res = attend(load(q4, qp), load(k4, kp), load(v4, kp), bias_ref[jnp.minimum(nb, 1)])
        store((o0, m0, d0), qp, res)
        return carry

    lax.fori_loop(0, SEQ // blk, body1, 0, unroll=ATTN_UNROLL)

    n_blk4 = QUAD_ROWS // blk

    def body4(idx, carry):
        nb = idx & (n_blk4 - 1)
        base = (idx - nb) * blk
        kb = jnp.maximum(nb - 1, 0)
        qp = [pl.ds(pl.multiple_of(base + nb * blk, blk), blk)]
        kp = [pl.ds(pl.multiple_of(base + kb * blk, blk), 2 * blk)]
        res = attend(load(q4, qp), load(k4, kp), load(v4, kp), bias_ref[2 + jnp.minimum(nb, 1)])
        store((o1, m1, d1), qp, res)
        return carry

    lax.fori_loop(0, SEQ // blk, body4, 0, unroll=ATTN_UNROLL)

    def body16(idx, carry):
        start = (idx & (QUAD - 1)) * QUAD_ROWS + lax.shift_right_logical(idx, 2)
        kp = [pl.ds(start, 2 * blk, stride=QUAD)]
        k2, v2 = load(k4, kp), load(v4, kp)
        for nb in range(2):
            qp = [pl.ds(start + nb * blk * QUAD, blk, stride=QUAD)]
            store((o2, m2, d2), qp, attend(load(q4, qp), k2, v2, bias_ref[2 + nb]))
        return carry

    lax.fori_loop(0, SEQ // (2 * blk), body16, 0, unroll=ATTN_UNROLL // 2)

    chunk = 512

    def mix(i, carry):
        r = pl.ds(pl.multiple_of(i * chunk, chunk), chunk)
        ma, mb, mc = m0[r, :], m1[r, :], m2[r, :]
        m = jnp.maximum(jnp.maximum(ma, mb), mc)
        wa, wb, wc = jnp.exp2(ma - m), jnp.exp2(mb - m), jnp.exp2(mc - m)
        acc = wa * o0[r, :] + wb * o1[r, :] + wc * o2[r, :]
        den = wa * d0[r, :] + wb * d1[r, :] + wc * d2[r, :]
        per_class = QUAD_ROWS // chunk
        c = i // per_class
        n0 = (i - c * per_class) * chunk
        out_ref[pl.ds(QUAD * n0 + c, chunk, stride=QUAD), :] = acc / den
        return carry

    lax.fori_loop(0, SEQ // chunk, mix, 0)


def _attn(q, k, v, batch):
    spec = pl.BlockSpec((None, SEQ, LANES), lambda b, hp: (b, 0, hp))
    bias = _attn_bias()
    return pl.pallas_call(
        _attn_kernel,
        grid=(batch, ATTN_WIDTH // LANES),
        in_specs=[spec, spec, spec, pl.BlockSpec(bias.shape, lambda b, hp: (0, 0, 0))],
        out_specs=spec,
        out_shape=jax.ShapeDtypeStruct((batch, SEQ, ATTN_WIDTH), F32),
        scratch_shapes=[pltpu.VMEM((SEQ, LANES), F32) for _ in range(12)],
        compiler_params=pltpu.CompilerParams(
            dimension_semantics=("arbitrary", "arbitrary"), vmem_limit_bytes=ATTN_VMEM_LIMIT),
        name="dilated_attn",
    )(q, k, v, jnp.asarray(bias))


def _pack_bf16_pairs(x):
    c = x.shape[1] // 2
    bits = pltpu.bitcast(x.astype(BF16).astype(F32), jnp.uint32)
    return lax.shift_right_logical(bits[:, :c], jnp.uint32(16)) | bits[:, c:]


def _unpack_bf16_pairs(w):
    lo = pltpu.bitcast(lax.shift_left(w, jnp.uint32(16)), F32)
    hi = pltpu.bitcast(w & jnp.uint32(0xFFFF0000), F32)
    return jnp.concatenate([lo, hi], axis=1).astype(BF16)


def _post_kernel(x_ref, attn_ref, u_ref, halo_ref, gate_ref, wao_ref, wmix_ref, pscale_ref,
                 wpo_ref, wo_ref, g2_ref, wr_ref, br_ref,
                 xo_ref, h2_ref, ri_ref, rf_ref, cnt_ref):
    tm = x_ref.shape[0]

    @pl.when(pl.program_id(0) == 0)
    def _():
        cnt_ref[...] = jnp.zeros_like(cnt_ref)

    pos0 = lax.rem(pl.program_id(0) * tm, SEQ)
    pos = pos0 + lax.broadcasted_iota(jnp.int32, (tm, 1), 0)
    u = u_ref[...]
    halo = halo_ref[...] * (pos0 > 0).astype(F32)

    mixed = []
    for gi, w in enumerate(POOL_WINDOWS):
        lo = gi * POOL_GROUP_DIM
        ug = u[:, lo:lo + POOL_GROUP_DIM]
        ext = jnp.concatenate([halo[:, lo:lo + POOL_GROUP_DIM], ug], axis=0)
        shift = 1
        while shift < w:
            ext = ext + pltpu.roll(ext, shift, 0)
            shift *= 2
        cnt = jnp.minimum(pos + 1, w).astype(F32)
        pooled = ext[POOL_HALO:] / cnt - ug
        mixed.append(jnp.dot(pooled.astype(BF16), wmix_ref[gi], preferred_element_type=F32))
    pool_out = (jnp.concatenate(mixed, axis=-1) * pscale_ref[...]).astype(BF16)

    y_a = jnp.dot(attn_ref[...].astype(BF16), wao_ref[...], preferred_element_type=F32)
    y_p = jnp.dot(pool_out, wpo_ref[...], preferred_element_type=F32)
    gates = gate_ref[...]
    merged = gates[:, :D_MODEL].astype(F32) * y_a + gates[:, D_MODEL:].astype(F32) * y_p
    x_new = x_ref[...] + jnp.dot(merged.astype(BF16), wo_ref[...], preferred_element_type=F32)
    xo_ref[...] = x_new

    ms = jnp.mean(x_new * x_new, axis=-1, keepdims=True)
    h2 = x_new * lax.rsqrt(ms + RMS_EPS) * g2_ref[...]
    h_hi = h2.astype(BF16)
    h2_ref[...] = _pack_bf16_pairs(h2)

    h_lo = (h2 - h_hi.astype(F32)).astype(BF16)
    nt = (((1,), (1,)), ((), ()))
    both = lax.dot_general(wr_ref[...], h_hi, nt, preferred_element_type=F32)
    cross = lax.dot_general(wr_ref[:ROUTER_ROWS, :], h_lo, nt, preferred_element_type=F32)
    logits = both[:ROUTER_ROWS] + both[ROUTER_ROWS:] + cross + jnp.concatenate(
        [br_ref[...]] * (tm // LANES), axis=1)

    sub = lax.broadcasted_iota(jnp.int32, (8, tm), 0)
    npg = EXPERTS_PER_GROUP

    def first_max(vals):
        vmax = jnp.max(vals, axis=0, keepdims=True)
        return vmax, jnp.min(jnp.where(vals == vmax, sub, npg), axis=0, keepdims=True)

    def of_group(parts, g_sel):
        out = parts[-1]
        for g in range(N_EXPERT_GROUPS - 2, -1, -1):
            out = jnp.where(g_sel == g, parts[g], out)
        return out

    glog = jnp.where(sub < N_EXPERT_GROUPS, logits[0:8], -jnp.inf)
    gmax, g_sel = first_max(glog)
    g_gate = 1.0 / jnp.sum(jnp.exp(glog - gmax), axis=0, keepdims=True)

    elog = of_group([logits[npg * (g + 1):npg * (g + 2)] for g in range(N_EXPERT_GROUPS)], g_sel)
    e_exp = jnp.exp(elog - jnp.max(elog, axis=0, keepdims=True))
    prob = e_exp / jnp.sum(e_exp, axis=0, keepdims=True)
    p1, i1 = first_max(prob)
    p2, i2 = first_max(jnp.where(sub == i1, -1.0, prob))
    scale = g_gate / (p1 + p2)

    chosen = (sub == i1) | (sub == i2)
    onehot = jnp.concatenate(
        [jnp.where(chosen & (g_sel == g), 1.0, 0.0) for g in range(N_EXPERT_GROUPS)],
        axis=0).astype(BF16)
    t_row = lax.broadcasted_iota(jnp.int32, (tm, tm), 0)
    t_col = lax.broadcasted_iota(jnp.int32, (tm, tm), 1)
    before = jnp.dot(onehot, jnp.where(t_row < t_col, 1.0, 0.0).astype(BF16),
                     preferred_element_type=F32)
    total = jnp.dot(onehot, jnp.ones((tm, LANES), BF16), preferred_element_type=F32)
    seen = cnt_ref[...]
    slot = before + jnp.concatenate([seen] * (tm // LANES), axis=1)
    cnt_ref[...] = seen + total
    slot = of_group([slot[npg * g:npg * (g + 1)] for g in range(N_EXPERT_GROUPS)], g_sel)
    r1 = jnp.sum(jnp.where(sub == i1, slot, 0.0), axis=0, keepdims=True).astype(jnp.int32)
    r2 = jnp.sum(jnp.where(sub == i2, slot, 0.0), axis=0, keepdims=True).astype(jnp.int32)

    e_base = g_sel * npg
    ri_ref[...] = jnp.where(sub == 0, e_base + i1,
                            jnp.where(sub == 1, e_base + i2,
                                      jnp.where(sub == 2, r1, jnp.where(sub == 3, r2, 0))))
    rf_ref[...] = jnp.where(sub == 0, p1 * scale, jnp.where(sub == 1, p2 * scale, 0.0))


ROUTER_ROWS = 8 + N_EXPERTS + 8


def _router_operands(w_group, b_group, w_expert, b_expert):
    d = w_group.shape[0]
    wt = jnp.zeros((ROUTER_ROWS, d), F32)
    wt = wt.at[:N_EXPERT_GROUPS].set(w_group.T)
    wt = wt.at[8:8 + N_EXPERTS].set(w_expert.transpose(0, 2, 1).reshape(N_EXPERTS, d))
    hi = wt.astype(BF16)
    lo = (wt - hi.astype(F32)).astype(BF16)
    b = jnp.zeros((ROUTER_ROWS,), F32).at[:N_EXPERT_GROUPS].set(b_group)
    b = b.at[8:8 + N_EXPERTS].set(b_expert.reshape(-1))
    return jnp.concatenate([hi, lo], axis=0), jnp.broadcast_to(b[:, None], (ROUTER_ROWS, LANES))


def _post(x, attn, u, gates, wao, wmix, pscale, wpo, wo, g2, wr, br):
    n = x.shape[0]
    tm = ROW_TILE
    row = lambda i: (i, 0)
    const = lambda i: (0, 0)
    halo_blocks = tm // POOL_HALO
    return pl.pallas_call(
        _post_kernel,
        grid=(n // tm,),
        in_specs=[
            pl.BlockSpec((tm, D_MODEL), row),
            pl.BlockSpec((tm, ATTN_WIDTH), row),
            pl.BlockSpec((tm, POOL_WIDTH), row),
            pl.BlockSpec((POOL_HALO, POOL_WIDTH),
                         lambda i: (jnp.maximum(i * halo_blocks - 1, 0), 0)),
            pl.BlockSpec((tm, 2 * D_MODEL), row),
            pl.BlockSpec((ATTN_WIDTH, D_MODEL), const),
            pl.BlockSpec((len(POOL_WINDOWS), POOL_GROUP_DIM, POOL_GROUP_DIM), lambda i: (0, 0, 0)),
            pl.BlockSpec((1, POOL_WIDTH), const),
            pl.BlockSpec((POOL_WIDTH, D_MODEL), const),
            pl.BlockSpec((D_MODEL, D_MODEL), const),
            pl.BlockSpec((1, D_MODEL), const),
            pl.BlockSpec((2 * ROUTER_ROWS, D_MODEL), const),
            pl.BlockSpec((ROUTER_ROWS, LANES), const),
        ],
        out_specs=[
            pl.BlockSpec((tm, D_MODEL), row),
            pl.BlockSpec((tm, D_MODEL // 2), row),
            pl.BlockSpec((8, tm), lambda i: (0, i)),
            pl.BlockSpec((8, tm), lambda i: (0, i)),
            pl.BlockSpec((N_EXPERTS, LANES), const),
        ],
        out_shape=[
            jax.ShapeDtypeStruct((n, D_MODEL), F32),
            jax.ShapeDtypeStruct((n, D_MODEL // 2), jnp.uint32),
            jax.ShapeDtypeStruct((8, n), jnp.int32),
            jax.ShapeDtypeStruct((8, n), F32),
            jax.ShapeDtypeStruct((N_EXPERTS, LANES), F32),
        ],
        compiler_params=_params(1),
        name="post_attn_router",
    )(x, attn, u, u, gates, wao, wmix, pscale, wpo, wo, g2, wr, br)


ZERO_ROWS = 8
SLAB = 8


def _experts_kernel(layer, be_ref, nused_ref, tok_ref, next_e_ref, h_hbm, wg_hbm, wu_hbm, wd_hbm,
                    y_ref, h_vmem, xs0, xs1, wg_f32, wu_f32, wd_f32, wg_bf, wu_bf, wd_bf,
                    h_sem, w_sems):
    i = pl.program_id(0)
    bm = xs0.shape[0]
    n_tok = h_hbm.shape[0]
    changed = (i == 0) | (be_ref[i] != be_ref[jnp.maximum(i - 1, 0)])
    used = i < nused_ref[0]
    cur = i & 1

    def weight_copies(e):
        return [pltpu.make_async_copy(src.at[layer, e], dst, w_sems.at[n])
                for n, (src, dst) in enumerate(
                    ((wg_hbm, wg_f32), (wu_hbm, wu_f32), (wd_hbm, wd_f32)))]

    @pl.when(i == 0)
    def _():
        @pl.when(used)
        def _():
            for c in weight_copies(be_ref[0]):
                c.start(priority=1)
        copy = pltpu.make_async_copy(h_hbm, h_vmem.at[pl.ds(0, n_tok)], h_sem)
        copy.start()
        h_vmem[pl.ds(n_tok, ZERO_ROWS), :] = jnp.zeros((ZERO_ROWS, h_vmem.shape[1]), h_vmem.dtype)
        copy.wait()

        def gather(j, carry):
            xs0[pl.ds(j, 1), :] = h_vmem[pl.ds(tok_ref[j], 1), :]
            return carry
        lax.fori_loop(0, bm, gather, 0, unroll=8)

    @pl.when(changed & used)
    def _():
        for c in weight_copies(be_ref[i]):
            c.wait()
        wg_bf[...] = wg_f32[...].astype(BF16)
        wu_bf[...] = wu_f32[...].astype(BF16)
        wd_bf[...] = wd_f32[...].astype(BF16)

        @pl.when(next_e_ref[i] != be_ref[i])
        def _():
            for n, c in enumerate(weight_copies(next_e_ref[i])):
                c.start(priority=n % 2)

    def block(x_ref, x_next_ref):
        nxt = jnp.where(i + 1 < nused_ref[0], i + 1, i) * bm
        for j in range(bm):
            x_next_ref[pl.ds(j, 1), :] = h_vmem[pl.ds(tok_ref[nxt + j], 1), :]

        x = _unpack_bf16_pairs(x_ref[...])
        a = jnp.dot(x, wg_bf[...], preferred_element_type=F32)
        b = jnp.dot(x, wu_bf[...], preferred_element_type=F32)
        mid = (a * jax.nn.sigmoid(a) * b).astype(BF16)
        y = _pack_bf16_pairs(jnp.dot(mid, wd_bf[...], preferred_element_type=F32))
        pieces = y.shape[1] // LANES
        for c in range(SLAB):
            piece = y[:, c * LANES:(c + 1) * LANES] if c < pieces else jnp.zeros((bm, LANES), y.dtype)
            y_ref[pl.ds(c, bm, stride=SLAB), :] = piece

    pl.when(used & (cur == 0))(lambda: block(xs0, xs1))
    pl.when(used & (cur == 1))(lambda: block(xs1, xs0))

    @pl.when(jnp.logical_not(used))
    def _():
        y_ref[...] = jnp.zeros_like(y_ref)


def _experts(block_expert, n_used, sorted_tok, next_expert, h2_packed, w_gate, w_up, w_down, layer):
    n_tok, width = h2_packed.shape
    bm = EXPERT_BLOCK
    any_space = pl.BlockSpec(memory_space=pl.ANY)
    grid_spec = pltpu.PrefetchScalarGridSpec(
        num_scalar_prefetch=4,
        grid=(sorted_tok.shape[0] // bm,),
        in_specs=[any_space, any_space, any_space, any_space],
        out_specs=pl.BlockSpec((bm * SLAB, LANES), lambda i, *_: (i, 0)),
        scratch_shapes=[
            pltpu.VMEM((n_tok + ZERO_ROWS, width), h2_packed.dtype),
            pltpu.VMEM((bm, width), h2_packed.dtype),
            pltpu.VMEM((bm, width), h2_packed.dtype),
            pltpu.VMEM((D_MODEL, EXPERT_HIDDEN), F32),
            pltpu.VMEM((D_MODEL, EXPERT_HIDDEN), F32),
            pltpu.VMEM((EXPERT_HIDDEN, D_MODEL), F32),
            pltpu.VMEM((D_MODEL, EXPERT_HIDDEN), BF16),
            pltpu.VMEM((D_MODEL, EXPERT_HIDDEN), BF16),
            pltpu.VMEM((EXPERT_HIDDEN, D_MODEL), BF16),
            pltpu.SemaphoreType.DMA(()),
            pltpu.SemaphoreType.DMA((3,)),
        ],
    )
    return pl.pallas_call(
        functools.partial(_experts_kernel, layer),
        grid_spec=grid_spec,
        out_shape=jax.ShapeDtypeStruct((sorted_tok.shape[0] * SLAB, LANES), h2_packed.dtype),
        compiler_params=pltpu.CompilerParams(
            dimension_semantics=("arbitrary",), vmem_limit_bytes=EXPERTS_VMEM_LIMIT),
        name="experts",
    )(block_expert, n_used, sorted_tok, next_expert, h2_packed, w_gate, w_up, w_down)


def _sorted_tokens_kernel(dest_ref, seg_end_ref, tok_ref):
    n_tok = dest_ref.shape[0]
    bm = EXPERT_BLOCK
    buf_len = tok_ref.shape[0]

    def fill_from(start):
        def fill(p, carry):
            tok_ref[start + p] = n_tok
            return carry
        return fill

    for e in range(N_EXPERTS):
        lax.fori_loop(0, bm, fill_from(jnp.maximum(seg_end_ref[e] - bm, 0)), 0, unroll=16)
    lax.fori_loop(0, N_EXPERTS * bm, fill_from(buf_len - N_EXPERTS * bm), 0, unroll=16)

    def place(t, carry):
        both = dest_ref[t]
        tok_ref[both & 0xFFFF] = t
        tok_ref[lax.shift_right_logical(both, 16)] = t
        return carry
    lax.fori_loop(0, n_tok, place, 0, unroll=8)


def _sorted_tokens(dest, seg_end, buf_len):
    assert TOP_K == 2 and buf_len < (1 << 16)
    n_tok = dest.shape[0] // TOP_K
    packed = dest[:n_tok] | (dest[n_tok:] << 16)
    smem = pl.BlockSpec(memory_space=pltpu.SMEM)
    return pl.pallas_call(
        _sorted_tokens_kernel,
        in_specs=[smem, smem],
        out_specs=smem,
        out_shape=jax.ShapeDtypeStruct((buf_len,), jnp.int32),
        name="moe_sorted_tokens",
    )(packed, seg_end)


def _combine_kernel(dest_ref, x_ref, gate_ref, y_ref, out_ref, ya0, ya1, yb0, yb1, sems):
    tm = x_ref.shape[0]
    i = pl.program_id(0)
    n_steps = pl.num_programs(0)
    n_tok = n_steps * tm
    half = x_ref.shape[1] // 2
    pieces = half // LANES

    def row_copy(step, bufs, sem, r, k):
        d = pl.multiple_of(dest_ref[k * n_tok + step * tm + r], SLAB)
        dst_row = r * SLAB if isinstance(r, int) else pl.multiple_of(r * SLAB, SLAB)
        return pltpu.make_async_copy(y_ref.at[pl.ds(d, pieces)],
                                     bufs[k].at[pl.ds(dst_row, pieces)], sem)

    def drain(step, bufs, sem):
        def wait(r, carry):
            for k in range(TOP_K):
                row_copy(step, bufs, sem, r, k).wait()
            return carry
        lax.fori_loop(0, tm, wait, 0, unroll=8)

    @pl.when(i == 0)
    def _():
        def issue(r, carry):
            for k in range(TOP_K):
                row_copy(0, (ya0, ya1), sems.at[0], r, k).start()
            return carry
        lax.fori_loop(0, tm, issue, 0, unroll=8)

    def tile(bufs, sem, next_bufs, next_sem):
        @pl.when(i + 1 < n_steps)
        def _():
            for r in range(tm):
                for k in range(TOP_K):
                    row_copy(i + 1, next_bufs, next_sem, r, k).start(priority=(r + k) % 2)

        drain(i, bufs, sem)
        g = gate_ref[...]
        for c in range(pieces):
            w = [b[pl.ds(c, tm, stride=SLAB), :] for b in bufs]
            for base, unpack in ((0, lambda v: lax.shift_left(v, jnp.uint32(16))),
                                 (half, lambda v: v & jnp.uint32(0xFFFF0000))):
                cols = slice(base + c * LANES, base + (c + 1) * LANES)
                out_ref[:, cols] = (x_ref[:, cols]
                                    + g[:, 0:1] * pltpu.bitcast(unpack(w[0]), F32)
                                    + g[:, 1:2] * pltpu.bitcast(unpack(w[1]), F32))

    parity = i & 1
    pl.when(parity == 0)(lambda: tile((ya0, ya1), sems.at[0], (yb0, yb1), sems.at[1]))
    pl.when(parity == 1)(lambda: tile((yb0, yb1), sems.at[1], (ya0, ya1), sems.at[0]))


def _combine(dest_slab, x_new, gates, y_slabs):
    n, d = x_new.shape
    tm = ROW_TILE
    grid_spec = pltpu.PrefetchScalarGridSpec(
        num_scalar_prefetch=1,
        grid=(n // tm,),
        in_specs=[
            pl.BlockSpec((tm, d), lambda i, dest: (i, 0)),
            pl.BlockSpec((tm, LANES), lambda i, dest: (i, 0)),
            pl.BlockSpec(memory_space=pl.ANY),
        ],
        out_specs=pl.BlockSpec((tm, d), lambda i, dest: (i, 0)),
        scratch_shapes=[pltpu.VMEM((tm * SLAB, LANES), y_slabs.dtype) for _ in range(2 * TOP_K)]
        + [pltpu.SemaphoreType.DMA((2,))],
    )
    return pl.pallas_call(
        _combine_kernel,
        grid_spec=grid_spec,
        out_shape=jax.ShapeDtypeStruct((n, d), F32),
        compiler_params=_params(1),
        name="moe_combine",
    )(dest_slab, x_new, gates, y_slabs)


def _dispatch_plan(expert_id, slot, counts):
    bm = EXPERT_BLOCK
    n_assign = expert_id.size
    padded = (counts + bm - 1) // bm * bm
    pend = jnp.cumsum(padded)
    pstart = pend - padded
    experts = jnp.arange(N_EXPERTS, dtype=jnp.int32)
    seg_start = jnp.sum(jnp.where(expert_id[..., None] == experts, pstart, 0), axis=-1)
    dest = (seg_start + slot).reshape(-1).astype(jnp.int32)
    buf_len = n_assign + N_EXPERTS * bm
    block_start = jnp.arange(buf_len // bm, dtype=jnp.int32) * bm
    n_used = (pend[-1:] // bm).astype(jnp.int32)
    block_start = jnp.minimum(block_start, pend[-1] - bm)
    block_expert = jnp.sum((pend[None, :] <= block_start[:, None]).astype(jnp.int32), axis=1)
    block_expert = jnp.minimum(block_expert, N_EXPERTS - 1)
    later = jnp.where(block_expert[None, :] > block_expert[:, None], block_expert[None, :], N_EXPERTS)
    next_expert = jnp.min(later, axis=1)
    next_expert = jnp.where(next_expert == N_EXPERTS, block_expert, next_expert).astype(jnp.int32)
    return dest, block_expert, next_expert, n_used, pend.astype(jnp.int32), buf_len


def kernel(x, norm1_g, w_in, q_norm_g, k_norm_g, w_attn_out, w_pool_mix, pool_scale, w_pool_out,
           w_o, norm2_g, w_router_group, b_router_group, w_router_expert, b_router_expert,
           w_exp_gate, w_exp_up, w_exp_down):
    batch, seq, d = x.shape
    assert (seq, d) == (SEQ, D_MODEL)
    n_tok = batch * seq
    depth = w_in.shape[0]
    half = ATTN_WIDTH // 2
    hsum = (jnp.arange(half)[:, None] // HEAD_DIM == jnp.arange(half)[None, :] // HEAD_DIM).astype(BF16)

    xf = x.reshape(n_tok, d)
    for layer in range(depth):
        qg = jnp.tile(q_norm_g[layer], N_HEADS)[None, :]
        kg = jnp.tile(k_norm_g[layer], N_HEADS)[None, :]
        q, k, v, u, gates = _inproj(xf, norm1_g[layer][None, :], w_in[layer].astype(BF16), qg, kg, hsum)
        attn = _attn(q.reshape(batch, seq, ATTN_WIDTH), k.reshape(batch, seq, ATTN_WIDTH),
                     v.reshape(batch, seq, ATTN_WIDTH), batch).reshape(n_tok, ATTN_WIDTH)

        w_router, b_router = _router_operands(
            w_router_group[layer], b_router_group[layer], w_router_expert[layer], b_router_expert[layer])
        x_new, h2, route_i, route_f, counts = _post(
            xf, attn, u, gates, w_attn_out[layer].astype(BF16), w_pool_mix[layer].astype(BF16),
            pool_scale[layer][None, :], w_pool_out[layer].astype(BF16), w_o[layer].astype(BF16),
            norm2_g[layer][None, :], w_router, b_router)

        dest, block_expert, next_expert, n_used, seg_end, buf_len = _dispatch_plan(
            route_i[:TOP_K], route_i[TOP_K:2 * TOP_K], counts[:, 0].astype(jnp.int32))
        sorted_tok = _sorted_tokens(dest, seg_end, buf_len)
        y_rows = _experts(block_expert, n_used, sorted_tok, next_expert, h2,
                          w_exp_gate, w_exp_up, w_exp_down, layer)
        gate_rows = jnp.pad(route_f[:TOP_K].T, ((0, 0), (0, LANES - TOP_K)))
        xf = _combine(dest * SLAB, x_new, gate_rows, y_rows)
    return xf.reshape(batch, seq, d)
```

```python
import functools

import jax
import jax.numpy as jnp
import numpy as np
from jax import lax
from jax.experimental import pallas as pl
from jax.experimental.pallas import tpu as pltpu

D_MODEL = 1024
SEQ = 4096
N_HEADS = 8
HEAD_DIM = 64
ATTN_WIDTH = N_HEADS * HEAD_DIM
DILATIONS = (1, 4, 16)
ATTN_BLOCK = 128
POOL_WINDOWS = (2, 4, 8, 16)
POOL_GROUP_DIM = 128
POOL_WIDTH = len(POOL_WINDOWS) * POOL_GROUP_DIM
POOL_HALO = 16
IN_PROJ_WIDTH = 3 * ATTN_WIDTH + POOL_WIDTH + 2 * D_MODEL
N_EXPERT_GROUPS = 4
EXPERTS_PER_GROUP = 8
N_EXPERTS = N_EXPERT_GROUPS * EXPERTS_PER_GROUP
TOP_K = 2
EXPERT_HIDDEN = 512
RMS_EPS = 1e-6
NEG_INF = -1e30

LANES = 128
ROW_TILE = 256
INPROJ_TILE = 512
POST_TILE = 512
EXPERT_BLOCK = 256
VMEM_LIMIT = 48 * 1024 * 1024
ATTN_VMEM_LIMIT = 56 * 1024 * 1024
EXPERTS_VMEM_LIMIT = 56 * 1024 * 1024

F32 = jnp.float32
BF16 = jnp.bfloat16


def _params(n_axes):
    return pltpu.CompilerParams(
        dimension_semantics=("arbitrary",) * n_axes, vmem_limit_bytes=VMEM_LIMIT)


def _inproj_kernel(x_ref, g_ref, w_ref, qg_ref, kg_ref, hsum_ref,
                   q_ref, k_ref, v_ref, u_ref, gate_ref):
    x = x_ref[...]
    ms = jnp.mean(x * x, axis=-1, keepdims=True)
    h = (x * lax.rsqrt(ms + RMS_EPS) * g_ref[...]).astype(BF16)

    def proj(lo, hi):
        return jnp.dot(h, w_ref[:, lo:hi], preferred_element_type=F32)

    def head_norm(t, gain):
        sq = (t * t).astype(BF16)
        half = ATTN_WIDTH // 2
        ssq = jnp.concatenate(
            [jnp.dot(sq[:, j * half:(j + 1) * half], hsum_ref[...], preferred_element_type=F32)
             for j in range(2)], axis=-1)
        return t * lax.rsqrt(ssq * (1.0 / HEAD_DIM) + RMS_EPS) * gain

    w = ATTN_WIDTH
    q_ref[...] = head_norm(proj(0, w), qg_ref[...])
    k_ref[...] = head_norm(proj(w, 2 * w), kg_ref[...])
    v_ref[...] = proj(2 * w, 3 * w)
    u_ref[...] = proj(3 * w, 3 * w + POOL_WIDTH)
    base = 3 * w + POOL_WIDTH
    for j in range(2 * D_MODEL // 512):
        gate_ref[:, j * 512:(j + 1) * 512] = jax.nn.sigmoid(
            proj(base + j * 512, base + (j + 1) * 512)).astype(BF16)


def _inproj(x, g1, w_in, qg, kg, hsum):
    n = x.shape[0]
    ROW_TILE = INPROJ_TILE
    row = lambda i: (i, 0)
    const = lambda i: (0, 0)
    return pl.pallas_call(
        _inproj_kernel,
        grid=(n // ROW_TILE,),
        in_specs=[
            pl.BlockSpec((ROW_TILE, D_MODEL), row),
            pl.BlockSpec((1, D_MODEL), const),
            pl.BlockSpec((D_MODEL, IN_PROJ_WIDTH), const),
            pl.BlockSpec((1, ATTN_WIDTH), const),
            pl.BlockSpec((1, ATTN_WIDTH), const),
            pl.BlockSpec((ATTN_WIDTH // 2, ATTN_WIDTH // 2), const),
        ],
        out_specs=[
            pl.BlockSpec((ROW_TILE, ATTN_WIDTH), row),
            pl.BlockSpec((ROW_TILE, ATTN_WIDTH), row),
            pl.BlockSpec((ROW_TILE, ATTN_WIDTH), row),
            pl.BlockSpec((ROW_TILE, POOL_WIDTH), row),
            pl.BlockSpec((ROW_TILE, 2 * D_MODEL), row),
        ],
        out_shape=[
            jax.ShapeDtypeStruct((n, ATTN_WIDTH), F32),
            jax.ShapeDtypeStruct((n, ATTN_WIDTH), F32),
            jax.ShapeDtypeStruct((n, ATTN_WIDTH), F32),
            jax.ShapeDtypeStruct((n, POOL_WIDTH), F32),
            jax.ShapeDtypeStruct((n, 2 * D_MODEL), BF16),
        ],
        compiler_params=_params(1),
        name="inproj",
    )(x, g1, w_in, qg, kg, hsum)


ATTN_UNROLL = 32
QUAD = 4
QUAD_ROWS = SEQ // QUAD
LOG2E = 1.4426950408889634


def _attn_bias():
    blk = ATTN_BLOCK
    r = np.arange(2 * blk) % blk
    c = np.arange(2 * blk)
    per_q, per_k = blk // QUAD, 2 * blk // QUAD
    tq = QUAD * (r % per_q) + r // per_q
    tk = QUAD * (c % per_k) + c // per_k
    masks = []
    for tq_, tk_ in ((tq, tk), (r, c)):
        first = tq_[:, None] - tk_[None, :]
        later = first + blk
        masks += [first >= 0, (later >= 0) & (later <= blk)]
    return np.where(np.stack(masks), 0.0, NEG_INF).astype(np.float32)


def _attn_kernel(q_ref, k_ref, v_ref, bias_ref, out_ref, q4, k4, v4,
                 o0, o1, o2, m0, m1, m2, d0, d1, d2):
    blk = ATTN_BLOCK
    head_a = lax.broadcasted_iota(jnp.int32, (blk, LANES), 1) < HEAD_DIM
    ones_cols = jnp.ones((2 * blk, LANES), BF16)

    for c in range(QUAD):
        dst = pl.ds(c * QUAD_ROWS, QUAD_ROWS)
        src = pl.ds(c, QUAD_ROWS, stride=QUAD)
        q4[dst, :] = q_ref[src, :] * (HEAD_DIM ** -0.5 * LOG2E)
        k4[dst, :] = k_ref[src, :]
        v4[dst, :] = v_ref[src, :]

    def attend(q2, k2, v2, bias):
        qs = jnp.concatenate(
            [jnp.where(head_a, q2, 0.0), jnp.where(head_a, 0.0, q2)], axis=0).astype(BF16)
        s = lax.dot_general(qs, k2.astype(BF16), (((1,), (1,)), ((), ())),
                            preferred_element_type=F32) + bias
        m = jnp.max(s, axis=-1, keepdims=True)
        e = jnp.exp2(s - m).astype(BF16)
        r = jnp.dot(e, jnp.concatenate([v2.astype(BF16), ones_cols], axis=1),
                    preferred_element_type=F32)
        mb = jnp.broadcast_to(m, (2 * blk, LANES))
        return (jnp.where(head_a, r[:blk, :LANES], r[blk:, :LANES]),
                jnp.where(head_a, mb[:blk], mb[blk:]),
                jnp.where(head_a, r[:blk, LANES:], r[blk:, LANES:]))

    def load(ref, pieces):
        return jnp.concatenate([ref[p, :] for p in pieces], axis=0)

    def store(refs, pieces, vals):
        for ref, val in zip(refs, vals):
            row = 0
            for p in pieces:
                ref[p, :] = val[row:row + p.size]
                row += p.size

    def body1(nb, carry):
        kb = jnp.maximum(nb - 1, 0)
        per_q, per_k = blk // QUAD, 2 * blk // QUAD
        qp = [pl.ds(pl.multiple_of(c * QUAD_ROWS + nb * per_q, per_q), per_q) for c in range(QUAD)]
        kp = [pl.ds(pl.multiple_of(c * QUAD_ROWS + kb * per_q, per_q), per_k) for c in range(QUAD)]
        res = attend(load(q4, qp), load(k4, kp), load(v4, kp), bias_ref[jnp.minimum(nb, 1)])
        store((o0, m0, d0), qp, res)
        return carry

    lax.fori_loop(0, SEQ // blk, body1, 0, unroll=ATTN_UNROLL)

    n_blk4 = QUAD_ROWS // blk

    def body4(idx, carry):
        nb = idx & (n_blk4 - 1)
        base = (idx - nb) * blk
        kb = jnp.maximum(nb - 1, 0)
        qp = [pl.ds(pl.multiple_of(base + nb * blk, blk), blk)]
        kp = [pl.ds(pl.multiple_of(base + kb * blk, blk), 2 * blk)]
        res = attend(load(q4, qp), load(k4, kp), load(v4, kp), bias_ref[2 + jnp.minimum(nb, 1)])
        store((o1, m1, d1), qp, res)
        return carry

    lax.fori_loop(0, SEQ // blk, body4, 0, unroll=ATTN_UNROLL)

    def body16(idx, carry):
        start = (idx & (QUAD - 1)) * QUAD_ROWS + lax.shift_right_logical(idx, 2)
        kp = [pl.ds(start, 2 * blk, stride=QUAD)]
        k2, v2 = load(k4, kp), load(v4, kp)
        for nb in range(2):
            qp = [pl.ds(start + nb * blk * QUAD, blk, stride=QUAD)]
            store((o2, m2, d2), qp, attend(load(q4, qp), k2, v2, bias_ref[2 + nb]))
        return carry

    lax.fori_loop(0, SEQ // (2 * blk), body16, 0, unroll=ATTN_UNROLL // 2)

    chunk = 512

    def mix(i, carry):
        r = pl.ds(pl.multiple_of(i * chunk, chunk), chunk)
        ma, mb, mc = m0[r, :], m1[r, :], m2[r, :]
        m = jnp.maximum(jnp.maximum(ma, mb), mc)
        wa, wb, wc = jnp.exp2(ma - m), jnp.exp2(mb - m), jnp.exp2(mc - m)
        acc = wa * o0[r, :] + wb * o1[r, :] + wc * o2[r, :]
        den = wa * d0[r, :] + wb * d1[r, :] + wc * d2[r, :]
        per_class = QUAD_ROWS // chunk
        c = i // per_class
        n0 = (i - c * per_class) * chunk
        out_ref[pl.ds(QUAD * n0 + c, chunk, stride=QUAD), :] = acc / den
        return carry

    lax.fori_loop(0, SEQ // chunk, mix, 0)


def _attn(q, k, v, batch):
    spec = pl.BlockSpec((None, SEQ, LANES), lambda b, hp: (b, 0, hp))
    bias = _attn_bias()
    return pl.pallas_call(
        _attn_kernel,
        grid=(batch, ATTN_WIDTH // LANES),
        in_specs=[spec, spec, spec, pl.BlockSpec(bias.shape, lambda b, hp: (0, 0, 0))],
        out_specs=spec,
        out_shape=jax.ShapeDtypeStruct((batch, SEQ, ATTN_WIDTH), F32),
        scratch_shapes=[pltpu.VMEM((SEQ, LANES), F32) for _ in range(12)],
        compiler_params=pltpu.CompilerParams(
            dimension_semantics=("arbitrary", "arbitrary"), vmem_limit_bytes=ATTN_VMEM_LIMIT),
        name="dilated_attn",
    )(q, k, v, jnp.asarray(bias))


def _pack_bf16_pairs(x):
    c = x.shape[1] // 2
    bits = pltpu.bitcast(x.astype(BF16).astype(F32), jnp.uint32)
    return lax.shift_right_logical(bits[:, :c], jnp.uint32(16)) | bits[:, c:]


def _unpack_bf16_pairs(w):
    lo = pltpu.bitcast(lax.shift_left(w, jnp.uint32(16)), F32)
    hi = pltpu.bitcast(w & jnp.uint32(0xFFFF0000), F32)
    return jnp.concatenate([lo, hi], axis=1).astype(BF16)


def _post_kernel(x_ref, attn_ref, u_ref, halo_ref, gate_ref, wao_ref, wmix_ref, pscale_ref,
                 wpo_ref, wo_ref, g2_ref, wr_ref, br_ref,
                 xo_ref, h2_ref, ri_ref, rf_ref, cnt_ref):
    tm = x_ref.shape[0]

    @pl.when(pl.program_id(0) == 0)
    def _():
        cnt_ref[...] = jnp.zeros_like(cnt_ref)

    pos0 = lax.rem(pl.program_id(0) * tm, SEQ)
    pos = pos0 + lax.broadcasted_iota(jnp.int32, (tm, 1), 0)
    u = u_ref[...]
    halo = halo_ref[...] * (pos0 > 0).astype(F32)

    mixed = []
    for gi, w in enumerate(POOL_WINDOWS):
        lo = gi * POOL_GROUP_DIM
        ug = u[:, lo:lo + POOL_GROUP_DIM]
        ext = jnp.concatenate([halo[:, lo:lo + POOL_GROUP_DIM], ug], axis=0)
        shift = 1
        while shift < w:
            ext = ext + pltpu.roll(ext, shift, 0)
            shift *= 2
        cnt = jnp.minimum(pos + 1, w).astype(F32)
        pooled = ext[POOL_HALO:] / cnt - ug
        mixed.append(jnp.dot(pooled.astype(BF16), wmix_ref[gi], preferred_element_type=F32))
    pool_out = (jnp.concatenate(mixed, axis=-1) * pscale_ref[...]).astype(BF16)

    y_a = jnp.dot(attn_ref[...].astype(BF16), wao_ref[...], preferred_element_type=F32)
    y_p = jnp.dot(pool_out, wpo_ref[...], preferred_element_type=F32)
    gates = gate_ref[...]
    merged = gates[:, :D_MODEL].astype(F32) * y_a + gates[:, D_MODEL:].astype(F32) * y_p
    x_new = x_ref[...] + jnp.dot(merged.astype(BF16), wo_ref[...], preferred_element_type=F32)
    xo_ref[...] = x_new

    ms = jnp.mean(x_new * x_new, axis=-1, keepdims=True)
    h2 = x_new * lax.rsqrt(ms + RMS_EPS) * g2_ref[...]
    h_hi = h2.astype(BF16)
    h2_ref[...] = _pack_bf16_pairs(h2)

    h_lo = (h2 - h_hi.astype(F32)).astype(BF16)
    nt = (((1,), (1,)), ((), ()))
    both = lax.dot_general(wr_ref[...], h_hi, nt, preferred_element_type=F32)
    cross = lax.dot_general(wr_ref[:ROUTER_ROWS, :], h_lo, nt, preferred_element_type=F32)
    logits = both[:ROUTER_ROWS] + both[ROUTER_ROWS:] + cross + jnp.concatenate(
        [br_ref[...]] * (tm // LANES), axis=1)

    sub = lax.broadcasted_iota(jnp.int32, (8, tm), 0)
    npg = EXPERTS_PER_GROUP

    def first_max(vals):
        vmax = jnp.max(vals, axis=0, keepdims=True)
        return vmax, jnp.min(jnp.where(vals == vmax, sub, npg), axis=0, keepdims=True)

    def of_group(parts, g_sel):
        out = parts[-1]
        for g in range(N_EXPERT_GROUPS - 2, -1, -1):
            out = jnp.where(g_sel == g, parts[g], out)
        return out

    glog = jnp.where(sub < N_EXPERT_GROUPS, logits[0:8], -jnp.inf)
    gmax, g_sel = first_max(glog)
    g_gate = 1.0 / jnp.sum(jnp.exp(glog - gmax), axis=0, keepdims=True)

    elog = of_group([logits[npg * (g + 1):npg * (g + 2)] for g in range(N_EXPERT_GROUPS)], g_sel)
    e_exp = jnp.exp(elog - jnp.max(elog, axis=0, keepdims=True))
    prob = e_exp / jnp.sum(e_exp, axis=0, keepdims=True)
    p1, i1 = first_max(prob)
    p2, i2 = first_max(jnp.where(sub == i1, -1.0, prob))
    scale = g_gate / (p1 + p2)

    chosen = (sub == i1) | (sub == i2)
    onehot = jnp.concatenate(
        [jnp.where(chosen & (g_sel == g), 1.0, 0.0) for g in range(N_EXPERT_GROUPS)],
        axis=0).astype(BF16)
    t_row = lax.broadcasted_iota(jnp.int32, (tm, tm), 0)
    t_col = lax.broadcasted_iota(jnp.int32, (tm, tm), 1)
    before = jnp.dot(onehot, jnp.where(t_row < t_col, 1.0, 0.0).astype(BF16),
                     preferred_element_type=F32)
    total = jnp.dot(onehot, jnp.ones((tm, LANES), BF16), preferred_element_type=F32)
    seen = cnt_ref[...]
    slot = before + jnp.concatenate([seen] * (tm // LANES), axis=1)
    cnt_ref[...] = seen + total
    slot = of_group([slot[npg * g:npg * (g + 1)] for g in range(N_EXPERT_GROUPS)], g_sel)
    r1 = jnp.sum(jnp.where(sub == i1, slot, 0.0), axis=0, keepdims=True).astype(jnp.int32)
    r2 = jnp.sum(jnp.where(sub == i2, slot, 0.0), axis=0, keepdims=True).astype(jnp.int32)

    e_base = g_sel * npg
    ri_ref[...] = jnp.where(sub == 0, e_base + i1,
                            jnp.where(sub == 1, e_base + i2,
                                      jnp.where(sub == 2, r1, jnp.where(sub == 3, r2, 0))))
    rf_ref[...] = jnp.where(sub == 0, p1 * scale, jnp.where(sub == 1, p2 * scale, 0.0))


ROUTER_ROWS = 8 + N_EXPERTS + 8


def _router_operands(w_group, b_group, w_expert, b_expert):
    d = w_group.shape[0]
    wt = jnp.zeros((ROUTER_ROWS, d), F32)
    wt = wt.at[:N_EXPERT_GROUPS].set(w_group.T)
    wt = wt.at[8:8 + N_EXPERTS].set(w_expert.transpose(0, 2, 1).reshape(N_EXPERTS, d))
    hi = wt.astype(BF16)
    lo = (wt - hi.astype(F32)).astype(BF16)
    b = jnp.zeros((ROUTER_ROWS,), F32).at[:N_EXPERT_GROUPS].set(b_group)
    b = b.at[8:8 + N_EXPERTS].set(b_expert.reshape(-1))
    return jnp.concatenate([hi, lo], axis=0), jnp.broadcast_to(b[:, None], (ROUTER_ROWS, LANES))


def _post(x, attn, u, gates, wao, wmix, pscale, wpo, wo, g2, wr, br):
    n = x.shape[0]
    tm = POST_TILE
    row = lambda i: (i, 0)
    const = lambda i: (0, 0)
    halo_blocks = tm // POOL_HALO
    return pl.pallas_call(
        _post_kernel,
        grid=(n // tm,),
        in_specs=[
            pl.BlockSpec((tm, D_MODEL), row),
            pl.BlockSpec((tm, ATTN_WIDTH), row),
            pl.BlockSpec((tm, POOL_WIDTH), row),
            pl.BlockSpec((POOL_HALO, POOL_WIDTH),
                         lambda i: (jnp.maximum(i * halo_blocks - 1, 0), 0)),
            pl.BlockSpec((tm, 2 * D_MODEL), row),
            pl.BlockSpec((ATTN_WIDTH, D_MODEL), const),
            pl.BlockSpec((len(POOL_WINDOWS), POOL_GROUP_DIM, POOL_GROUP_DIM), lambda i: (0, 0, 0)),
            pl.BlockSpec((1, POOL_WIDTH), const),
            pl.BlockSpec((POOL_WIDTH, D_MODEL), const),
            pl.BlockSpec((D_MODEL, D_MODEL), const),
            pl.BlockSpec((1, D_MODEL), const),
            pl.BlockSpec((2 * ROUTER_ROWS, D_MODEL), const),
            pl.BlockSpec((ROUTER_ROWS, LANES), const),
        ],
        out_specs=[
            pl.BlockSpec((tm, D_MODEL), row),
            pl.BlockSpec((tm, D_MODEL // 2), row),
            pl.BlockSpec((8, tm), lambda i: (0, i)),
            pl.BlockSpec((8, tm), lambda i: (0, i)),
            pl.BlockSpec((N_EXPERTS, LANES), const),
        ],
        out_shape=[
            jax.ShapeDtypeStruct((n, D_MODEL), F32),
            jax.ShapeDtypeStruct((n, D_MODEL // 2), jnp.uint32),
            jax.ShapeDtypeStruct((8, n), jnp.int32),
            jax.ShapeDtypeStruct((8, n), F32),
            jax.ShapeDtypeStruct((N_EXPERTS, LANES), F32),
        ],
        compiler_params=_params(1),
        name="post_attn_router",
    )(x, attn, u, u, gates, wao, wmix, pscale, wpo, wo, g2, wr, br)


ZERO_ROWS = 8
SLAB = D_MODEL // 2 // LANES


def _experts_kernel(layer, be_ref, nused_ref, tok_ref, next_e_ref, h_hbm, wg_hbm, wu_hbm, wd_hbm,
                    y_ref, h_vmem, xs0, xs1, wg_f32, wu_f32, wd_f32, wg_bf, wu_bf, wd_bf,
                    h_sem, w_sems):
    i = pl.program_id(0)
    bm = xs0.shape[0]
    n_tok = h_hbm.shape[0]
    changed = (i == 0) | (be_ref[i] != be_ref[jnp.maximum(i - 1, 0)])
    used = i < nused_ref[0]
    cur = i & 1

    def weight_copies(e):
        return [pltpu.make_async_copy(src.at[layer, e], dst, w_sems.at[n])
                for n, (src, dst) in enumerate(
                    ((wg_hbm, wg_f32), (wu_hbm, wu_f32), (wd_hbm, wd_f32)))]

    @pl.when(i == 0)
    def _():
        @pl.when(used)
        def _():
            for c in weight_copies(be_ref[0]):
                c.start(priority=1)
        copy = pltpu.make_async_copy(h_hbm, h_vmem.at[pl.ds(0, n_tok)], h_sem)
        copy.start()
        h_vmem[pl.ds(n_tok, ZERO_ROWS), :] = jnp.zeros((ZERO_ROWS, h_vmem.shape[1]), h_vmem.dtype)
        copy.wait()

        def gather(j, carry):
            xs0[pl.ds(j, 1), :] = h_vmem[pl.ds(tok_ref[j], 1), :]
            return carry
        lax.fori_loop(0, bm, gather, 0, unroll=8)

    @pl.when(changed & used)
    def _():
        for c in weight_copies(be_ref[i]):
            c.wait()
        wg_bf[...] = wg_f32[...].astype(BF16)
        wu_bf[...] = wu_f32[...].astype(BF16)
        wd_bf[...] = wd_f32[...].astype(BF16)

        @pl.when(next_e_ref[i] != be_ref[i])
        def _():
            for n, c in enumerate(weight_copies(next_e_ref[i])):
                c.start(priority=n % 2)

    def block(x_ref, x_next_ref):
        nxt = jnp.where(i + 1 < nused_ref[0], i + 1, i) * bm
        for j in range(bm):
            x_next_ref[pl.ds(j, 1), :] = h_vmem[pl.ds(tok_ref[nxt + j], 1), :]

        x = _unpack_bf16_pairs(x_ref[...])
        a = jnp.dot(x, wg_bf[...], preferred_element_type=F32)
        b = jnp.dot(x, wu_bf[...], preferred_element_type=F32)
        mid = (a * jax.nn.sigmoid(a) * b).astype(BF16)
        y = _pack_bf16_pairs(jnp.dot(mid, wd_bf[...], preferred_element_type=F32))
        for c in range(SLAB):
            y_ref[pl.ds(c, bm, stride=SLAB), :] = y[:, c * LANES:(c + 1) * LANES]

    pl.when(used & (cur == 0))(lambda: block(xs0, xs1))
    pl.when(used & (cur == 1))(lambda: block(xs1, xs0))

    @pl.when(jnp.logical_not(used))
    def _():
        y_ref[...] = jnp.zeros_like(y_ref)


def _experts(block_expert, n_used, sorted_tok, next_expert, h2_packed, w_gate, w_up, w_down, layer):
    n_tok, width = h2_packed.shape
    bm = EXPERT_BLOCK
    any_space = pl.BlockSpec(memory_space=pl.ANY)
    grid_spec = pltpu.PrefetchScalarGridSpec(
        num_scalar_prefetch=4,
        grid=(sorted_tok.shape[0] // bm,),
        in_specs=[any_space, any_space, any_space, any_space],
        out_specs=pl.BlockSpec((bm * SLAB, LANES), lambda i, *_: (i, 0)),
        scratch_shapes=[
            pltpu.VMEM((n_tok + ZERO_ROWS, width), h2_packed.dtype),
            pltpu.VMEM((bm, width), h2_packed.dtype),
            pltpu.VMEM((bm, width), h2_packed.dtype),
            pltpu.VMEM((D_MODEL, EXPERT_HIDDEN), F32),
            pltpu.VMEM((D_MODEL, EXPERT_HIDDEN), F32),
            pltpu.VMEM((EXPERT_HIDDEN, D_MODEL), F32),
            pltpu.VMEM((D_MODEL, EXPERT_HIDDEN), BF16),
            pltpu.VMEM((D_MODEL, EXPERT_HIDDEN), BF16),
            pltpu.VMEM((EXPERT_HIDDEN, D_MODEL), BF16),
            pltpu.SemaphoreType.DMA(()),
            pltpu.SemaphoreType.DMA((3,)),
        ],
    )
    return pl.pallas_call(
        functools.partial(_experts_kernel, layer),
        grid_spec=grid_spec,
        out_shape=jax.ShapeDtypeStruct((sorted_tok.shape[0] * SLAB, LANES), h2_packed.dtype),
        compiler_params=pltpu.CompilerParams(
            dimension_semantics=("arbitrary",), vmem_limit_bytes=EXPERTS_VMEM_LIMIT),
        name="experts",
    )(block_expert, n_used, sorted_tok, next_expert, h2_packed, w_gate, w_up, w_down)


def _sorted_tokens_kernel(dest_ref, seg_end_ref, tok_ref):
    n_tok = dest_ref.shape[0]
    bm = EXPERT_BLOCK
    buf_len = tok_ref.shape[0]

    def fill_from(start):
        def fill(p, carry):
            tok_ref[start + p] = n_tok
            return carry
        return fill

    for e in range(N_EXPERTS):
        lax.fori_loop(0, bm, fill_from(jnp.maximum(seg_end_ref[e] - bm, 0)), 0, unroll=16)
    lax.fori_loop(0, N_EXPERTS * bm, fill_from(buf_len - N_EXPERTS * bm), 0, unroll=16)

    def place(t, carry):
        both = dest_ref[t]
        tok_ref[both & 0xFFFF] = t
        tok_ref[lax.shift_right_logical(both, 16)] = t
        return carry
    lax.fori_loop(0, n_tok, place, 0, unroll=8)


def _sorted_tokens(dest, seg_end, buf_len):
    assert TOP_K == 2 and buf_len < (1 << 16)
    n_tok = dest.shape[0] // TOP_K
    packed = dest[:n_tok] | (dest[n_tok:] << 16)
    smem = pl.BlockSpec(memory_space=pltpu.SMEM)
    return pl.pallas_call(
        _sorted_tokens_kernel,
        in_specs=[smem, smem],
        out_specs=smem,
        out_shape=jax.ShapeDtypeStruct((buf_len,), jnp.int32),
        name="moe_sorted_tokens",
    )(packed, seg_end)


def _combine_kernel(dest_ref, x_ref, gate_ref, y_ref, out_ref, ya0, ya1, yb0, yb1, sems):
    tm = x_ref.shape[0]
    i = pl.program_id(0)
    n_steps = pl.num_programs(0)
    n_tok = n_steps * tm
    half = x_ref.shape[1] // 2
    pieces = half // LANES

    def row_copy(step, bufs, sem, r, k):
        d = pl.multiple_of(dest_ref[k * n_tok + step * tm + r], SLAB)
        dst_row = r * SLAB if isinstance(r, int) else pl.multiple_of(r * SLAB, SLAB)
        return pltpu.make_async_copy(y_ref.at[pl.ds(d, pieces)],
                                     bufs[k].at[pl.ds(dst_row, pieces)], sem)

    def drain(step, bufs, sem):
        def wait(r, carry):
            for k in range(TOP_K):
                row_copy(step, bufs, sem, r, k).wait()
            return carry
        lax.fori_loop(0, tm, wait, 0, unroll=8)

    @pl.when(i == 0)
    def _():
        def issue(r, carry):
            for k in range(TOP_K):
                row_copy(0, (ya0, ya1), sems.at[0], r, k).start()
            return carry
        lax.fori_loop(0, tm, issue, 0, unroll=8)

    def tile(bufs, sem, next_bufs, next_sem):
        @pl.when(i + 1 < n_steps)
        def _():
            for r in range(tm):
                for k in range(TOP_K):
                    row_copy(i + 1, next_bufs, next_sem, r, k).start(priority=(r + k) % 2)

        drain(i, bufs, sem)
        g = gate_ref[...]
        for c in range(pieces):
            w = [b[pl.ds(c, tm, stride=SLAB), :] for b in bufs]
            for base, unpack in ((0, lambda v: lax.shift_left(v, jnp.uint32(16))),
                                 (half, lambda v: v & jnp.uint32(0xFFFF0000))):
                cols = slice(base + c * LANES, base + (c + 1) * LANES)
                out_ref[:, cols] = (x_ref[:, cols]
                                    + g[:, 0:1] * pltpu.bitcast(unpack(w[0]), F32)
                                    + g[:, 1:2] * pltpu.bitcast(unpack(w[1]), F32))

    parity = i & 1
    pl.when(parity == 0)(lambda: tile((ya0, ya1), sems.at[0], (yb0, yb1), sems.at[1]))
    pl.when(parity == 1)(lambda: tile((yb0, yb1), sems.at[1], (ya0, ya1), sems.at[0]))


def _combine(dest_slab, x_new, gates, y_slabs):
    n, d = x_new.shape
    tm = ROW_TILE
    grid_spec = pltpu.PrefetchScalarGridSpec(
        num_scalar_prefetch=1,
        grid=(n // tm,),
        in_specs=[
            pl.BlockSpec((tm, d), lambda i, dest: (i, 0)),
            pl.BlockSpec((tm, LANES), lambda i, dest: (i, 0)),
            pl.BlockSpec(memory_space=pl.ANY),
        ],
        out_specs=pl.BlockSpec((tm, d), lambda i, dest: (i, 0)),
        scratch_shapes=[pltpu.VMEM((tm * SLAB, LANES), y_slabs.dtype) for _ in range(2 * TOP_K)]
        + [pltpu.SemaphoreType.DMA((2,))],
    )
    return pl.pallas_call(
        _combine_kernel,
        grid_spec=grid_spec,
        out_shape=jax.ShapeDtypeStruct((n, d), F32),
        compiler_params=_params(1),
        name="moe_combine",
    )(dest_slab, x_new, gates, y_slabs)


def _dispatch_plan(expert_id, slot, counts):
    bm = EXPERT_BLOCK
    n_assign = expert_id.size
    padded = (counts + bm - 1) // bm * bm
    pend = jnp.cumsum(padded)
    pstart = pend - padded
    experts = jnp.arange(N_EXPERTS, dtype=jnp.int32)
    seg_start = jnp.sum(jnp.where(expert_id[..., None] == experts, pstart, 0), axis=-1)
    dest = (seg_start + slot).reshape(-1).astype(jnp.int32)
    buf_len = n_assign + N_EXPERTS * bm
    block_start = jnp.arange(buf_len // bm, dtype=jnp.int32) * bm
    n_used = (pend[-1:] // bm).astype(jnp.int32)
    block_start = jnp.minimum(block_start, pend[-1] - bm)
    block_expert = jnp.sum((pend[None, :] <= block_start[:, None]).astype(jnp.int32), axis=1)
    block_expert = jnp.minimum(block_expert, N_EXPERTS - 1)
    later = jnp.where(block_expert[None, :] > block_expert[:, None], block_expert[None, :], N_EXPERTS)
    next_expert = jnp.min(later, axis=1)
    next_expert = jnp.where(next_expert == N_EXPERTS, block_expert, next_expert).astype(jnp.int32)
    return dest, block_expert, next_expert, n_used, pend.astype(jnp.int32), buf_len


def kernel(x, norm1_g, w_in, q_norm_g, k_norm_g, w_attn_out, w_pool_mix, pool_scale, w_pool_out,
           w_o, norm2_g, w_router_group, b_router_group, w_router_expert, b_router_expert,
           w_exp_gate, w_exp_up, w_exp_down):
    batch, seq, d = x.shape
    assert (seq, d) == (SEQ, D_MODEL)
    n_tok = batch * seq
    depth = w_in.shape[0]
    half = ATTN_WIDTH // 2
    hsum = (jnp.arange(half)[:, None] // HEAD_DIM == jnp.arange(half)[None, :] // HEAD_DIM).astype(BF16)

    xf = x.reshape(n_tok, d)
    for layer in range(depth):
        qg = jnp.tile(q_norm_g[layer], N_HEADS)[None, :]
        kg = jnp.tile(k_norm_g[layer], N_HEADS)[None, :]
        q, k, v, u, gates = _inproj(xf, norm1_g[layer][None, :], w_in[layer].astype(BF16), qg, kg, hsum)
        attn = _attn(q.reshape(batch, seq, ATTN_WIDTH), k.reshape(batch, seq, ATTN_WIDTH),
                     v.reshape(batch, seq, ATTN_WIDTH), batch).reshape(n_tok, ATTN_WIDTH)

        w_router, b_router = _router_operands(
            w_router_group[layer], b_router_group[layer], w_router_expert[layer], b_router_expert[layer])
        x_new, h2, route_i, route_f, counts = _post(
            xf, attn, u, gates, w_attn_out[layer].astype(BF16), w_pool_mix[layer].astype(BF16),
            pool_scale[layer][None, :], w_pool_out[layer].astype(BF16), w_o[layer].astype(BF16),
            norm2_g[layer][None, :], w_router, b_router)

        dest, block_expert, next_expert, n_used, seg_end, buf_len = _dispatch_plan(
            route_i[:TOP_K], route_i[TOP_K:2 * TOP_K], counts[:, 0].astype(jnp.int32))
        sorted_tok = _sorted_tokens(dest, seg_end, buf_len)
        y_rows = _experts(block_expert, n_used, sorted_tok, next_expert, h2,
                          w_exp_gate, w_exp_up, w_exp_down, layer)
        gate_rows = jnp.pad(route_f[:TOP_K].T, ((0, 0), (0, LANES - TOP_K)))
        xf = _combine(dest * SLAB, x_new, gate_rows, y_rows)
    return xf.reshape(batch, seq, d)
```

```python
import functools

import jax
import jax.numpy as jnp
import numpy as np
from jax import lax
from jax.experimental import pallas as pl
from jax.experimental.pallas import tpu as pltpu

D_MODEL = 1024
SEQ = 4096
N_HEADS = 8
HEAD_DIM = 64
ATTN_WIDTH = N_HEADS * HEAD_DIM
DILATIONS = (1, 4, 16)
ATTN_BLOCK = 128
POOL_WINDOWS = (2, 4, 8, 16)
POOL_GROUP_DIM = 128
POOL_WIDTH = len(POOL_WINDOWS) * POOL_GROUP_DIM
POOL_HALO = 16
IN_PROJ_WIDTH = 3 * ATTN_WIDTH + POOL_WIDTH + 2 * D_MODEL
N_EXPERT_GROUPS = 4
EXPERTS_PER_GROUP = 8
N_EXPERTS = N_EXPERT_GROUPS * EXPERTS_PER_GROUP
TOP_K = 2
EXPERT_HIDDEN = 512
RMS_EPS = 1e-6
NEG_INF = -1e30

LANES = 128
ROW_TILE = 256
INPROJ_TILE = 512
POST_TILE = 512
POST_SUBTILE = 512
EXPERT_BLOCK = 256
VMEM_LIMIT = 48 * 1024 * 1024
ATTN_VMEM_LIMIT = 56 * 1024 * 1024
EXPERTS_VMEM_LIMIT = 56 * 1024 * 1024

F32 = jnp.float32
BF16 = jnp.bfloat16


def _params(n_axes):
    return pltpu.CompilerParams(
        dimension_semantics=("arbitrary",) * n_axes, vmem_limit_bytes=VMEM_LIMIT)


def _inproj_kernel(x_ref, g_ref, w_ref, qg_ref, kg_ref, hsum_ref,
                   q_ref, k_ref, v_ref, u_ref, gate_ref):
    x = x_ref[...]
    ms = jnp.mean(x * x, axis=-1, keepdims=True)
    h = (x * lax.rsqrt(ms + RMS_EPS) * g_ref[...]).astype(BF16)

    def proj(lo, hi):
        return jnp.dot(h, w_ref[:, lo:hi], preferred_element_type=F32)

    def head_norm(t, gain):
        sq = (t * t).astype(BF16)
        half = ATTN_WIDTH // 2
        ssq = jnp.concatenate(
            [jnp.dot(sq[:, j * half:(j + 1) * half], hsum_ref[...], preferred_element_type=F32)
             for j in range(2)], axis=-1)
        return t * lax.rsqrt(ssq * (1.0 / HEAD_DIM) + RMS_EPS) * gain

    w = ATTN_WIDTH
    q_ref[...] = head_norm(proj(0, w), qg_ref[...])
    k_ref[...] = head_norm(proj(w, 2 * w), kg_ref[...])
    v_ref[...] = proj(2 * w, 3 * w)
    u_ref[...] = proj(3 * w, 3 * w + POOL_WIDTH)
    base = 3 * w + POOL_WIDTH
    for j in range(2 * D_MODEL // 512):
        gate_ref[:, j * 512:(j + 1) * 512] = jax.nn.sigmoid(
            proj(base + j * 512, base + (j + 1) * 512)).astype(BF16)


def _inproj(x, g1, w_in, qg, kg, hsum):
    n = x.shape[0]
    ROW_TILE = INPROJ_TILE
    row = lambda i: (i, 0)
    const = lambda i: (0, 0)
    return pl.pallas_call(
        _inproj_kernel,
        grid=(n // ROW_TILE,),
        in_specs=[
            pl.BlockSpec((ROW_TILE, D_MODEL), row),
            pl.BlockSpec((1, D_MODEL), const),
            pl.BlockSpec((D_MODEL, IN_PROJ_WIDTH), const),
            pl.BlockSpec((1, ATTN_WIDTH), const),
            pl.BlockSpec((1, ATTN_WIDTH), const),
            pl.BlockSpec((ATTN_WIDTH // 2, ATTN_WIDTH // 2), const),
        ],
        out_specs=[
            pl.BlockSpec((ROW_TILE, ATTN_WIDTH), row),
            pl.BlockSpec((ROW_TILE, ATTN_WIDTH), row),
            pl.BlockSpec((ROW_TILE, ATTN_WIDTH), row),
            pl.BlockSpec((ROW_TILE, POOL_WIDTH), row),
            pl.BlockSpec((ROW_TILE, 2 * D_MODEL), row),
        ],
        out_shape=[
            jax.ShapeDtypeStruct((n, ATTN_WIDTH), F32),
            jax.ShapeDtypeStruct((n, ATTN_WIDTH), F32),
            jax.ShapeDtypeStruct((n, ATTN_WIDTH), F32),
            jax.ShapeDtypeStruct((n, POOL_WIDTH), F32),
            jax.ShapeDtypeStruct((n, 2 * D_MODEL), BF16),
        ],
        compiler_params=_params(1),
        name="inproj",
    )(x, g1, w_in, qg, kg, hsum)


ATTN_UNROLL = 32
QUAD = 4
QUAD_ROWS = SEQ // QUAD
LOG2E = 1.4426950408889634


def _attn_bias():
    blk = ATTN_BLOCK
    r = np.arange(2 * blk) % blk
    c = np.arange(2 * blk)
    per_q, per_k = blk // QUAD, 2 * blk // QUAD
    tq = QUAD * (r % per_q) + r // per_q
    tk = QUAD * (c % per_k) + c // per_k
    masks = []
    for tq_, tk_ in ((tq, tk), (r, c)):
        first = tq_[:, None] - tk_[None, :]
        later = first + blk
        masks += [first >= 0, (later >= 0) & (later <= blk)]
    return np.where(np.stack(masks), 0.0, NEG_INF).astype(np.float32)


def _attn_kernel(q_ref, k_ref, v_ref, bias_ref, out_ref, q4, k4, v4,
                 o0, o1, o2, m0, m1, m2, d0, d1, d2):
    blk = ATTN_BLOCK
    head_a = lax.broadcasted_iota(jnp.int32, (blk, LANES), 1) < HEAD_DIM
    ones_cols = jnp.ones((2 * blk, LANES), BF16)

    for c in range(QUAD):
        dst = pl.ds(c * QUAD_ROWS, QUAD_ROWS)
        src = pl.ds(c, QUAD_ROWS, stride=QUAD)
        q4[dst, :] = q_ref[src, :] * (HEAD_DIM ** -0.5 * LOG2E)
        k4[dst, :] = k_ref[src, :]
        v4[dst, :] = v_ref[src, :]

    def attend(q2, k2, v2, bias):
        qs = jnp.concatenate(
            [jnp.where(head_a, q2, 0.0), jnp.where(head_a, 0.0, q2)], axis=0).astype(BF16)
        s = lax.dot_general(qs, k2.astype(BF16), (((1,), (1,)), ((), ())),
                            preferred_element_type=F32) + bias
        m = jnp.max(s, axis=-1, keepdims=True)
        e = jnp.exp2(s - m).astype(BF16)
        r = jnp.dot(e, jnp.concatenate([v2.astype(BF16), ones_cols], axis=1),
                    preferred_element_type=F32)
        mb = jnp.broadcast_to(m, (2 * blk, LANES))
        return (jnp.where(head_a, r[:blk, :LANES], r[blk:, :LANES]),
                jnp.where(head_a, mb[:blk], mb[blk:]),
                jnp.where(head_a, r[:blk, LANES:], r[blk:, LANES:]))

    def load(ref, pieces):
        return jnp.concatenate([ref[p, :] for p in pieces], axis=0)

    def store(refs, pieces, vals):
        for ref, val in zip(refs, vals):
            row = 0
            for p in pieces:
                ref[p, :] = val[row:row + p.size]
                row += p.size

    def body1(nb, carry):
        kb = jnp.maximum(nb - 1, 0)
        per_q, per_k = blk // QUAD, 2 * blk // QUAD
        qp = [pl.ds(pl.multiple_of(c * QUAD_ROWS + nb * per_q, per_q), per_q) for c in range(QUAD)]
        kp = [pl.ds(pl.multiple_of(c * QUAD_ROWS + kb * per_q, per_q), per_k) for c in range(QUAD)]
        res = attend(load(q4, qp), load(k4, kp), load(v4, kp), bias_ref[jnp.minimum(nb, 1)])
        store((o0, m0, d0), qp, res)
        return carry

    lax.fori_loop(0, SEQ // blk, body1, 0, unroll=ATTN_UNROLL)

    n_blk4 = QUAD_ROWS // blk

    def body4(idx, carry):
        nb = idx & (n_blk4 - 1)
        base = (idx - nb) * blk
        kb = jnp.maximum(nb - 1, 0)
        qp = [pl.ds(pl.multiple_of(base + nb * blk, blk), blk)]
        kp = [pl.ds(pl.multiple_of(base + kb * blk, blk), 2 * blk)]
        res = attend(load(q4, qp), load(k4, kp), load(v4, kp), bias_ref[2 + jnp.minimum(nb, 1)])
        store((o1, m1, d1), qp, res)
        return carry

    lax.fori_loop(0, SEQ // blk, body4, 0, unroll=ATTN_UNROLL)

    def body16(idx, carry):
        start = (idx & (QUAD - 1)) * QUAD_ROWS + lax.shift_right_logical(idx, 2)
        kp = [pl.ds(start, 2 * blk, stride=QUAD)]
        k2, v2 = load(k4, kp), load(v4, kp)
        for nb in range(2):
            qp = [pl.ds(start + nb * blk * QUAD, blk, stride=QUAD)]
            store((o2, m2, d2), qp, attend(load(q4, qp), k2, v2, bias_ref[2 + nb]))
        return carry

    lax.fori_loop(0, SEQ // (2 * blk), body16, 0, unroll=ATTN_UNROLL // 2)

    chunk = 512

    def mix(i, carry):
        r = pl.ds(pl.multiple_of(i * chunk, chunk), chunk)
        ma, mb, mc = m0[r, :], m1[r, :], m2[r, :]
        m = jnp.maximum(jnp.maximum(ma, mb), mc)
        wa, wb, wc = jnp.exp2(ma - m), jnp.exp2(mb - m), jnp.exp2(mc - m)
        acc = wa * o0[r, :] + wb * o1[r, :] + wc * o2[r, :]
        den = wa * d0[r, :] + wb * d1[r, :] + wc * d2[r, :]
        per_class = QUAD_ROWS // chunk
        c = i // per_class
        n0 = (i - c * per_class) * chunk
        out_ref[pl.ds(QUAD * n0 + c, chunk, stride=QUAD), :] = acc / den
        return carry

    lax.fori_loop(0, SEQ // chunk, mix, 0)


def _attn(q, k, v, batch):
    spec = pl.BlockSpec((None, SEQ, LANES), lambda b, hp: (b, 0, hp))
    bias = _attn_bias()
    return pl.pallas_call(
        _attn_kernel,
        grid=(batch, ATTN_WIDTH // LANES),
        in_specs=[spec, spec, spec, pl.BlockSpec(bias.shape, lambda b, hp: (0, 0, 0))],
        out_specs=spec,
        out_shape=jax.ShapeDtypeStruct((batch, SEQ, ATTN_WIDTH), F32),
        scratch_shapes=[pltpu.VMEM((SEQ, LANES), F32) for _ in range(12)],
        compiler_params=pltpu.CompilerParams(
            dimension_semantics=("arbitrary", "arbitrary"), vmem_limit_bytes=ATTN_VMEM_LIMIT),
        name="dilated_attn",
    )(q, k, v, jnp.asarray(bias))


def _pack_bf16_pairs(x):
    c = x.shape[1] // 2
    bits = pltpu.bitcast(x.astype(BF16).astype(F32), jnp.uint32)
    return lax.shift_right_logical(bits[:, :c], jnp.uint32(16)) | bits[:, c:]


def _unpack_bf16_pairs(w):
    lo = pltpu.bitcast(lax.shift_left(w, jnp.uint32(16)), F32)
    hi = pltpu.bitcast(w & jnp.uint32(0xFFFF0000), F32)
    return jnp.concatenate([lo, hi], axis=1).astype(BF16)


def _post_kernel(x_ref, attn_ref, u_ref, halo_ref, gate_ref, wao_ref, wmix_ref, pscale_ref,
                 wpo_ref, wo_ref, g2_ref, wr_ref, br_ref,
                 xo_ref, h2_ref, ri_ref, rf_ref, cnt_ref):
    @pl.when(pl.program_id(0) == 0)
    def _():
        cnt_ref[...] = jnp.zeros_like(cnt_ref)

    tile_pos = lax.rem(pl.program_id(0) * x_ref.shape[0], SEQ)
    for r0 in range(0, x_ref.shape[0], POST_SUBTILE):
        _post_subtile(r0, tile_pos, x_ref, attn_ref, u_ref, halo_ref, gate_ref, wao_ref, wmix_ref,
                      pscale_ref, wpo_ref, wo_ref, g2_ref, wr_ref, br_ref,
                      xo_ref, h2_ref, ri_ref, rf_ref, cnt_ref)


def _post_subtile(r0, tile_pos, x_ref, attn_ref, u_ref, halo_ref, gate_ref, wao_ref, wmix_ref,
                  pscale_ref, wpo_ref, wo_ref, g2_ref, wr_ref, br_ref,
                  xo_ref, h2_ref, ri_ref, rf_ref, cnt_ref):
    tm = POST_SUBTILE
    rows = slice(r0, r0 + tm)
    pos0 = tile_pos + r0
    pos = pos0 + lax.broadcasted_iota(jnp.int32, (tm, 1), 0)
    u = u_ref[rows, :]
    if r0 == 0:
        halo = halo_ref[...] * (pos0 > 0).astype(F32)
    else:
        halo = u_ref[r0 - POOL_HALO:r0, :]

    mixed = []
    for gi, w in enumerate(POOL_WINDOWS):
        lo = gi * POOL_GROUP_DIM
        ug = u[:, lo:lo + POOL_GROUP_DIM]
        ext = jnp.concatenate([halo[:, lo:lo + POOL_GROUP_DIM], ug], axis=0)
        shift = 1
        while shift < w:
            ext = ext + pltpu.roll(ext, shift, 0)
            shift *= 2
        cnt = jnp.minimum(pos + 1, w).astype(F32)
        pooled = ext[POOL_HALO:] / cnt - ug
        mixed.append(jnp.dot(pooled.astype(BF16), wmix_ref[gi], preferred_element_type=F32))
    pool_out = (jnp.concatenate(mixed, axis=-1) * pscale_ref[...]).astype(BF16)

    y_a = jnp.dot(attn_ref[rows, :].astype(BF16), wao_ref[...], preferred_element_type=F32)
    y_p = jnp.dot(pool_out, wpo_ref[...], preferred_element_type=F32)
    gates = gate_ref[rows, :]
    merged = gates[:, :D_MODEL].astype(F32) * y_a + gates[:, D_MODEL:].astype(F32) * y_p
    x_new = x_ref[rows, :] + jnp.dot(merged.astype(BF16), wo_ref[...], preferred_element_type=F32)
    xo_ref[rows, :] = x_new

    ms = jnp.mean(x_new * x_new, axis=-1, keepdims=True)
    h2 = x_new * lax.rsqrt(ms + RMS_EPS) * g2_ref[...]
    h_hi = h2.astype(BF16)
    h2_ref[rows, :] = _pack_bf16_pairs(h2)

    h_lo = (h2 - h_hi.astype(F32)).astype(BF16)
    nt = (((1,), (1,)), ((), ()))
    both = lax.dot_general(wr_ref[...], h_hi, nt, preferred_element_type=F32)
    cross = lax.dot_general(wr_ref[:ROUTER_ROWS, :], h_lo, nt, preferred_element_type=F32)
    logits = both[:ROUTER_ROWS] + both[ROUTER_ROWS:] + cross + jnp.concatenate(
        [br_ref[...]] * (tm // LANES), axis=1)

    sub = lax.broadcasted_iota(jnp.int32, (8, tm), 0)
    npg = EXPERTS_PER_GROUP

    def first_max(vals):
        vmax = jnp.max(vals, axis=0, keepdims=True)
        return vmax, jnp.min(jnp.where(vals == vmax, sub, npg), axis=0, keepdims=True)

    def of_group(parts, g_sel):
        out = parts[-1]
        for g in range(N_EXPERT_GROUPS - 2, -1, -1):
            out = jnp.where(g_sel == g, parts[g], out)
        return out

    glog = jnp.where(sub < N_EXPERT_GROUPS, logits[0:8], -jnp.inf)
    gmax, g_sel = first_max(glog)
    g_gate = 1.0 / jnp.sum(jnp.exp(glog - gmax), axis=0, keepdims=True)

    elog = of_group([logits[npg * (g + 1):npg * (g + 2)] for g in range(N_EXPERT_GROUPS)], g_sel)
    e_exp = jnp.exp(elog - jnp.max(elog, axis=0, keepdims=True))
    prob = e_exp / jnp.sum(e_exp, axis=0, keepdims=True)
    p1, i1 = first_max(prob)
    p2, i2 = first_max(jnp.where(sub == i1, -1.0, prob))
    scale = g_gate / (p1 + p2)

    chosen = (sub == i1) | (sub == i2)
    onehot = jnp.concatenate(
        [jnp.where(chosen & (g_sel == g), 1.0, 0.0) for g in range(N_EXPERT_GROUPS)],
        axis=0).astype(BF16)
    t_row = lax.broadcasted_iota(jnp.int32, (tm, tm), 0)
    t_col = lax.broadcasted_iota(jnp.int32, (tm, tm), 1)
    before = jnp.dot(onehot, jnp.where(t_row < t_col, 1.0, 0.0).astype(BF16),
                     preferred_element_type=F32)
    total = jnp.dot(onehot, jnp.ones((tm, LANES), BF16), preferred_element_type=F32)
    seen = cnt_ref[...]
    slot = before + jnp.concatenate([seen] * (tm // LANES), axis=1)
    cnt_ref[...] = seen + total
    slot = of_group([slot[npg * g:npg * (g + 1)] for g in range(N_EXPERT_GROUPS)], g_sel)
    r1 = jnp.sum(jnp.where(sub == i1, slot, 0.0), axis=0, keepdims=True).astype(jnp.int32)
    r2 = jnp.sum(jnp.where(sub == i2, slot, 0.0), axis=0, keepdims=True).astype(jnp.int32)

    e_base = g_sel * npg
    ri_ref[:, rows] = jnp.where(sub == 0, e_base + i1,
                            jnp.where(sub == 1, e_base + i2,
                                      jnp.where(sub == 2, r1, jnp.where(sub == 3, r2, 0))))
    rf_ref[:, rows] = jnp.where(sub == 0, p1 * scale, jnp.where(sub == 1, p2 * scale, 0.0))


ROUTER_ROWS = 8 + N_EXPERTS + 8


def _router_operands(w_group, b_group, w_expert, b_expert):
    d = w_group.shape[0]
    wt = jnp.zeros((ROUTER_ROWS, d), F32)
    wt = wt.at[:N_EXPERT_GROUPS].set(w_group.T)
    wt = wt.at[8:8 + N_EXPERTS].set(w_expert.transpose(0, 2, 1).reshape(N_EXPERTS, d))
    hi = wt.astype(BF16)
    lo = (wt - hi.astype(F32)).astype(BF16)
    b = jnp.zeros((ROUTER_ROWS,), F32).at[:N_EXPERT_GROUPS].set(b_group)
    b = b.at[8:8 + N_EXPERTS].set(b_expert.reshape(-1))
    return jnp.concatenate([hi, lo], axis=0), jnp.broadcast_to(b[:, None], (ROUTER_ROWS, LANES))


def _post(x, attn, u, gates, wao, wmix, pscale, wpo, wo, g2, wr, br):
    n = x.shape[0]
    tm = POST_TILE
    row = lambda i: (i, 0)
    const = lambda i: (0, 0)
    halo_blocks = tm // POOL_HALO
    return pl.pallas_call(
        _post_kernel,
        grid=(n // tm,),
        in_specs=[
            pl.BlockSpec((tm, D_MODEL), row),
            pl.BlockSpec((tm, ATTN_WIDTH), row),
            pl.BlockSpec((tm, POOL_WIDTH), row),
            pl.BlockSpec((POOL_HALO, POOL_WIDTH),
                         lambda i: (jnp.maximum(i * halo_blocks - 1, 0), 0)),
            pl.BlockSpec((tm, 2 * D_MODEL), row),
            pl.BlockSpec((ATTN_WIDTH, D_MODEL), const),
            pl.BlockSpec((len(POOL_WINDOWS), POOL_GROUP_DIM, POOL_GROUP_DIM), lambda i: (0, 0, 0)),
            pl.BlockSpec((1, POOL_WIDTH), const),
            pl.BlockSpec((POOL_WIDTH, D_MODEL), const),
            pl.BlockSpec((D_MODEL, D_MODEL), const),
            pl.BlockSpec((1, D_MODEL), const),
            pl.BlockSpec((2 * ROUTER_ROWS, D_MODEL), const),
            pl.BlockSpec((ROUTER_ROWS, LANES), const),
        ],
        out_specs=[
            pl.BlockSpec((tm, D_MODEL), row),
            pl.BlockSpec((tm, D_MODEL // 2), row),
            pl.BlockSpec((8, tm), lambda i: (0, i)),
            pl.BlockSpec((8, tm), lambda i: (0, i)),
            pl.BlockSpec((N_EXPERTS, LANES), const),
        ],
        out_shape=[
            jax.ShapeDtypeStruct((n, D_MODEL), F32),
            jax.ShapeDtypeStruct((n, D_MODEL // 2), jnp.uint32),
            jax.ShapeDtypeStruct((8, n), jnp.int32),
            jax.ShapeDtypeStruct((8, n), F32),
            jax.ShapeDtypeStruct((N_EXPERTS, LANES), F32),
        ],
        compiler_params=_params(1),
        name="post_attn_router",
    )(x, attn, u, u, gates, wao, wmix, pscale, wpo, wo, g2, wr, br)


ZERO_ROWS = 8
SLAB = D_MODEL // 2 // LANES


def _experts_kernel(layer, be_ref, nused_ref, tok_ref, plan_ref, h_hbm, wg_hbm, wu_hbm, wd_hbm,
                    y_ref, h_vmem, xs0, xs1, wg_f32, wu_f32, wd_f32, wg_bf, wu_bf, wd_bf,
                    h_sem, w_sems):
    i = pl.program_id(0)
    bm = xs0.shape[0]
    n_tok = h_hbm.shape[0]
    n_blocks = pl.num_programs(0)
    changed = (i == 0) | (be_ref[i] != be_ref[jnp.maximum(i - 1, 0)])
    used = i < nused_ref[0]
    cur = i & 1
    this_e, stage = be_ref[i], plan_ref[i]
    next_e, after_e = plan_ref[n_blocks + i], plan_ref[2 * n_blocks + i]

    def weight_copies(e, s):
        return [pltpu.make_async_copy(src.at[layer, e], dst.at[s], w_sems.at[s, n])
                for n, (src, dst) in enumerate(
                    ((wg_hbm, wg_f32), (wu_hbm, wu_f32), (wd_hbm, wd_f32)))]

    @pl.when(i == 0)
    def _():
        @pl.when(used)
        def _():
            for c in weight_copies(this_e, 0):
                c.start(priority=1)

            @pl.when(next_e != this_e)
            def _():
                for n, c in enumerate(weight_copies(next_e, 1)):
                    c.start(priority=n % 2)
        copy = pltpu.make_async_copy(h_hbm, h_vmem.at[pl.ds(0, n_tok)], h_sem)
        copy.start()
        h_vmem[pl.ds(n_tok, ZERO_ROWS), :] = jnp.zeros((ZERO_ROWS, h_vmem.shape[1]), h_vmem.dtype)
        copy.wait()

        def gather(j, carry):
            xs0[pl.ds(j, 1), :] = h_vmem[pl.ds(tok_ref[j], 1), :]
            return carry
        lax.fori_loop(0, bm, gather, 0, unroll=8)

    @pl.when(changed & used)
    def _():
        for c in weight_copies(this_e, stage):
            c.wait()
        wg_bf[...] = wg_f32[stage].astype(BF16)
        wu_bf[...] = wu_f32[stage].astype(BF16)
        wd_bf[...] = wd_f32[stage].astype(BF16)

        @pl.when(after_e != next_e)
        def _():
            for n, c in enumerate(weight_copies(after_e, stage)):
                c.start(priority=n % 2)

    def block(x_ref, x_next_ref):
        nxt = jnp.where(i + 1 < nused_ref[0], i + 1, i) * bm
        for j in range(bm):
            x_next_ref[pl.ds(j, 1), :] = h_vmem[pl.ds(tok_ref[nxt + j], 1), :]

        x = _unpack_bf16_pairs(x_ref[...])
        a = jnp.dot(x, wg_bf[...], preferred_element_type=F32)
        b = jnp.dot(x, wu_bf[...], preferred_element_type=F32)
        mid = (a * jax.nn.sigmoid(a) * b).astype(BF16)
        y = _pack_bf16_pairs(jnp.dot(mid, wd_bf[...], preferred_element_type=F32))
        for c in range(SLAB):
            y_ref[pl.ds(c, bm, stride=SLAB), :] = y[:, c * LANES:(c + 1) * LANES]

    pl.when(used & (cur == 0))(lambda: block(xs0, xs1))
    pl.when(used & (cur == 1))(lambda: block(xs1, xs0))

    @pl.when(jnp.logical_not(used))
    def _():
        y_ref[...] = jnp.zeros_like(y_ref)


def _experts(block_expert, n_used, sorted_tok, next_expert, h2_packed, w_gate, w_up, w_down, layer):
    n_tok, width = h2_packed.shape
    bm = EXPERT_BLOCK
    any_space = pl.BlockSpec(memory_space=pl.ANY)
    grid_spec = pltpu.PrefetchScalarGridSpec(
        num_scalar_prefetch=4,
        grid=(sorted_tok.shape[0] // bm,),
        in_specs=[any_space, any_space, any_space, any_space],
        out_specs=pl.BlockSpec((bm * SLAB, LANES), lambda i, *_: (i, 0)),
        scratch_shapes=[
            pltpu.VMEM((n_tok + ZERO_ROWS, width), h2_packed.dtype),
            pltpu.VMEM((bm, width), h2_packed.dtype),
            pltpu.VMEM((bm, width), h2_packed.dtype),
            pltpu.VMEM((2, D_MODEL, EXPERT_HIDDEN), F32),
            pltpu.VMEM((2, D_MODEL, EXPERT_HIDDEN), F32),
            pltpu.VMEM((2, EXPERT_HIDDEN, D_MODEL), F32),
            pltpu.VMEM((D_MODEL, EXPERT_HIDDEN), BF16),
            pltpu.VMEM((D_MODEL, EXPERT_HIDDEN), BF16),
            pltpu.VMEM((EXPERT_HIDDEN, D_MODEL), BF16),
            pltpu.SemaphoreType.DMA(()),
            pltpu.SemaphoreType.DMA((2, 3)),
        ],
    )
    return pl.pallas_call(
        functools.partial(_experts_kernel, layer),
        grid_spec=grid_spec,
        out_shape=jax.ShapeDtypeStruct((sorted_tok.shape[0] * SLAB, LANES), h2_packed.dtype),
        compiler_params=pltpu.CompilerParams(
            dimension_semantics=("arbitrary",), vmem_limit_bytes=EXPERTS_VMEM_LIMIT),
        name="experts",
    )(block_expert, n_used, sorted_tok, next_expert, h2_packed, w_gate, w_up, w_down)


def _sorted_tokens_kernel(dest_ref, seg_end_ref, tok_ref):
    n_tok = dest_ref.shape[0]
    bm = EXPERT_BLOCK
    buf_len = tok_ref.shape[0]

    def fill_from(start):
        def fill(p, carry):
            tok_ref[start + p] = n_tok
            return carry
        return fill

    for e in range(N_EXPERTS):
        lax.fori_loop(0, bm, fill_from(jnp.maximum(seg_end_ref[e] - bm, 0)), 0, unroll=16)
    lax.fori_loop(0, N_EXPERTS * bm, fill_from(buf_len - N_EXPERTS * bm), 0, unroll=16)

    def place(t, carry):
        both = dest_ref[t]
        tok_ref[both & 0xFFFF] = t
        tok_ref[lax.shift_right_logical(both, 16)] = t
        return carry
    lax.fori_loop(0, n_tok, place, 0, unroll=8)


def _sorted_tokens(dest, seg_end, buf_len):
    assert TOP_K == 2 and buf_len < (1 << 16)
    n_tok = dest.shape[0] // TOP_K
    packed = dest[:n_tok] | (dest[n_tok:] << 16)
    smem = pl.BlockSpec(memory_space=pltpu.SMEM)
    return pl.pallas_call(
        _sorted_tokens_kernel,
        in_specs=[smem, smem],
        out_specs=smem,
        out_shape=jax.ShapeDtypeStruct((buf_len,), jnp.int32),
        name="moe_sorted_tokens",
    )(packed, seg_end)


def _combine_kernel(dest_ref, x_ref, gate_ref, y_ref, out_ref, ya0, ya1, yb0, yb1, sems):
    tm = x_ref.shape[0]
    i = pl.program_id(0)
    n_steps = pl.num_programs(0)
    n_tok = n_steps * tm
    half = x_ref.shape[1] // 2
    pieces = half // LANES

    def row_copy(step, bufs, sem, r, k):
        d = pl.multiple_of(dest_ref[k * n_tok + step * tm + r], SLAB)
        dst_row = r * SLAB if isinstance(r, int) else pl.multiple_of(r * SLAB, SLAB)
        return pltpu.make_async_copy(y_ref.at[pl.ds(d, pieces)],
                                     bufs[k].at[pl.ds(dst_row, pieces)], sem)

    def drain(step, bufs, sem):
        def wait(r, carry):
            for k in range(TOP_K):
                row_copy(step, bufs, sem, r, k).wait()
            return carry
        lax.fori_loop(0, tm, wait, 0, unroll=8)

    @pl.when(i == 0)
    def _():
        def issue(r, carry):
            for k in range(TOP_K):
                row_copy(0, (ya0, ya1), sems.at[0], r, k).start()
            return carry
        lax.fori_loop(0, tm, issue, 0, unroll=8)

    def tile(bufs, sem, next_bufs, next_sem):
        @pl.when(i + 1 < n_steps)
        def _():
            for r in range(tm):
                for k in range(TOP_K):
                    row_copy(i + 1, next_bufs, next_sem, r, k).start(priority=(r + k) % 2)

        drain(i, bufs, sem)
        g = gate_ref[...]
        for c in range(pieces):
            w = [b[pl.ds(c, tm, stride=SLAB), :] for b in bufs]
            for base, unpack in ((0, lambda v: lax.shift_left(v, jnp.uint32(16))),
                                 (half, lambda v: v & jnp.uint32(0xFFFF0000))):
                cols = slice(base + c * LANES, base + (c + 1) * LANES)
                out_ref[:, cols] = (x_ref[:, cols]
                                    + g[:, 0:1] * pltpu.bitcast(unpack(w[0]), F32)
                                    + g[:, 1:2] * pltpu.bitcast(unpack(w[1]), F32))

    parity = i & 1
    pl.when(parity == 0)(lambda: tile((ya0, ya1), sems.at[0], (yb0, yb1), sems.at[1]))
    pl.when(parity == 1)(lambda: tile((yb0, yb1), sems.at[1], (ya0, ya1), sems.at[0]))


def _combine(dest_slab, x_new, gates, y_slabs):
    n, d = x_new.shape
    tm = ROW_TILE
    grid_spec = pltpu.PrefetchScalarGridSpec(
        num_scalar_prefetch=1,
        grid=(n // tm,),
        in_specs=[
            pl.BlockSpec((tm, d), lambda i, dest: (i, 0)),
            pl.BlockSpec((tm, LANES), lambda i, dest: (i, 0)),
            pl.BlockSpec(memory_space=pl.ANY),
        ],
        out_specs=pl.BlockSpec((tm, d), lambda i, dest: (i, 0)),
        scratch_shapes=[pltpu.VMEM((tm * SLAB, LANES), y_slabs.dtype) for _ in range(2 * TOP_K)]
        + [pltpu.SemaphoreType.DMA((2,))],
    )
    return pl.pallas_call(
        _combine_kernel,
        grid_spec=grid_spec,
        out_shape=jax.ShapeDtypeStruct((n, d), F32),
        compiler_params=_params(1),
        name="moe_combine",
    )(dest_slab, x_new, gates, y_slabs)


def _dispatch_plan(expert_id, slot, counts):
    bm = EXPERT_BLOCK
    n_assign = expert_id.size
    padded = (counts + bm - 1) // bm * bm
    pend = jnp.cumsum(padded)
    pstart = pend - padded
    experts = jnp.arange(N_EXPERTS, dtype=jnp.int32)
    seg_start = jnp.sum(jnp.where(expert_id[..., None] == experts, pstart, 0), axis=-1)
    dest = (seg_start + slot).reshape(-1).astype(jnp.int32)
    buf_len = n_assign + N_EXPERTS * bm
    block_start = jnp.arange(buf_len // bm, dtype=jnp.int32) * bm
    n_used = (pend[-1:] // bm).astype(jnp.int32)
    block_start = jnp.minimum(block_start, pend[-1] - bm)
    block_expert = jnp.sum((pend[None, :] <= block_start[:, None]).astype(jnp.int32), axis=1)
    block_expert = jnp.minimum(block_expert, N_EXPERTS - 1)
    def following(e):
        later = jnp.where(block_expert[None, :] > e[:, None], block_expert[None, :], N_EXPERTS)
        nxt = jnp.min(later, axis=1)
        return jnp.where(nxt == N_EXPERTS, e, nxt)

    next_expert = following(block_expert)
    used_before = jnp.sum((counts > 0)[None, :] & (experts[None, :] < block_expert[:, None]), axis=1)
    plan = jnp.concatenate([used_before & 1, next_expert, following(next_expert)]).astype(jnp.int32)
    return dest, block_expert, plan, n_used, pend.astype(jnp.int32), buf_len


def kernel(x, norm1_g, w_in, q_norm_g, k_norm_g, w_attn_out, w_pool_mix, pool_scale, w_pool_out,
           w_o, norm2_g, w_router_group, b_router_group, w_router_expert, b_router_expert,
           w_exp_gate, w_exp_up, w_exp_down):
    batch, seq, d = x.shape
    assert (seq, d) == (SEQ, D_MODEL)
    n_tok = batch * seq
    depth = w_in.shape[0]
    half = ATTN_WIDTH // 2
    hsum = (jnp.arange(half)[:, None] // HEAD_DIM == jnp.arange(half)[None, :] // HEAD_DIM).astype(BF16)

    xf = x.reshape(n_tok, d)
    for layer in range(depth):
        qg = jnp.tile(q_norm_g[layer], N_HEADS)[None, :]
        kg = jnp.tile(k_norm_g[layer], N_HEADS)[None, :]
        q, k, v, u, gates = _inproj(xf, norm1_g[layer][None, :], w_in[layer].astype(BF16), qg, kg, hsum)
        attn = _attn(q.reshape(batch, seq, ATTN_WIDTH), k.reshape(batch, seq, ATTN_WIDTH),
                     v.reshape(batch, seq, ATTN_WIDTH), batch).reshape(n_tok, ATTN_WIDTH)

        w_router, b_router = _router_operands(
            w_router_group[layer], b_router_group[layer], w_router_expert[layer], b_router_expert[layer])
        x_new, h2, route_i, route_f, counts = _post(
            xf, attn, u, gates, w_attn_out[layer].astype(BF16), w_pool_mix[layer].astype(BF16),
            pool_scale[layer][None, :], w_pool_out[layer].astype(BF16), w_o[layer].astype(BF16),
            norm2_g[layer][None, :], w_router, b_router)

        dest, block_expert, next_expert, n_used, seg_end, buf_len = _dispatch_plan(
            route_i[:TOP_K], route_i[TOP_K:2 * TOP_K], counts[:, 0].astype(jnp.int32))
        sorted_tok = _sorted_tokens(dest, seg_end, buf_len)
        y_rows = _experts(block_expert, n_used, sorted_tok, next_expert, h2,
                          w_exp_gate, w_exp_up, w_exp_down, layer)
        gate_rows = jnp.pad(route_f[:TOP_K].T, ((0, 0), (0, LANES - TOP_K)))
        xf = _combine(dest * SLAB, x_new, gate_rows, y_rows)
    return xf.reshape(batch, seq, d)
```

```python
import functools

import jax
import jax.numpy as jnp
import numpy as np
from jax import lax
from jax.experimental import pallas as pl
from jax.experimental.pallas import tpu as pltpu

D_MODEL = 1024
SEQ = 4096
N_HEADS = 8
HEAD_DIM = 64
ATTN_WIDTH = N_HEADS * HEAD_DIM
DILATIONS = (1, 4, 16)
ATTN_BLOCK = 128
POOL_WINDOWS = (2, 4, 8, 16)
POOL_GROUP_DIM = 128
POOL_WIDTH = len(POOL_WINDOWS) * POOL_GROUP_DIM
POOL_HALO = 16
IN_PROJ_WIDTH = 3 * ATTN_WIDTH + POOL_WIDTH + 2 * D_MODEL
N_EXPERT_GROUPS = 4
EXPERTS_PER_GROUP = 8
N_EXPERTS = N_EXPERT_GROUPS * EXPERTS_PER_GROUP
TOP_K = 2
EXPERT_HIDDEN = 512
RMS_EPS = 1e-6
NEG_INF = -1e30

LANES = 128
ROW_TILE = 512
INPROJ_TILE = 512
POST_TILE = 512
POST_SUBTILE = 512
EXPERT_BLOCK = 256
VMEM_LIMIT = 48 * 1024 * 1024
ATTN_VMEM_LIMIT = 56 * 1024 * 1024
EXPERTS_VMEM_LIMIT = 56 * 1024 * 1024

F32 = jnp.float32
BF16 = jnp.bfloat16


def _params(n_axes):
    return pltpu.CompilerParams(
        dimension_semantics=("arbitrary",) * n_axes, vmem_limit_bytes=VMEM_LIMIT)


def _inproj_kernel(x_ref, g_ref, w_ref, qg_ref, kg_ref, hsum_ref,
                   q_ref, k_ref, v_ref, u_ref, gate_ref):
    x = x_ref[...]
    ms = jnp.mean(x * x, axis=-1, keepdims=True)
    h = (x * lax.rsqrt(ms + RMS_EPS) * g_ref[...]).astype(BF16)

    def proj(lo, hi):
        return jnp.dot(h, w_ref[:, lo:hi], preferred_element_type=F32)

    def head_norm(t, gain):
        sq = (t * t).astype(BF16)
        half = ATTN_WIDTH // 2
        ssq = jnp.concatenate(
            [jnp.dot(sq[:, j * half:(j + 1) * half], hsum_ref[...], preferred_element_type=F32)
             for j in range(2)], axis=-1)
        return t * lax.rsqrt(ssq * (1.0 / HEAD_DIM) + RMS_EPS) * gain

    w = ATTN_WIDTH
    q_ref[...] = head_norm(proj(0, w), qg_ref[...])
    k_ref[...] = head_norm(proj(w, 2 * w), kg_ref[...])
    v_ref[...] = proj(2 * w, 3 * w)
    u_ref[...] = proj(3 * w, 3 * w + POOL_WIDTH)
    base = 3 * w + POOL_WIDTH
    for j in range(2 * D_MODEL // 512):
        gate_ref[:, j * 512:(j + 1) * 512] = jax.nn.sigmoid(
            proj(base + j * 512, base + (j + 1) * 512)).astype(BF16)


def _inproj(x, g1, w_in, qg, kg, hsum):
    n = x.shape[0]
    ROW_TILE = INPROJ_TILE
    row = lambda i: (i, 0)
    const = lambda i: (0, 0)
    return pl.pallas_call(
        _inproj_kernel,
        grid=(n // ROW_TILE,),
        in_specs=[
            pl.BlockSpec((ROW_TILE, D_MODEL), row),
            pl.BlockSpec((1, D_MODEL), const),
            pl.BlockSpec((D_MODEL, IN_PROJ_WIDTH), const),
            pl.BlockSpec((1, ATTN_WIDTH), const),
            pl.BlockSpec((1, ATTN_WIDTH), const),
            pl.BlockSpec((ATTN_WIDTH // 2, ATTN_WIDTH // 2), const),
        ],
        out_specs=[
            pl.BlockSpec((ROW_TILE, ATTN_WIDTH), row),
            pl.BlockSpec((ROW_TILE, ATTN_WIDTH), row),
            pl.BlockSpec((ROW_TILE, ATTN_WIDTH), row),
            pl.BlockSpec((ROW_TILE, POOL_WIDTH), row),
            pl.BlockSpec((ROW_TILE, 2 * D_MODEL), row),
        ],
        out_shape=[
            jax.ShapeDtypeStruct((n, ATTN_WIDTH), F32),
            jax.ShapeDtypeStruct((n, ATTN_WIDTH), F32),
            jax.ShapeDtypeStruct((n, ATTN_WIDTH), F32),
            jax.ShapeDtypeStruct((n, POOL_WIDTH), F32),
            jax.ShapeDtypeStruct((n, 2 * D_MODEL), BF16),
        ],
        compiler_params=_params(1),
        name="inproj",
    )(x, g1, w_in, qg, kg, hsum)


ATTN_UNROLL = 32
QUAD = 4
QUAD_ROWS = SEQ // QUAD
LOG2E = 1.4426950408889634


def _attn_bias():
    blk = ATTN_BLOCK
    r = np.arange(2 * blk) % blk
    c = np.arange(2 * blk)
    per_q, per_k = blk // QUAD, 2 * blk // QUAD
    tq = QUAD * (r % per_q) + r // per_q
    tk = QUAD * (c % per_k) + c // per_k
    masks = []
    for tq_, tk_ in ((tq, tk), (r, c)):
        first = tq_[:, None] - tk_[None, :]
        later = first + blk
        masks += [first >= 0, (later >= 0) & (later <= blk)]
    return np.where(np.stack(masks), 0.0, NEG_INF).astype(np.float32)


def _attn_kernel(q_ref, k_ref, v_ref, bias_ref, out_ref, q4, k4, v4,
                 o0, o1, o2, m0, m1, m2, d0, d1, d2):
    blk = ATTN_BLOCK
    head_a = lax.broadcasted_iota(jnp.int32, (blk, LANES), 1) < HEAD_DIM
    ones_cols = jnp.ones((2 * blk, LANES), BF16)

    for c in range(QUAD):
        dst = pl.ds(c * QUAD_ROWS, QUAD_ROWS)
        src = pl.ds(c, QUAD_ROWS, stride=QUAD)
        q4[dst, :] = q_ref[src, :] * (HEAD_DIM ** -0.5 * LOG2E)
        k4[dst, :] = k_ref[src, :]
        v4[dst, :] = v_ref[src, :]

    def attend(q2, k2, v2, bias):
        qs = jnp.concatenate(
            [jnp.where(head_a, q2, 0.0), jnp.where(head_a, 0.0, q2)], axis=0).astype(BF16)
        s = lax.dot_general(qs, k2.astype(BF16), (((1,), (1,)), ((), ())),
                            preferred_element_type=F32) + bias
        m = jnp.max(s, axis=-1, keepdims=True)
        e = jnp.exp2(s - m).astype(BF16)
        r = jnp.dot(e, jnp.concatenate([v2.astype(BF16), ones_cols], axis=1),
                    preferred_element_type=F32)
        mb = jnp.broadcast_to(m, (2 * blk, LANES))
        return (jnp.where(head_a, r[:blk, :LANES], r[blk:, :LANES]),
                jnp.where(head_a, mb[:blk], mb[blk:]),
                jnp.where(head_a, r[:blk, LANES:], r[blk:, LANES:]))

    def load(ref, pieces):
        return jnp.concatenate([ref[p, :] for p in pieces], axis=0)

    def store(refs, pieces, vals):
        for ref, val in zip(refs, vals):
            row = 0
            for p in pieces:
                ref[p, :] = val[row:row + p.size]
                row += p.size

    def body1(nb, carry):
        kb = jnp.maximum(nb - 1, 0)
        per_q, per_k = blk // QUAD, 2 * blk // QUAD
        qp = [pl.ds(pl.multiple_of(c * QUAD_ROWS + nb * per_q, per_q), per_q) for c in range(QUAD)]
        kp = [pl.ds(pl.multiple_of(c * QUAD_ROWS + kb * per_q, per_q), per_k) for c in range(QUAD)]
        res = attend(load(q4, qp), load(k4, kp), load(v4, kp), bias_ref[jnp.minimum(nb, 1)])
        store((o0, m0, d0), qp, res)
        return carry

    lax.fori_loop(0, SEQ // blk, body1, 0, unroll=ATTN_UNROLL)

    n_blk4 = QUAD_ROWS // blk

    def body4(idx, carry):
        nb = idx & (n_blk4 - 1)
        base = (idx - nb) * blk
        kb = jnp.maximum(nb - 1, 0)
        qp = [pl.ds(pl.multiple_of(base + nb * blk, blk), blk)]
        kp = [pl.ds(pl.multiple_of(base + kb * blk, blk), 2 * blk)]
        res = attend(load(q4, qp), load(k4, kp), load(v4, kp), bias_ref[2 + jnp.minimum(nb, 1)])
        store((o1, m1, d1), qp, res)
        return carry

    lax.fori_loop(0, SEQ // blk, body4, 0, unroll=ATTN_UNROLL)

    def body16(idx, carry):
        start = (idx & (QUAD - 1)) * QUAD_ROWS + lax.shift_right_logical(idx, 2)
        kp = [pl.ds(start, 2 * blk, stride=QUAD)]
        k2, v2 = load(k4, kp), load(v4, kp)
        for nb in range(2):
            qp = [pl.ds(start + nb * blk * QUAD, blk, stride=QUAD)]
            store((o2, m2, d2), qp, attend(load(q4, qp), k2, v2, bias_ref[2 + nb]))
        return carry

    lax.fori_loop(0, SEQ // (2 * blk), body16, 0, unroll=ATTN_UNROLL // 2)

    chunk = 512

    def mix(i, carry):
        r = pl.ds(pl.multiple_of(i * chunk, chunk), chunk)
        ma, mb, mc = m0[r, :], m1[r, :], m2[r, :]
        m = jnp.maximum(jnp.maximum(ma, mb), mc)
        wa, wb, wc = jnp.exp2(ma - m), jnp.exp2(mb - m), jnp.exp2(mc - m)
        acc = wa * o0[r, :] + wb * o1[r, :] + wc * o2[r, :]
        den = wa * d0[r, :] + wb * d1[r, :] + wc * d2[r, :]
        per_class = QUAD_ROWS // chunk
        c = i // per_class
        n0 = (i - c * per_class) * chunk
        out_ref[pl.ds(QUAD * n0 + c, chunk, stride=QUAD), :] = acc / den
        return carry

    lax.fori_loop(0, SEQ // chunk, mix, 0)


def _attn(q, k, v, batch):
    spec = pl.BlockSpec((None, SEQ, LANES), lambda b, hp: (b, 0, hp))
    bias = _attn_bias()
    return pl.pallas_call(
        _attn_kernel,
        grid=(batch, ATTN_WIDTH // LANES),
        in_specs=[spec, spec, spec, pl.BlockSpec(bias.shape, lambda b, hp: (0, 0, 0))],
        out_specs=spec,
        out_shape=jax.ShapeDtypeStruct((batch, SEQ, ATTN_WIDTH), F32),
        scratch_shapes=[pltpu.VMEM((SEQ, LANES), F32) for _ in range(12)],
        compiler_params=pltpu.CompilerParams(
            dimension_semantics=("arbitrary", "arbitrary"), vmem_limit_bytes=ATTN_VMEM_LIMIT),
        name="dilated_attn",
    )(q, k, v, jnp.asarray(bias))


def _pack_bf16_pairs(x):
    c = x.shape[1] // 2
    bits = pltpu.bitcast(x.astype(BF16).astype(F32), jnp.uint32)
    return lax.shift_right_logical(bits[:, :c], jnp.uint32(16)) | bits[:, c:]


def _unpack_bf16_pairs(w):
    lo = pltpu.bitcast(lax.shift_left(w, jnp.uint32(16)), F32)
    hi = pltpu.bitcast(w & jnp.uint32(0xFFFF0000), F32)
    return jnp.concatenate([lo, hi], axis=1).astype(BF16)


def _post_kernel(x_ref, attn_ref, u_ref, halo_ref, gate_ref, wao_ref, wmix_ref, pscale_ref,
                 wpo_ref, wo_ref, g2_ref, wr_ref, br_ref,
                 xo_ref, h2_ref, ri_ref, rf_ref, cnt_ref):
    @pl.when(pl.program_id(0) == 0)
    def _():
        cnt_ref[...] = jnp.zeros_like(cnt_ref)

    tile_pos = lax.rem(pl.program_id(0) * x_ref.shape[0], SEQ)
    for r0 in range(0, x_ref.shape[0], POST_SUBTILE):
        _post_subtile(r0, tile_pos, x_ref, attn_ref, u_ref, halo_ref, gate_ref, wao_ref, wmix_ref,
                      pscale_ref, wpo_ref, wo_ref, g2_ref, wr_ref, br_ref,
                      xo_ref, h2_ref, ri_ref, rf_ref, cnt_ref)


def _post_subtile(r0, tile_pos, x_ref, attn_ref, u_ref, halo_ref, gate_ref, wao_ref, wmix_ref,
                  pscale_ref, wpo_ref, wo_ref, g2_ref, wr_ref, br_ref,
                  xo_ref, h2_ref, ri_ref, rf_ref, cnt_ref):
    tm = POST_SUBTILE
    rows = slice(r0, r0 + tm)
    pos0 = tile_pos + r0
    pos = pos0 + lax.broadcasted_iota(jnp.int32, (tm, 1), 0)
    u = u_ref[rows, :]
    if r0 == 0:
        halo = halo_ref[...] * (pos0 > 0).astype(F32)
    else:
        halo = u_ref[r0 - POOL_HALO:r0, :]

    mixed = []
    for gi, w in enumerate(POOL_WINDOWS):
        lo = gi * POOL_GROUP_DIM
        ug = u[:, lo:lo + POOL_GROUP_DIM]
        ext = jnp.concatenate([halo[:, lo:lo + POOL_GROUP_DIM], ug], axis=0)
        shift = 1
        while shift < w:
            ext = ext + pltpu.roll(ext, shift, 0)
            shift *= 2
        cnt = jnp.minimum(pos + 1, w).astype(F32)
        pooled = ext[POOL_HALO:] / cnt - ug
        mixed.append(jnp.dot(pooled.astype(BF16), wmix_ref[gi], preferred_element_type=F32))
    pool_out = (jnp.concatenate(mixed, axis=-1) * pscale_ref[...]).astype(BF16)

    y_a = jnp.dot(attn_ref[rows, :].astype(BF16), wao_ref[...], preferred_element_type=F32)
    y_p = jnp.dot(pool_out, wpo_ref[...], preferred_element_type=F32)
    gates = gate_ref[rows, :]
    merged = gates[:, :D_MODEL].astype(F32) * y_a + gates[:, D_MODEL:].astype(F32) * y_p
    x_new = x_ref[rows, :] + jnp.dot(merged.astype(BF16), wo_ref[...], preferred_element_type=F32)
    xo_ref[rows, :] = x_new

    ms = jnp.mean(x_new * x_new, axis=-1, keepdims=True)
    h2 = x_new * lax.rsqrt(ms + RMS_EPS) * g2_ref[...]
    h_hi = h2.astype(BF16)
    h2_ref[rows, :] = _pack_bf16_pairs(h2)

    h_lo = (h2 - h_hi.astype(F32)).astype(BF16)
    nt = (((1,), (1,)), ((), ()))
    both = lax.dot_general(wr_ref[...], h_hi, nt, preferred_element_type=F32)
    cross = lax.dot_general(wr_ref[:ROUTER_ROWS, :], h_lo, nt, preferred_element_type=F32)
    logits = both[:ROUTER_ROWS] + both[ROUTER_ROWS:] + cross + jnp.concatenate(
        [br_ref[...]] * (tm // LANES), axis=1)

    sub = lax.broadcasted_iota(jnp.int32, (8, tm), 0)
    npg = EXPERTS_PER_GROUP

    def first_max(vals):
        vmax = jnp.max(vals, axis=0, keepdims=True)
        return vmax, jnp.min(jnp.where(vals == vmax, sub, npg), axis=0, keepdims=True)

    def of_group(parts, g_sel):
        out = parts[-1]
        for g in range(N_EXPERT_GROUPS - 2, -1, -1):
            out = jnp.where(g_sel == g, parts[g], out)
        return out

    glog = jnp.where(sub < N_EXPERT_GROUPS, logits[0:8], -jnp.inf)
    gmax, g_sel = first_max(glog)
    g_gate = 1.0 / jnp.sum(jnp.exp(glog - gmax), axis=0, keepdims=True)

    elog = of_group([logits[npg * (g + 1):npg * (g + 2)] for g in range(N_EXPERT_GROUPS)], g_sel)
    e_exp = jnp.exp(elog - jnp.max(elog, axis=0, keepdims=True))
    prob = e_exp / jnp.sum(e_exp, axis=0, keepdims=True)
    p1, i1 = first_max(prob)
    p2, i2 = first_max(jnp.where(sub == i1, -1.0, prob))
    scale = g_gate / (p1 + p2)

    chosen = (sub == i1) | (sub == i2)
    onehot = jnp.concatenate(
        [jnp.where(chosen & (g_sel == g), 1.0, 0.0) for g in range(N_EXPERT_GROUPS)],
        axis=0).astype(BF16)
    t_row = lax.broadcasted_iota(jnp.int32, (tm, tm), 0)
    t_col = lax.broadcasted_iota(jnp.int32, (tm, tm), 1)
    before = jnp.dot(onehot, jnp.where(t_row < t_col, 1.0, 0.0).astype(BF16),
                     preferred_element_type=F32)
    total = jnp.dot(onehot, jnp.ones((tm, LANES), BF16), preferred_element_type=F32)
    seen = cnt_ref[...]
    slot = before + jnp.concatenate([seen] * (tm // LANES), axis=1)
    cnt_ref[...] = seen + total
    slot = of_group([slot[npg * g:npg * (g + 1)] for g in range(N_EXPERT_GROUPS)], g_sel)
    r1 = jnp.sum(jnp.where(sub == i1, slot, 0.0), axis=0, keepdims=True).astype(jnp.int32)
    r2 = jnp.sum(jnp.where(sub == i2, slot, 0.0), axis=0, keepdims=True).astype(jnp.int32)

    e_base = g_sel * npg
    ri_ref[:, rows] = jnp.where(sub == 0, e_base + i1,
                            jnp.where(sub == 1, e_base + i2,
                                      jnp.where(sub == 2, r1, jnp.where(sub == 3, r2, 0))))
    rf_ref[:, rows] = jnp.where(sub == 0, p1 * scale, jnp.where(sub == 1, p2 * scale, 0.0))


ROUTER_ROWS = 8 + N_EXPERTS + 8


def _router_operands(w_group, b_group, w_expert, b_expert):
    d = w_group.shape[0]
    wt = jnp.zeros((ROUTER_ROWS, d), F32)
    wt = wt.at[:N_EXPERT_GROUPS].set(w_group.T)
    wt = wt.at[8:8 + N_EXPERTS].set(w_expert.transpose(0, 2, 1).reshape(N_EXPERTS, d))
    hi = wt.astype(BF16)
    lo = (wt - hi.astype(F32)).astype(BF16)
    b = jnp.zeros((ROUTER_ROWS,), F32).at[:N_EXPERT_GROUPS].set(b_group)
    b = b.at[8:8 + N_EXPERTS].set(b_expert.reshape(-1))
    return jnp.concatenate([hi, lo], axis=0), jnp.broadcast_to(b[:, None], (ROUTER_ROWS, LANES))


def _post(x, attn, u, gates, wao, wmix, pscale, wpo, wo, g2, wr, br):
    n = x.shape[0]
    tm = POST_TILE
    row = lambda i: (i, 0)
    const = lambda i: (0, 0)
    halo_blocks = tm // POOL_HALO
    return pl.pallas_call(
        _post_kernel,
        grid=(n // tm,),
        in_specs=[
            pl.BlockSpec((tm, D_MODEL), row),
            pl.BlockSpec((tm, ATTN_WIDTH), row),
            pl.BlockSpec((tm, POOL_WIDTH), row),
            pl.BlockSpec((POOL_HALO, POOL_WIDTH),
                         lambda i: (jnp.maximum(i * halo_blocks - 1, 0), 0)),
            pl.BlockSpec((tm, 2 * D_MODEL), row),
            pl.BlockSpec((ATTN_WIDTH, D_MODEL), const),
            pl.BlockSpec((len(POOL_WINDOWS), POOL_GROUP_DIM, POOL_GROUP_DIM), lambda i: (0, 0, 0)),
            pl.BlockSpec((1, POOL_WIDTH), const),
            pl.BlockSpec((POOL_WIDTH, D_MODEL), const),
            pl.BlockSpec((D_MODEL, D_MODEL), const),
            pl.BlockSpec((1, D_MODEL), const),
            pl.BlockSpec((2 * ROUTER_ROWS, D_MODEL), const),
            pl.BlockSpec((ROUTER_ROWS, LANES), const),
        ],
        out_specs=[
            pl.BlockSpec((tm, D_MODEL), row),
            pl.BlockSpec((tm, D_MODEL // 2), row),
            pl.BlockSpec((8, tm), lambda i: (0, i)),
            pl.BlockSpec((8, tm), lambda i: (0, i)),
            pl.BlockSpec((N_EXPERTS, LANES), const),
        ],
        out_shape=[
            jax.ShapeDtypeStruct((n, D_MODEL), F32),
            jax.ShapeDtypeStruct((n, D_MODEL // 2), jnp.uint32),
            jax.ShapeDtypeStruct((8, n), jnp.int32),
            jax.ShapeDtypeStruct((8, n), F32),
            jax.ShapeDtypeStruct((N_EXPERTS, LANES), F32),
        ],
        compiler_params=_params(1),
        name="post_attn_router",
    )(x, attn, u, u, gates, wao, wmix, pscale, wpo, wo, g2, wr, br)


ZERO_ROWS = 8
SLAB = D_MODEL // 2 // LANES


def _experts_kernel(layer, be_ref, nused_ref, tok_ref, plan_ref, h_hbm, wg_hbm, wu_hbm, wd_hbm,
                    y_ref, h_vmem, xs0, xs1, wg_f32, wu_f32, wd_f32, wg_bf, wu_bf, wd_bf,
                    h_sem, w_sems):
    step = pl.program_id(0)
    bm = xs0.shape[0]
    n_tok = h_hbm.shape[0]
    n_used = nused_ref[0]

    def weight_copies(e):
        return [pltpu.make_async_copy(src.at[layer, e], dst, w_sems.at[n])
                for n, (src, dst) in enumerate(
                    ((wg_hbm, wg_f32), (wu_hbm, wu_f32), (wd_hbm, wd_f32)))]

    @pl.when(step == 0)
    def _():
        @pl.when(n_used > 0)
        def _():
            for c in weight_copies(be_ref[0]):
                c.start(priority=1)
        copy = pltpu.make_async_copy(h_hbm, h_vmem.at[pl.ds(0, n_tok)], h_sem)
        copy.start()
        h_vmem[pl.ds(n_tok, ZERO_ROWS), :] = jnp.zeros((ZERO_ROWS, h_vmem.shape[1]), h_vmem.dtype)
        copy.wait()

        def gather(j, carry):
            xs0[pl.ds(j, 1), :] = h_vmem[pl.ds(tok_ref[j], 1), :]
            return carry
        lax.fori_loop(0, bm, gather, 0, unroll=8)

    def block(i, x_ref, x_next_ref, y_rows):
        used = i < n_used
        changed = (i == 0) | (be_ref[i] != be_ref[jnp.maximum(i - 1, 0)])

        @pl.when(changed & used)
        def _():
            for c in weight_copies(be_ref[i]):
                c.wait()
            wg_bf[...] = wg_f32[...].astype(BF16)
            wu_bf[...] = wu_f32[...].astype(BF16)
            wd_bf[...] = wd_f32[...].astype(BF16)

            @pl.when(plan_ref[i] != be_ref[i])
            def _():
                for n, c in enumerate(weight_copies(plan_ref[i])):
                    c.start(priority=n % 2)

        @pl.when(used)
        def _():
            nxt = jnp.where(i + 1 < n_used, i + 1, i) * bm
            for j in range(bm):
                x_next_ref[pl.ds(j, 1), :] = h_vmem[pl.ds(tok_ref[nxt + j], 1), :]

            x = _unpack_bf16_pairs(x_ref[...])
            a = jnp.dot(x, wg_bf[...], preferred_element_type=F32)
            b = jnp.dot(x, wu_bf[...], preferred_element_type=F32)
            mid = (a * jax.nn.sigmoid(a) * b).astype(BF16)
            y = _pack_bf16_pairs(jnp.dot(mid, wd_bf[...], preferred_element_type=F32))
            for c in range(SLAB):
                y_ref[pl.ds(y_rows + c, bm, stride=SLAB), :] = y[:, c * LANES:(c + 1) * LANES]

        @pl.when(jnp.logical_not(used))
        def _():
            y_ref[pl.ds(y_rows, bm * SLAB), :] = jnp.zeros((bm * SLAB, LANES), y_ref.dtype)

    block(2 * step, xs0, xs1, 0)
    block(2 * step + 1, xs1, xs0, bm * SLAB)


def _experts(block_expert, n_used, sorted_tok, next_expert, h2_packed, w_gate, w_up, w_down, layer):
    n_tok, width = h2_packed.shape
    bm = EXPERT_BLOCK
    any_space = pl.BlockSpec(memory_space=pl.ANY)
    n_blocks = sorted_tok.shape[0] // bm
    assert n_blocks % 2 == 0
    grid_spec = pltpu.PrefetchScalarGridSpec(
        num_scalar_prefetch=4,
        grid=(n_blocks // 2,),
        in_specs=[any_space, any_space, any_space, any_space],
        out_specs=pl.BlockSpec((2 * bm * SLAB, LANES), lambda i, *_: (i, 0)),
        scratch_shapes=[
            pltpu.VMEM((n_tok + ZERO_ROWS, width), h2_packed.dtype),
            pltpu.VMEM((bm, width), h2_packed.dtype),
            pltpu.VMEM((bm, width), h2_packed.dtype),
            pltpu.VMEM((D_MODEL, EXPERT_HIDDEN), F32),
            pltpu.VMEM((D_MODEL, EXPERT_HIDDEN), F32),
            pltpu.VMEM((EXPERT_HIDDEN, D_MODEL), F32),
            pltpu.VMEM((D_MODEL, EXPERT_HIDDEN), BF16),
            pltpu.VMEM((D_MODEL, EXPERT_HIDDEN), BF16),
            pltpu.VMEM((EXPERT_HIDDEN, D_MODEL), BF16),
            pltpu.SemaphoreType.DMA(()),
            pltpu.SemaphoreType.DMA((3,)),
        ],
    )
    return pl.pallas_call(
        functools.partial(_experts_kernel, layer),
        grid_spec=grid_spec,
        out_shape=jax.ShapeDtypeStruct((sorted_tok.shape[0] * SLAB, LANES), h2_packed.dtype),
        compiler_params=pltpu.CompilerParams(
            dimension_semantics=("arbitrary",), vmem_limit_bytes=EXPERTS_VMEM_LIMIT),
        name="experts",
    )(block_expert, n_used, sorted_tok, next_expert, h2_packed, w_gate, w_up, w_down)


def _sorted_tokens_kernel(dest_ref, seg_end_ref, tok_ref):
    n_tok = dest_ref.shape[0]
    bm = EXPERT_BLOCK
    buf_len = tok_ref.shape[0]

    def fill_from(start):
        def fill(p, carry):
            tok_ref[start + p] = n_tok
            return carry
        return fill

    for e in range(N_EXPERTS):
        lax.fori_loop(0, bm, fill_from(jnp.maximum(seg_end_ref[e] - bm, 0)), 0, unroll=16)
    lax.fori_loop(0, N_EXPERTS * bm, fill_from(buf_len - N_EXPERTS * bm), 0, unroll=16)

    def place(t, carry):
        both = dest_ref[t]
        tok_ref[both & 0xFFFF] = t
        tok_ref[lax.shift_right_logical(both, 16)] = t
        return carry
    lax.fori_loop(0, n_tok, place, 0, unroll=8)


def _sorted_tokens(dest, seg_end, buf_len):
    assert TOP_K == 2 and buf_len < (1 << 16)
    n_tok = dest.shape[0] // TOP_K
    packed = dest[:n_tok] | (dest[n_tok:] << 16)
    smem = pl.BlockSpec(memory_space=pltpu.SMEM)
    return pl.pallas_call(
        _sorted_tokens_kernel,
        in_specs=[smem, smem],
        out_specs=smem,
        out_shape=jax.ShapeDtypeStruct((buf_len,), jnp.int32),
        name="moe_sorted_tokens",
    )(packed, seg_end)


def _combine_kernel(dest_ref, x_ref, gate_ref, y_ref, out_ref, ya0, ya1, yb0, yb1, sems):
    tm = x_ref.shape[0]
    i = pl.program_id(0)
    n_steps = pl.num_programs(0)
    n_tok = n_steps * tm
    half = x_ref.shape[1] // 2
    pieces = half // LANES

    def row_copy(step, bufs, sem, r, k):
        d = pl.multiple_of(dest_ref[k * n_tok + step * tm + r], SLAB)
        dst_row = r * SLAB if isinstance(r, int) else pl.multiple_of(r * SLAB, SLAB)
        return pltpu.make_async_copy(y_ref.at[pl.ds(d, pieces)],
                                     bufs[k].at[pl.ds(dst_row, pieces)], sem)

    def drain(step, bufs, sem):
        def wait(r, carry):
            for k in range(TOP_K):
                row_copy(step, bufs, sem, r, k).wait()
            return carry
        lax.fori_loop(0, tm, wait, 0, unroll=8)

    @pl.when(i == 0)
    def _():
        def issue(r, carry):
            for k in range(TOP_K):
                row_copy(0, (ya0, ya1), sems.at[0], r, k).start()
            return carry
        lax.fori_loop(0, tm, issue, 0, unroll=8)

    def tile(bufs, sem, next_bufs, next_sem):
        @pl.when(i + 1 < n_steps)
        def _():
            for r in range(tm):
                for k in range(TOP_K):
                    row_copy(i + 1, next_bufs, next_sem, r, k).start(priority=(r + k) % 2)

        drain(i, bufs, sem)
        g = gate_ref[...]
        for c in range(pieces):
            w = [b[pl.ds(c, tm, stride=SLAB), :] for b in bufs]
            for base, unpack in ((0, lambda v: lax.shift_left(v, jnp.uint32(16))),
                                 (half, lambda v: v & jnp.uint32(0xFFFF0000))):
                cols = slice(base + c * LANES, base + (c + 1) * LANES)
                out_ref[:, cols] = (x_ref[:, cols]
                                    + g[:, 0:1] * pltpu.bitcast(unpack(w[0]), F32)
                                    + g[:, 1:2] * pltpu.bitcast(unpack(w[1]), F32))

    parity = i & 1
    pl.when(parity == 0)(lambda: tile((ya0, ya1), sems.at[0], (yb0, yb1), sems.at[1]))
    pl.when(parity == 1)(lambda: tile((yb0, yb1), sems.at[1], (ya0, ya1), sems.at[0]))


def _combine(dest_slab, x_new, gates, y_slabs):
    n, d = x_new.shape
    tm = ROW_TILE
    grid_spec = pltpu.PrefetchScalarGridSpec(
        num_scalar_prefetch=1,
        grid=(n // tm,),
        in_specs=[
            pl.BlockSpec((tm, d), lambda i, dest: (i, 0)),
            pl.BlockSpec((tm, LANES), lambda i, dest: (i, 0)),
            pl.BlockSpec(memory_space=pl.ANY),
        ],
        out_specs=pl.BlockSpec((tm, d), lambda i, dest: (i, 0)),
        scratch_shapes=[pltpu.VMEM((tm * SLAB, LANES), y_slabs.dtype) for _ in range(2 * TOP_K)]
        + [pltpu.SemaphoreType.DMA((2,))],
    )
    return pl.pallas_call(
        _combine_kernel,
        grid_spec=grid_spec,
        out_shape=jax.ShapeDtypeStruct((n, d), F32),
        compiler_params=_params(1),
        name="moe_combine",
    )(dest_slab, x_new, gates, y_slabs)


def _dispatch_plan(expert_id, slot, counts):
    bm = EXPERT_BLOCK
    n_assign = expert_id.size
    padded = (counts + bm - 1) // bm * bm
    pend = jnp.cumsum(padded)
    pstart = pend - padded
    experts = jnp.arange(N_EXPERTS, dtype=jnp.int32)
    seg_start = jnp.sum(jnp.where(expert_id[..., None] == experts, pstart, 0), axis=-1)
    dest = (seg_start + slot).reshape(-1).astype(jnp.int32)
    buf_len = n_assign + N_EXPERTS * bm
    block_start = jnp.arange(buf_len // bm, dtype=jnp.int32) * bm
    n_used = (pend[-1:] // bm).astype(jnp.int32)
    block_start = jnp.minimum(block_start, pend[-1] - bm)
    block_expert = jnp.sum((pend[None, :] <= block_start[:, None]).astype(jnp.int32), axis=1)
    block_expert = jnp.minimum(block_expert, N_EXPERTS - 1)
    later = jnp.where(block_expert[None, :] > block_expert[:, None], block_expert[None, :], N_EXPERTS)
    next_expert = jnp.min(later, axis=1)
    next_expert = jnp.where(next_expert == N_EXPERTS, block_expert, next_expert).astype(jnp.int32)
    return dest, block_expert, next_expert, n_used, pend.astype(jnp.int32), buf_len


def kernel(x, norm1_g, w_in, q_norm_g, k_norm_g, w_attn_out, w_pool_mix, pool_scale, w_pool_out,
           w_o, norm2_g, w_router_group, b_router_group, w_router_expert, b_router_expert,
           w_exp_gate, w_exp_up, w_exp_down):
    batch, seq, d = x.shape
    assert (seq, d) == (SEQ, D_MODEL)
    n_tok = batch * seq
    depth = w_in.shape[0]
    half = ATTN_WIDTH // 2
    hsum = (jnp.arange(half)[:, None] // HEAD_DIM == jnp.arange(half)[None, :] // HEAD_DIM).astype(BF16)

    xf = x.reshape(n_tok, d)
    for layer in range(depth):
        qg = jnp.tile(q_norm_g[layer], N_HEADS)[None, :]
        kg = jnp.tile(k_norm_g[layer], N_HEADS)[None, :]
        q, k, v, u, gates = _inproj(xf, norm1_g[layer][None, :], w_in[layer].astype(BF16), qg, kg, hsum)
        attn = _attn(q.reshape(batch, seq, ATTN_WIDTH), k.reshape(batch, seq, ATTN_WIDTH),
                     v.reshape(batch, seq, ATTN_WIDTH), batch).reshape(n_tok, ATTN_WIDTH)

        w_router, b_router = _router_operands(
            w_router_group[layer], b_router_group[layer], w_router_expert[layer], b_router_expert[layer])
        x_new, h2, route_i, route_f, counts = _post(
            xf, attn, u, gates, w_attn_out[layer].astype(BF16), w_pool_mix[layer].astype(BF16),
            pool_scale[layer][None, :], w_pool_out[layer].astype(BF16), w_o[layer].astype(BF16),
            norm2_g[layer][None, :], w_router, b_router)

        dest, block_expert, next_expert, n_used, seg_end, buf_len = _dispatch_plan(
            route_i[:TOP_K], route_i[TOP_K:2 * TOP_K], counts[:, 0].astype(jnp.int32))
        sorted_tok = _sorted_tokens(dest, seg_end, buf_len)
        y_rows = _experts(block_expert, n_used, sorted_tok, next_expert, h2,
                          w_exp_gate, w_exp_up, w_exp_down, layer)
        gate_rows = jnp.pad(route_f[:TOP_K].T, ((0, 0), (0, LANES - TOP_K)))
        xf = _combine(dest * SLAB, x_new, gate_rows, y_rows)
    return xf.reshape(batch, seq, d)
```

```python
import functools

import jax
import jax.numpy as jnp
import numpy as np
from jax import lax
from jax.experimental import pallas as pl
from jax.experimental.pallas import tpu as pltpu

D_MODEL = 1024
SEQ = 4096
N_HEADS = 8
HEAD_DIM = 64
ATTN_WIDTH = N_HEADS * HEAD_DIM
DILATIONS = (1, 4, 16)
ATTN_BLOCK = 128
POOL_WINDOWS = (2, 4, 8, 16)
POOL_GROUP_DIM = 128
POOL_WIDTH = len(POOL_WINDOWS) * POOL_GROUP_DIM
POOL_HALO = 16
IN_PROJ_WIDTH = 3 * ATTN_WIDTH + POOL_WIDTH + 2 * D_MODEL
N_EXPERT_GROUPS = 4
EXPERTS_PER_GROUP = 8
N_EXPERTS = N_EXPERT_GROUPS * EXPERTS_PER_GROUP
TOP_K = 2
EXPERT_HIDDEN = 512
RMS_EPS = 1e-6
NEG_INF = -1e30

LANES = 128
ROW_TILE = 512
INPROJ_TILE = 512
POST_TILE = 512
POST_SUBTILE = 512
EXPERT_BLOCK = 256
VMEM_LIMIT = 48 * 1024 * 1024
ATTN_VMEM_LIMIT = 48 * 1024 * 1024
EXPERTS_VMEM_LIMIT = 56 * 1024 * 1024

F32 = jnp.float32
BF16 = jnp.bfloat16


def _params(n_axes):
    return pltpu.CompilerParams(
        dimension_semantics=("arbitrary",) * n_axes, vmem_limit_bytes=VMEM_LIMIT)


def _inproj_kernel(x_ref, g_ref, w_ref, qg_ref, kg_ref, hsum_ref,
                   q_ref, k_ref, v_ref, u_ref, gate_ref, rows_scr):
    tm = x_ref.shape[0]

    def store_quad(out_ref, val):
        for l in range(val.shape[1] // LANES):
            lanes = slice(l * LANES, (l + 1) * LANES)
            rows_scr[l] = val[:, lanes]
            for c in range(QUAD):
                out_ref[c, :, lanes] = rows_scr[l, pl.ds(c, tm // QUAD, stride=QUAD), :]

    x = x_ref[...]
    ms = jnp.mean(x * x, axis=-1, keepdims=True)
    h = (x * lax.rsqrt(ms + RMS_EPS) * g_ref[...]).astype(BF16)

    def proj(lo, hi):
        return jnp.dot(h, w_ref[:, lo:hi], preferred_element_type=F32)

    def head_norm(t, gain):
        sq = (t * t).astype(BF16)
        half = ATTN_WIDTH // 2
        ssq = jnp.concatenate(
            [jnp.dot(sq[:, j * half:(j + 1) * half], hsum_ref[...], preferred_element_type=F32)
             for j in range(2)], axis=-1)
        return t * lax.rsqrt(ssq * (1.0 / HEAD_DIM) + RMS_EPS) * gain

    w = ATTN_WIDTH
    store_quad(q_ref, head_norm(proj(0, w), qg_ref[...]) * (HEAD_DIM ** -0.5 * LOG2E))
    store_quad(k_ref, head_norm(proj(w, 2 * w), kg_ref[...]))
    store_quad(v_ref, proj(2 * w, 3 * w))
    u_ref[...] = proj(3 * w, 3 * w + POOL_WIDTH)
    base = 3 * w + POOL_WIDTH
    for j in range(2 * D_MODEL // 512):
        gate_ref[:, j * 512:(j + 1) * 512] = jax.nn.sigmoid(
            proj(base + j * 512, base + (j + 1) * 512)).astype(BF16)


def _inproj(x, g1, w_in, qg, kg, hsum):
    n = x.shape[0]
    tm = INPROJ_TILE
    tiles_per_seq = SEQ // tm
    row = lambda i: (i, 0)
    const = lambda i: (0, 0)
    quad_spec = pl.BlockSpec((None, QUAD, tm // QUAD, ATTN_WIDTH),
                             lambda i: (i // tiles_per_seq, 0, i % tiles_per_seq, 0))
    quad_shape = jax.ShapeDtypeStruct((n // SEQ, QUAD, QUAD_ROWS, ATTN_WIDTH), F32)
    return pl.pallas_call(
        _inproj_kernel,
        grid=(n // tm,),
        in_specs=[
            pl.BlockSpec((tm, D_MODEL), row),
            pl.BlockSpec((1, D_MODEL), const),
            pl.BlockSpec((D_MODEL, IN_PROJ_WIDTH), const),
            pl.BlockSpec((1, ATTN_WIDTH), const),
            pl.BlockSpec((1, ATTN_WIDTH), const),
            pl.BlockSpec((ATTN_WIDTH // 2, ATTN_WIDTH // 2), const),
        ],
        out_specs=[
            quad_spec, quad_spec, quad_spec,
            pl.BlockSpec((tm, POOL_WIDTH), row),
            pl.BlockSpec((tm, 2 * D_MODEL), row),
        ],
        out_shape=[
            quad_shape, quad_shape, quad_shape,
            jax.ShapeDtypeStruct((n, POOL_WIDTH), F32),
            jax.ShapeDtypeStruct((n, 2 * D_MODEL), BF16),
        ],
        scratch_shapes=[pltpu.VMEM((ATTN_WIDTH // LANES, tm, LANES), F32)],
        compiler_params=_params(1),
        name="inproj",
    )(x, g1, w_in, qg, kg, hsum)


ATTN_UNROLL = 32
QUAD = 4
QUAD_ROWS = SEQ // QUAD
LOG2E = 1.4426950408889634


def _attn_bias():
    blk = ATTN_BLOCK
    r = np.arange(2 * blk) % blk
    c = np.arange(2 * blk)
    per_q, per_k = blk // QUAD, 2 * blk // QUAD
    tq = QUAD * (r % per_q) + r // per_q
    tk = QUAD * (c % per_k) + c // per_k
    masks = []
    for tq_, tk_ in ((tq, tk), (r, c)):
        first = tq_[:, None] - tk_[None, :]
        later = first + blk
        masks += [first >= 0, (later >= 0) & (later <= blk)]
    return np.where(np.stack(masks), 0.0, NEG_INF).astype(np.float32)


def _attn_kernel(q4, k4, v4, bias_ref, out_ref, o0, o1, o2, m0, m1, m2, d0, d1, d2):
    blk = ATTN_BLOCK
    head_a = lax.broadcasted_iota(jnp.int32, (blk, LANES), 1) < HEAD_DIM
    ones_cols = jnp.ones((2 * blk, LANES), BF16)

    def attend(q2, k2, v2, bias):
        qs = jnp.concatenate(
            [jnp.where(head_a, q2, 0.0), jnp.where(head_a, 0.0, q2)], axis=0).astype(BF16)
        s = lax.dot_general(qs, k2.astype(BF16), (((1,), (1,)), ((), ())),
                            preferred_element_type=F32) + bias
        m = jnp.max(s, axis=-1, keepdims=True)
        e = jnp.exp2(s - m).astype(BF16)
        r = jnp.dot(e, jnp.concatenate([v2.astype(BF16), ones_cols], axis=1),
                    preferred_element_type=F32)
        mb = jnp.broadcast_to(m, (2 * blk, LANES))
        return (jnp.where(head_a, r[:blk, :LANES], r[blk:, :LANES]),
                jnp.where(head_a, mb[:blk], mb[blk:]),
                jnp.where(head_a, r[:blk, LANES:], r[blk:, LANES:]))

    def load(ref, pieces):
        return jnp.concatenate([ref[p, :] for p in pieces], axis=0)

    def store(refs, pieces, vals):
        for ref, val in zip(refs, vals):
            row = 0
            for p in pieces:
                ref[p, :] = val[row:row + p.size]
                row += p.size

    def body1(nb, carry):
        kb = jnp.maximum(nb - 1, 0)
        per_q, per_k = blk // QUAD, 2 * blk // QUAD
        qp = [pl.ds(pl.multiple_of(c * QUAD_ROWS + nb * per_q, per_q), per_q) for c in range(QUAD)]
        kp = [pl.ds(pl.multiple_of(c * QUAD_ROWS + kb * per_q, per_q), per_k) for c in range(QUAD)]
        res = attend(load(q4, qp), load(k4, kp), load(v4, kp), bias_ref[jnp.minimum(nb, 1)])
        store((o0, m0, d0), qp, res)
        return carry

    lax.fori_loop(0, SEQ // blk, body1, 0, unroll=ATTN_UNROLL)

    n_blk4 = QUAD_ROWS // blk

    def body4(idx, carry):
        nb = idx & (n_blk4 - 1)
        base = (idx - nb) * blk
        kb = jnp.maximum(nb - 1, 0)
        qp = [pl.ds(pl.multiple_of(base + nb * blk, blk), blk)]
        kp = [pl.ds(pl.multiple_of(base + kb * blk, blk), 2 * blk)]
        res = attend(load(q4, qp), load(k4, kp), load(v4, kp), bias_ref[2 + jnp.minimum(nb, 1)])
        store((o1, m1, d1), qp, res)
        return carry

    lax.fori_loop(0, SEQ // blk, body4, 0, unroll=ATTN_UNROLL)

    def body16(idx, carry):
        start = (idx & (QUAD - 1)) * QUAD_ROWS + lax.shift_right_logical(idx, 2)
        kp = [pl.ds(start, 2 * blk, stride=QUAD)]
        k2, v2 = load(k4, kp), load(v4, kp)
        for nb in range(2):
            qp = [pl.ds(start + nb * blk * QUAD, blk, stride=QUAD)]
            store((o2, m2, d2), qp, attend(load(q4, qp), k2, v2, bias_ref[2 + nb]))
        return carry

    lax.fori_loop(0, SEQ // (2 * blk), body16, 0, unroll=ATTN_UNROLL // 2)

    chunk = 512

    def mix(i, carry):
        r = pl.ds(pl.multiple_of(i * chunk, chunk), chunk)
        ma, mb, mc = m0[r, :], m1[r, :], m2[r, :]
        m = jnp.maximum(jnp.maximum(ma, mb), mc)
        wa, wb, wc = jnp.exp2(ma - m), jnp.exp2(mb - m), jnp.exp2(mc - m)
        acc = wa * o0[r, :] + wb * o1[r, :] + wc * o2[r, :]
        den = wa * d0[r, :] + wb * d1[r, :] + wc * d2[r, :]
        per_class = QUAD_ROWS // chunk
        c = i // per_class
        n0 = (i - c * per_class) * chunk
        out_ref[pl.ds(QUAD * n0 + c, chunk, stride=QUAD), :] = acc / den
        return carry

    lax.fori_loop(0, SEQ // chunk, mix, 0)


def _attn(q, k, v, batch):
    spec = pl.BlockSpec((None, SEQ, LANES), lambda b, hp: (b, 0, hp))
    bias = _attn_bias()
    return pl.pallas_call(
        _attn_kernel,
        grid=(batch, ATTN_WIDTH // LANES),
        in_specs=[spec, spec, spec, pl.BlockSpec(bias.shape, lambda b, hp: (0, 0, 0))],
        out_specs=spec,
        out_shape=jax.ShapeDtypeStruct((batch, SEQ, ATTN_WIDTH), F32),
        scratch_shapes=[pltpu.VMEM((SEQ, LANES), F32) for _ in range(9)],
        compiler_params=pltpu.CompilerParams(
            dimension_semantics=("arbitrary", "arbitrary"), vmem_limit_bytes=ATTN_VMEM_LIMIT),
        name="dilated_attn",
    )(q, k, v, jnp.asarray(bias))


def _pack_bf16_pairs(x):
    c = x.shape[1] // 2
    bits = pltpu.bitcast(x.astype(BF16).astype(F32), jnp.uint32)
    return lax.shift_right_logical(bits[:, :c], jnp.uint32(16)) | bits[:, c:]


def _unpack_bf16_pairs(w):
    lo = pltpu.bitcast(lax.shift_left(w, jnp.uint32(16)), F32)
    hi = pltpu.bitcast(w & jnp.uint32(0xFFFF0000), F32)
    return jnp.concatenate([lo, hi], axis=1).astype(BF16)


def _post_kernel(x_ref, attn_ref, u_ref, halo_ref, gate_ref, wao_ref, wmix_ref, pscale_ref,
                 wpo_ref, wo_ref, g2_ref, wr_ref, br_ref,
                 xo_ref, h2_ref, ri_ref, rf_ref, cnt_ref):
    @pl.when(pl.program_id(0) == 0)
    def _():
        cnt_ref[...] = jnp.zeros_like(cnt_ref)

    tile_pos = lax.rem(pl.program_id(0) * x_ref.shape[0], SEQ)
    for r0 in range(0, x_ref.shape[0], POST_SUBTILE):
        _post_subtile(r0, tile_pos, x_ref, attn_ref, u_ref, halo_ref, gate_ref, wao_ref, wmix_ref,
                      pscale_ref, wpo_ref, wo_ref, g2_ref, wr_ref, br_ref,
                      xo_ref, h2_ref, ri_ref, rf_ref, cnt_ref)


def _post_subtile(r0, tile_pos, x_ref, attn_ref, u_ref, halo_ref, gate_ref, wao_ref, wmix_ref,
                  pscale_ref, wpo_ref, wo_ref, g2_ref, wr_ref, br_ref,
                  xo_ref, h2_ref, ri_ref, rf_ref, cnt_ref):
    tm = POST_SUBTILE
    rows = slice(r0, r0 + tm)
    pos0 = tile_pos + r0
    pos = pos0 + lax.broadcasted_iota(jnp.int32, (tm, 1), 0)
    u = u_ref[rows, :]
    if r0 == 0:
        halo = halo_ref[...] * (pos0 > 0).astype(F32)
    else:
        halo = u_ref[r0 - POOL_HALO:r0, :]

    mixed = []
    for gi, w in enumerate(POOL_WINDOWS):
        lo = gi * POOL_GROUP_DIM
        ug = u[:, lo:lo + POOL_GROUP_DIM]
        ext = jnp.concatenate([halo[:, lo:lo + POOL_GROUP_DIM], ug], axis=0)
        shift = 1
        while shift < w:
            ext = ext + pltpu.roll(ext, shift, 0)
            shift *= 2
        cnt = jnp.minimum(pos + 1, w).astype(F32)
        pooled = ext[POOL_HALO:] / cnt - ug
        mixed.append(jnp.dot(pooled.astype(BF16), wmix_ref[gi], preferred_element_type=F32))
    pool_out = (jnp.concatenate(mixed, axis=-1) * pscale_ref[...]).astype(BF16)

    y_a = jnp.dot(attn_ref[rows, :].astype(BF16), wao_ref[...], preferred_element_type=F32)
    y_p = jnp.dot(pool_out, wpo_ref[...], preferred_element_type=F32)
    gates = gate_ref[rows, :]
    merged = gates[:, :D_MODEL].astype(F32) * y_a + gates[:, D_MODEL:].astype(F32) * y_p
    x_new = x_ref[rows, :] + jnp.dot(merged.astype(BF16), wo_ref[...], preferred_element_type=F32)
    xo_ref[rows, :] = x_new

    ms = jnp.mean(x_new * x_new, axis=-1, keepdims=True)
    h2 = x_new * lax.rsqrt(ms + RMS_EPS) * g2_ref[...]
    h_hi = h2.astype(BF16)
    h2_ref[rows, :] = _pack_bf16_pairs(h2)

    h_lo = (h2 - h_hi.astype(F32)).astype(BF16)
    nt = (((1,), (1,)), ((), ()))
    both = lax.dot_general(wr_ref[...], h_hi, nt, preferred_element_type=F32)
    cross = lax.dot_general(wr_ref[:ROUTER_ROWS, :], h_lo, nt, preferred_element_type=F32)
    logits = both[:ROUTER_ROWS] + both[ROUTER_ROWS:] + cross + jnp.concatenate(
        [br_ref[...]] * (tm // LANES), axis=1)

    sub = lax.broadcasted_iota(jnp.int32, (8, tm), 0)
    npg = EXPERTS_PER_GROUP

    def first_max(vals):
        vmax = jnp.max(vals, axis=0, keepdims=True)
        return vmax, jnp.min(jnp.where(vals == vmax, sub, npg), axis=0, keepdims=True)

    def of_group(parts, g_sel):
        out = parts[-1]
        for g in range(N_EXPERT_GROUPS - 2, -1, -1):
            out = jnp.where(g_sel == g, parts[g], out)
        return out

    glog = jnp.where(sub < N_EXPERT_GROUPS, logits[0:8], -jnp.inf)
    gmax, g_sel = first_max(glog)
    g_gate = 1.0 / jnp.sum(jnp.exp(glog - gmax), axis=0, keepdims=True)

    elog = of_group([logits[npg * (g + 1):npg * (g + 2)] for g in range(N_EXPERT_GROUPS)], g_sel)
    e_exp = jnp.exp(elog - jnp.max(elog, axis=0, keepdims=True))
    prob = e_exp / jnp.sum(e_exp, axis=0, keepdims=True)
    p1, i1 = first_max(prob)
    p2, i2 = first_max(jnp.where(sub == i1, -1.0, prob))
    scale = g_gate / (p1 + p2)

    chosen = (sub == i1) | (sub == i2)
    onehot = jnp.concatenate(
        [jnp.where(chosen & (g_sel == g), 1.0, 0.0) for g in range(N_EXPERT_GROUPS)],
        axis=0).astype(BF16)
    t_row = lax.broadcasted_iota(jnp.int32, (tm, tm), 0)
    t_col = lax.broadcasted_iota(jnp.int32, (tm, tm), 1)
    before = jnp.dot(onehot, jnp.where(t_row < t_col, 1.0, 0.0).astype(BF16),
                     preferred_element_type=F32)
    total = jnp.dot(onehot, jnp.ones((tm, LANES), BF16), preferred_element_type=F32)
    seen = cnt_ref[...]
    slot = before + jnp.concatenate([seen] * (tm // LANES), axis=1)
    cnt_ref[...] = seen + total
    slot = of_group([slot[npg * g:npg * (g + 1)] for g in range(N_EXPERT_GROUPS)], g_sel)
    r1 = jnp.sum(jnp.where(sub == i1, slot, 0.0), axis=0, keepdims=True).astype(jnp.int32)
    r2 = jnp.sum(jnp.where(sub == i2, slot, 0.0), axis=0, keepdims=True).astype(jnp.int32)

    e_base = g_sel * npg
    ri_ref[:, rows] = jnp.where(sub == 0, e_base + i1,
                            jnp.where(sub == 1, e_base + i2,
                                      jnp.where(sub == 2, r1, jnp.where(sub == 3, r2, 0))))
    rf_ref[:, rows] = jnp.where(sub == 0, p1 * scale, jnp.where(sub == 1, p2 * scale, 0.0))


ROUTER_ROWS = 8 + N_EXPERTS + 8


def _router_operands(w_group, b_group, w_expert, b_expert):
    d = w_group.shape[0]
    wt = jnp.zeros((ROUTER_ROWS, d), F32)
    wt = wt.at[:N_EXPERT_GROUPS].set(w_group.T)
    wt = wt.at[8:8 + N_EXPERTS].set(w_expert.transpose(0, 2, 1).reshape(N_EXPERTS, d))
    hi = wt.astype(BF16)
    lo = (wt - hi.astype(F32)).astype(BF16)
    b = jnp.zeros((ROUTER_ROWS,), F32).at[:N_EXPERT_GROUPS].set(b_group)
    b = b.at[8:8 + N_EXPERTS].set(b_expert.reshape(-1))
    return jnp.concatenate([hi, lo], axis=0), jnp.broadcast_to(b[:, None], (ROUTER_ROWS, LANES))


def _post(x, attn, u, gates, wao, wmix, pscale, wpo, wo, g2, wr, br):
    n = x.shape[0]
    tm = POST_TILE
    row = lambda i: (i, 0)
    const = lambda i: (0, 0)
    halo_blocks = tm // POOL_HALO
    return pl.pallas_call(
        _post_kernel,
        grid=(n // tm,),
        in_specs=[
            pl.BlockSpec((tm, D_MODEL), row),
            pl.BlockSpec((tm, ATTN_WIDTH), row),
            pl.BlockSpec((tm, POOL_WIDTH), row),
            pl.BlockSpec((POOL_HALO, POOL_WIDTH),
                         lambda i: (jnp.maximum(i * halo_blocks - 1, 0), 0)),
            pl.BlockSpec((tm, 2 * D_MODEL), row),
            pl.BlockSpec((ATTN_WIDTH, D_MODEL), const),
            pl.BlockSpec((len(POOL_WINDOWS), POOL_GROUP_DIM, POOL_GROUP_DIM), lambda i: (0, 0, 0)),
            pl.BlockSpec((1, POOL_WIDTH), const),
            pl.BlockSpec((POOL_WIDTH, D_MODEL), const),
            pl.BlockSpec((D_MODEL, D_MODEL), const),
            pl.BlockSpec((1, D_MODEL), const),
            pl.BlockSpec((2 * ROUTER_ROWS, D_MODEL), const),
            pl.BlockSpec((ROUTER_ROWS, LANES), const),
        ],
        out_specs=[
            pl.BlockSpec((tm, D_MODEL), row),
            pl.BlockSpec((tm, D_MODEL // 2), row),
            pl.BlockSpec((8, tm), lambda i: (0, i)),
            pl.BlockSpec((8, tm), lambda i: (0, i)),
            pl.BlockSpec((N_EXPERTS, LANES), const),
        ],
        out_shape=[
            jax.ShapeDtypeStruct((n, D_MODEL), F32),
            jax.ShapeDtypeStruct((n, D_MODEL // 2), jnp.uint32),
            jax.ShapeDtypeStruct((8, n), jnp.int32),
            jax.ShapeDtypeStruct((8, n), F32),
            jax.ShapeDtypeStruct((N_EXPERTS, LANES), F32),
        ],
        compiler_params=_params(1),
        name="post_attn_router",
    )(x, attn, u, u, gates, wao, wmix, pscale, wpo, wo, g2, wr, br)


ZERO_ROWS = 8
SLAB = D_MODEL // 2 // LANES


def _experts_kernel(layer, be_ref, nused_ref, tok_ref, plan_ref, h_hbm, wg_hbm, wu_hbm, wd_hbm,
                    y_ref, h_vmem, xs0, xs1, wg_f32, wu_f32, wd_f32, wg_bf, wu_bf, wd_bf,
                    h_sem, w_sems):
    step = pl.program_id(0)
    bm = xs0.shape[0]
    n_tok = h_hbm.shape[0]
    n_used = nused_ref[0]

    def weight_copies(e):
        return [pltpu.make_async_copy(src.at[layer, e], dst, w_sems.at[n])
                for n, (src, dst) in enumerate(
                    ((wg_hbm, wg_f32), (wu_hbm, wu_f32), (wd_hbm, wd_f32)))]

    @pl.when(step == 0)
    def _():
        @pl.when(n_used > 0)
        def _():
            for c in weight_copies(be_ref[0]):
                c.start(priority=1)
        copy = pltpu.make_async_copy(h_hbm, h_vmem.at[pl.ds(0, n_tok)], h_sem)
        copy.start()
        h_vmem[pl.ds(n_tok, ZERO_ROWS), :] = jnp.zeros((ZERO_ROWS, h_vmem.shape[1]), h_vmem.dtype)
        copy.wait()

        def gather(j, carry):
            xs0[pl.ds(j, 1), :] = h_vmem[pl.ds(tok_ref[j], 1), :]
            return carry
        lax.fori_loop(0, bm, gather, 0, unroll=8)

    def block(i, x_ref, x_next_ref, y_rows):
        used = i < n_used
        changed = (i == 0) | (be_ref[i] != be_ref[jnp.maximum(i - 1, 0)])

        @pl.when(changed & used)
        def _():
            for c in weight_copies(be_ref[i]):
                c.wait()
            wg_bf[...] = wg_f32[...].astype(BF16)
            wu_bf[...] = wu_f32[...].astype(BF16)
            wd_bf[...] = wd_f32[...].astype(BF16)

            @pl.when(plan_ref[i] != be_ref[i])
            def _():
                for n, c in enumerate(weight_copies(plan_ref[i])):
                    c.start(priority=n % 2)

        @pl.when(used)
        def _():
            nxt = jnp.where(i + 1 < n_used, i + 1, i) * bm
            for j in range(bm):
                x_next_ref[pl.ds(j, 1), :] = h_vmem[pl.ds(tok_ref[nxt + j], 1), :]

            x = _unpack_bf16_pairs(x_ref[...])
            a = jnp.dot(x, wg_bf[...], preferred_element_type=F32)
            b = jnp.dot(x, wu_bf[...], preferred_element_type=F32)
            mid = (a * jax.nn.sigmoid(a) * b).astype(BF16)
            y = _pack_bf16_pairs(jnp.dot(mid, wd_bf[...], preferred_element_type=F32))
            for c in range(SLAB):
                y_ref[pl.ds(y_rows + c, bm, stride=SLAB), :] = y[:, c * LANES:(c + 1) * LANES]

        @pl.when(jnp.logical_not(used))
        def _():
            y_ref[pl.ds(y_rows, bm * SLAB), :] = jnp.zeros((bm * SLAB, LANES), y_ref.dtype)

    block(2 * step, xs0, xs1, 0)
    block(2 * step + 1, xs1, xs0, bm * SLAB)


def _experts(block_expert, n_used, sorted_tok, next_expert, h2_packed, w_gate, w_up, w_down, layer):
    n_tok, width = h2_packed.shape
    bm = EXPERT_BLOCK
    any_space = pl.BlockSpec(memory_space=pl.ANY)
    n_blocks = sorted_tok.shape[0] // bm
    assert n_blocks % 2 == 0
    grid_spec = pltpu.PrefetchScalarGridSpec(
        num_scalar_prefetch=4,
        grid=(n_blocks // 2,),
        in_specs=[any_space, any_space, any_space, any_space],
        out_specs=pl.BlockSpec((2 * bm * SLAB, LANES), lambda i, *_: (i, 0)),
        scratch_shapes=[
            pltpu.VMEM((n_tok + ZERO_ROWS, width), h2_packed.dtype),
            pltpu.VMEM((bm, width), h2_packed.dtype),
            pltpu.VMEM((bm, width), h2_packed.dtype),
            pltpu.VMEM((D_MODEL, EXPERT_HIDDEN), F32),
            pltpu.VMEM((D_MODEL, EXPERT_HIDDEN), F32),
            pltpu.VMEM((EXPERT_HIDDEN, D_MODEL), F32),
            pltpu.VMEM((D_MODEL, EXPERT_HIDDEN), BF16),
            pltpu.VMEM((D_MODEL, EXPERT_HIDDEN), BF16),
            pltpu.VMEM((EXPERT_HIDDEN, D_MODEL), BF16),
            pltpu.SemaphoreType.DMA(()),
            pltpu.SemaphoreType.DMA((3,)),
        ],
    )
    return pl.pallas_call(
        functools.partial(_experts_kernel, layer),
        grid_spec=grid_spec,
        out_shape=jax.ShapeDtypeStruct((sorted_tok.shape[0] * SLAB, LANES), h2_packed.dtype),
        compiler_params=pltpu.CompilerParams(
            dimension_semantics=("arbitrary",), vmem_limit_bytes=EXPERTS_VMEM_LIMIT),
        name="experts",
    )(block_expert, n_used, sorted_tok, next_expert, h2_packed, w_gate, w_up, w_down)


def _sorted_tokens_kernel(dest_ref, seg_end_ref, tok_ref):
    n_tok = dest_ref.shape[0]
    bm = EXPERT_BLOCK
    buf_len = tok_ref.shape[0]

    def fill_from(start):
        def fill(p, carry):
            tok_ref[start + p] = n_tok
            return carry
        return fill

    for e in range(N_EXPERTS):
        lax.fori_loop(0, bm, fill_from(jnp.maximum(seg_end_ref[e] - bm, 0)), 0, unroll=16)
    lax.fori_loop(0, N_EXPERTS * bm, fill_from(buf_len - N_EXPERTS * bm), 0, unroll=16)

    def place(t, carry):
        both = dest_ref[t]
        tok_ref[both & 0xFFFF] = t
        tok_ref[lax.shift_right_logical(both, 16)] = t
        return carry
    lax.fori_loop(0, n_tok, place, 0, unroll=8)


def _sorted_tokens(dest, seg_end, buf_len):
    assert TOP_K == 2 and buf_len < (1 << 16)
    n_tok = dest.shape[0] // TOP_K
    packed = dest[:n_tok] | (dest[n_tok:] << 16)
    smem = pl.BlockSpec(memory_space=pltpu.SMEM)
    return pl.pallas_call(
        _sorted_tokens_kernel,
        in_specs=[smem, smem],
        out_specs=smem,
        out_shape=jax.ShapeDtypeStruct((buf_len,), jnp.int32),
        name="moe_sorted_tokens",
    )(packed, seg_end)


def _combine_kernel(dest_ref, x_ref, gate_ref, y_ref, out_ref, ya0, ya1, yb0, yb1, sems):
    tm = x_ref.shape[0]
    i = pl.program_id(0)
    n_steps = pl.num_programs(0)
    n_tok = n_steps * tm
    half = x_ref.shape[1] // 2
    pieces = half // LANES

    def row_copy(step, bufs, sem, r, k):
        d = pl.multiple_of(dest_ref[k * n_tok + step * tm + r], SLAB)
        dst_row = r * SLAB if isinstance(r, int) else pl.multiple_of(r * SLAB, SLAB)
        return pltpu.make_async_copy(y_ref.at[pl.ds(d, pieces)],
                                     bufs[k].at[pl.ds(dst_row, pieces)], sem)

    def drain(step, bufs, sem):
        def wait(r, carry):
            for k in range(TOP_K):
                row_copy(step, bufs, sem, r, k).wait()
            return carry
        lax.fori_loop(0, tm, wait, 0, unroll=8)

    @pl.when(i == 0)
    def _():
        def issue(r, carry):
            for k in range(TOP_K):
                row_copy(0, (ya0, ya1), sems.at[0], r, k).start()
            return carry
        lax.fori_loop(0, tm, issue, 0, unroll=8)

    def tile(bufs, sem, next_bufs, next_sem):
        @pl.when(i + 1 < n_steps)
        def _():
            for r in range(tm):
                for k in range(TOP_K):
                    row_copy(i + 1, next_bufs, next_sem, r, k).start(priority=(r + k) % 2)

        drain(i, bufs, sem)
        g = gate_ref[...]
        for c in range(pieces):
            w = [b[pl.ds(c, tm, stride=SLAB), :] for b in bufs]
            for base, unpack in ((0, lambda v: lax.shift_left(v, jnp.uint32(16))),
                                 (half, lambda v: v & jnp.uint32(0xFFFF0000))):
                cols = slice(base + c * LANES, base + (c + 1) * LANES)
                out_ref[:, cols] = (x_ref[:, cols]
                                    + g[:, 0:1] * pltpu.bitcast(unpack(w[0]), F32)
                                    + g[:, 1:2] * pltpu.bitcast(unpack(w[1]), F32))

    parity = i & 1
    pl.when(parity == 0)(lambda: tile((ya0, ya1), sems.at[0], (yb0, yb1), sems.at[1]))
    pl.when(parity == 1)(lambda: tile((yb0, yb1), sems.at[1], (ya0, ya1), sems.at[0]))


def _combine(dest_slab, x_new, gates, y_slabs):
    n, d = x_new.shape
    tm = ROW_TILE
    grid_spec = pltpu.PrefetchScalarGridSpec(
        num_scalar_prefetch=1,
        grid=(n // tm,),
        in_specs=[
            pl.BlockSpec((tm, d), lambda i, dest: (i, 0)),
            pl.BlockSpec((tm, LANES), lambda i, dest: (i, 0)),
            pl.BlockSpec(memory_space=pl.ANY),
        ],
        out_specs=pl.BlockSpec((tm, d), lambda i, dest: (i, 0)),
        scratch_shapes=[pltpu.VMEM((tm * SLAB, LANES), y_slabs.dtype) for _ in range(2 * TOP_K)]
        + [pltpu.SemaphoreType.DMA((2,))],
    )
    return pl.pallas_call(
        _combine_kernel,
        grid_spec=grid_spec,
        out_shape=jax.ShapeDtypeStruct((n, d), F32),
        compiler_params=_params(1),
        name="moe_combine",
    )(dest_slab, x_new, gates, y_slabs)


def _dispatch_plan(expert_id, slot, counts):
    bm = EXPERT_BLOCK
    n_assign = expert_id.size
    padded = (counts + bm - 1) // bm * bm
    pend = jnp.cumsum(padded)
    pstart = pend - padded
    experts = jnp.arange(N_EXPERTS, dtype=jnp.int32)
    seg_start = jnp.sum(jnp.where(expert_id[..., None] == experts, pstart, 0), axis=-1)
    dest = (seg_start + slot).reshape(-1).astype(jnp.int32)
    buf_len = n_assign + N_EXPERTS * bm
    block_start = jnp.arange(buf_len // bm, dtype=jnp.int32) * bm
    n_used = (pend[-1:] // bm).astype(jnp.int32)
    block_start = jnp.minimum(block_start, pend[-1] - bm)
    block_expert = jnp.sum((pend[None, :] <= block_start[:, None]).astype(jnp.int32), axis=1)
    block_expert = jnp.minimum(block_expert, N_EXPERTS - 1)
    later = jnp.where(block_expert[None, :] > block_expert[:, None], block_expert[None, :], N_EXPERTS)
    next_expert = jnp.min(later, axis=1)
    next_expert = jnp.where(next_expert == N_EXPERTS, block_expert, next_expert).astype(jnp.int32)
    return dest, block_expert, next_expert, n_used, pend.astype(jnp.int32), buf_len


def kernel(x, norm1_g, w_in, q_norm_g, k_norm_g, w_attn_out, w_pool_mix, pool_scale, w_pool_out,
           w_o, norm2_g, w_router_group, b_router_group, w_router_expert, b_router_expert,
           w_exp_gate, w_exp_up, w_exp_down):
    batch, seq, d = x.shape
    assert (seq, d) == (SEQ, D_MODEL)
    n_tok = batch * seq
    depth = w_in.shape[0]
    half = ATTN_WIDTH // 2
    hsum = (jnp.arange(half)[:, None] // HEAD_DIM == jnp.arange(half)[None, :] // HEAD_DIM).astype(BF16)

    xf = x.reshape(n_tok, d)
    for layer in range(depth):
        qg = jnp.tile(q_norm_g[layer], N_HEADS)[None, :]
        kg = jnp.tile(k_norm_g[layer], N_HEADS)[None, :]
        q, k, v, u, gates = _inproj(xf, norm1_g[layer][None, :], w_in[layer].astype(BF16), qg, kg, hsum)
        attn = _attn(q.reshape(batch, seq, ATTN_WIDTH), k.reshape(batch, seq, ATTN_WIDTH),
                     v.reshape(batch, seq, ATTN_WIDTH), batch).reshape(n_tok, ATTN_WIDTH)

        w_router, b_router = _router_operands(
            w_router_group[layer], b_router_group[layer], w_router_expert[layer], b_router_expert[layer])
        x_new, h2, route_i, route_f, counts = _post(
            xf, attn, u, gates, w_attn_out[layer].astype(BF16), w_pool_mix[layer].astype(BF16),
            pool_scale[layer][None, :], w_pool_out[layer].astype(BF16), w_o[layer].astype(BF16),
            norm2_g[layer][None, :], w_router, b_router)

        dest, block_expert, next_expert, n_used, seg_end, buf_len = _dispatch_plan(
            route_i[:TOP_K], route_i[TOP_K:2 * TOP_K], counts[:, 0].astype(jnp.int32))
        sorted_tok = _sorted_tokens(dest, seg_end, buf_len)
        y_rows = _experts(block_expert, n_used, sorted_tok, next_expert, h2,
                          w_exp_gate, w_exp_up, w_exp_down, layer)
        gate_rows = jnp.pad(route_f[:TOP_K].T, ((0, 0), (0, LANES - TOP_K)))
        xf = _combine(dest * SLAB, x_new, gate_rows, y_rows)
    return xf.reshape(batch, seq, d)
```

```python
import functools

import jax
import jax.numpy as jnp
import numpy as np
from jax import lax
from jax.experimental import pallas as pl
from jax.experimental.pallas import tpu as pltpu

D_MODEL = 1024
SEQ = 4096
N_HEADS = 8
HEAD_DIM = 64
ATTN_WIDTH = N_HEADS * HEAD_DIM
DILATIONS = (1, 4, 16)
ATTN_BLOCK = 128
POOL_WINDOWS = (2, 4, 8, 16)
POOL_GROUP_DIM = 128
POOL_WIDTH = len(POOL_WINDOWS) * POOL_GROUP_DIM
POOL_HALO = 16
IN_PROJ_WIDTH = 3 * ATTN_WIDTH + POOL_WIDTH + 2 * D_MODEL
N_EXPERT_GROUPS = 4
EXPERTS_PER_GROUP = 8
N_EXPERTS = N_EXPERT_GROUPS * EXPERTS_PER_GROUP
TOP_K = 2
EXPERT_HIDDEN = 512
RMS_EPS = 1e-6
NEG_INF = -1e30

LANES = 128
ROW_TILE = 512
INPROJ_TILE = 512
POST_TILE = 512
POST_SUBTILE = 512
EXPERT_BLOCK = 256
VMEM_LIMIT = 48 * 1024 * 1024
ATTN_VMEM_LIMIT = 52 * 1024 * 1024
EXPERTS_VMEM_LIMIT = 56 * 1024 * 1024

F32 = jnp.float32
BF16 = jnp.bfloat16


def _params(n_axes):
    return pltpu.CompilerParams(
        dimension_semantics=("arbitrary",) * n_axes, vmem_limit_bytes=VMEM_LIMIT)


def _inproj_kernel(x_ref, g_ref, w_ref, qg_ref, kg_ref, hsum_ref,
                   q_ref, k_ref, v_ref, u_ref, gate_ref):
    x = x_ref[...]
    ms = jnp.mean(x * x, axis=-1, keepdims=True)
    h = (x * lax.rsqrt(ms + RMS_EPS) * g_ref[...]).astype(BF16)

    def proj(lo, hi):
        return jnp.dot(h, w_ref[:, lo:hi], preferred_element_type=F32)

    def head_norm(t, gain):
        sq = (t * t).astype(BF16)
        half = ATTN_WIDTH // 2
        ssq = jnp.concatenate(
            [jnp.dot(sq[:, j * half:(j + 1) * half], hsum_ref[...], preferred_element_type=F32)
             for j in range(2)], axis=-1)
        return t * lax.rsqrt(ssq * (1.0 / HEAD_DIM) + RMS_EPS) * gain

    w = ATTN_WIDTH
    q_ref[...] = head_norm(proj(0, w), qg_ref[...])
    k_ref[...] = head_norm(proj(w, 2 * w), kg_ref[...])
    v_ref[...] = proj(2 * w, 3 * w)
    u_ref[...] = proj(3 * w, 3 * w + POOL_WIDTH)
    base = 3 * w + POOL_WIDTH
    for j in range(2 * D_MODEL // 512):
        gate_ref[:, j * 512:(j + 1) * 512] = jax.nn.sigmoid(
            proj(base + j * 512, base + (j + 1) * 512)).astype(BF16)


def _inproj(x, g1, w_in, qg, kg, hsum):
    n = x.shape[0]
    tm = INPROJ_TILE
    row = lambda i: (i, 0)
    const = lambda i: (0, 0)
    return pl.pallas_call(
        _inproj_kernel,
        grid=(n // tm,),
        in_specs=[
            pl.BlockSpec((tm, D_MODEL), row),
            pl.BlockSpec((1, D_MODEL), const),
            pl.BlockSpec((D_MODEL, IN_PROJ_WIDTH), const),
            pl.BlockSpec((1, ATTN_WIDTH), const),
            pl.BlockSpec((1, ATTN_WIDTH), const),
            pl.BlockSpec((ATTN_WIDTH // 2, ATTN_WIDTH // 2), const),
        ],
        out_specs=[
            pl.BlockSpec((tm, ATTN_WIDTH), row),
            pl.BlockSpec((tm, ATTN_WIDTH), row),
            pl.BlockSpec((tm, ATTN_WIDTH), row),
            pl.BlockSpec((tm, POOL_WIDTH), row),
            pl.BlockSpec((tm, 2 * D_MODEL), row),
        ],
        out_shape=[
            jax.ShapeDtypeStruct((n, ATTN_WIDTH), F32),
            jax.ShapeDtypeStruct((n, ATTN_WIDTH), F32),
            jax.ShapeDtypeStruct((n, ATTN_WIDTH), F32),
            jax.ShapeDtypeStruct((n, POOL_WIDTH), F32),
            jax.ShapeDtypeStruct((n, 2 * D_MODEL), BF16),
        ],
        compiler_params=_params(1),
        name="inproj",
    )(x, g1, w_in, qg, kg, hsum)


ATTN_UNROLL = 32
QUAD = 4
QUAD_ROWS = SEQ // QUAD
LOG2E = 1.4426950408889634


def _attn_bias():
    blk = ATTN_BLOCK
    r = np.arange(2 * blk) % blk
    c = np.arange(2 * blk)
    per_q, per_k = blk // QUAD, 2 * blk // QUAD
    tq = QUAD * (r % per_q) + r // per_q
    tk = QUAD * (c % per_k) + c // per_k
    masks = []
    for tq_, tk_ in ((tq, tk), (r, c)):
        first = tq_[:, None] - tk_[None, :]
        later = first + blk
        masks += [first >= 0, (later >= 0) & (later <= blk)]
    return np.where(np.stack(masks), 0.0, NEG_INF).astype(np.float32)


def _attn_kernel(q_ref, k_ref, v_ref, bias_ref, out_ref, q4, k4, v4,
                 o0, o1, m0, m1, d0, d1, merged):
    blk = ATTN_BLOCK
    head_a = lax.broadcasted_iota(jnp.int32, (blk, LANES), 1) < HEAD_DIM
    ones_cols = jnp.ones((2 * blk, LANES), BF16)

    for c in range(QUAD):
        dst = pl.ds(c * QUAD_ROWS, QUAD_ROWS)
        src = pl.ds(c, QUAD_ROWS, stride=QUAD)
        q4[dst, :] = q_ref[src, :] * (HEAD_DIM ** -0.5 * LOG2E)
        k4[dst, :] = k_ref[src, :]
        v4[dst, :] = v_ref[src, :]

    def attend(q2, k2, v2, bias):
        qs = jnp.concatenate(
            [jnp.where(head_a, q2, 0.0), jnp.where(head_a, 0.0, q2)], axis=0).astype(BF16)
        s = lax.dot_general(qs, k2.astype(BF16), (((1,), (1,)), ((), ())),
                            preferred_element_type=F32) + bias
        m = jnp.max(s, axis=-1, keepdims=True)
        e = jnp.exp2(s - m).astype(BF16)
        r = jnp.dot(e, jnp.concatenate([v2.astype(BF16), ones_cols], axis=1),
                    preferred_element_type=F32)
        mb = jnp.broadcast_to(m, (2 * blk, LANES))
        return (jnp.where(head_a, r[:blk, :LANES], r[blk:, :LANES]),
                jnp.where(head_a, mb[:blk], mb[blk:]),
                jnp.where(head_a, r[:blk, LANES:], r[blk:, LANES:]))

    def load(ref, pieces):
        return jnp.concatenate([ref[p, :] for p in pieces], axis=0)

    def store(refs, pieces, vals):
        for ref, val in zip(refs, vals):
            row = 0
            for p in pieces:
                ref[p, :] = val[row:row + p.size]
                row += p.size

    def body1(nb, carry):
        kb = jnp.maximum(nb - 1, 0)
        per_q, per_k = blk // QUAD, 2 * blk // QUAD
        qp = [pl.ds(pl.multiple_of(c * QUAD_ROWS + nb * per_q, per_q), per_q) for c in range(QUAD)]
        kp = [pl.ds(pl.multiple_of(c * QUAD_ROWS + kb * per_q, per_q), per_k) for c in range(QUAD)]
        res = attend(load(q4, qp), load(k4, kp), load(v4, kp), bias_ref[jnp.minimum(nb, 1)])
        store((o0, m0, d0), qp, res)
        return carry

    lax.fori_loop(0, SEQ // blk, body1, 0, unroll=ATTN_UNROLL)

    n_blk4 = QUAD_ROWS // blk

    def body4(idx, carry):
        nb = idx & (n_blk4 - 1)
        base = (idx - nb) * blk
        kb = jnp.maximum(nb - 1, 0)
        qp = [pl.ds(pl.multiple_of(base + nb * blk, blk), blk)]
        kp = [pl.ds(pl.multiple_of(base + kb * blk, blk), 2 * blk)]
        res = attend(load(q4, qp), load(k4, kp), load(v4, kp), bias_ref[2 + jnp.minimum(nb, 1)])
        store((o1, m1, d1), qp, res)
        return carry

    lax.fori_loop(0, SEQ // blk, body4, 0, unroll=ATTN_UNROLL)

    def body16(idx, carry):
        start = (idx & (QUAD - 1)) * QUAD_ROWS + lax.shift_right_logical(idx, 2)
        kp = [pl.ds(start, 2 * blk, stride=QUAD)]
        k2, v2 = load(k4, kp), load(v4, kp)
        for nb in range(2):
            qp = pl.ds(start + nb * blk * QUAD, blk, stride=QUAD)
            oc, mc, dc = attend(load(q4, [qp]), k2, v2, bias_ref[2 + nb])
            ma, mb = m0[qp, :], m1[qp, :]
            m = jnp.maximum(jnp.maximum(ma, mb), mc)
            wa, wb, wc = jnp.exp2(ma - m), jnp.exp2(mb - m), jnp.exp2(mc - m)
            acc = wa * o0[qp, :] + wb * o1[qp, :] + wc * oc
            den = wa * d0[qp, :] + wb * d1[qp, :] + wc * dc
            merged[qp, :] = acc / den
        return carry

    lax.fori_loop(0, SEQ // (2 * blk), body16, 0, unroll=ATTN_UNROLL // 2)

    for c in range(QUAD):
        out_ref[pl.ds(c, QUAD_ROWS, stride=QUAD), :] = merged[pl.ds(c * QUAD_ROWS, QUAD_ROWS), :]


def _attn(q, k, v, batch):
    spec = pl.BlockSpec((None, SEQ, LANES), lambda b, hp: (b, 0, hp))
    bias = _attn_bias()
    return pl.pallas_call(
        _attn_kernel,
        grid=(batch, ATTN_WIDTH // LANES),
        in_specs=[spec, spec, spec, pl.BlockSpec(bias.shape, lambda b, hp: (0, 0, 0))],
        out_specs=spec,
        out_shape=jax.ShapeDtypeStruct((batch, SEQ, ATTN_WIDTH), F32),
        scratch_shapes=[pltpu.VMEM((SEQ, LANES), F32) for _ in range(10)],
        compiler_params=pltpu.CompilerParams(
            dimension_semantics=("arbitrary", "arbitrary"), vmem_limit_bytes=ATTN_VMEM_LIMIT),
        name="dilated_attn",
    )(q, k, v, jnp.asarray(bias))


def _pack_bf16_pairs(x):
    c = x.shape[1] // 2
    bits = pltpu.bitcast(x.astype(BF16).astype(F32), jnp.uint32)
    return lax.shift_right_logical(bits[:, :c], jnp.uint32(16)) | bits[:, c:]


def _unpack_bf16_pairs(w):
    lo = pltpu.bitcast(lax.shift_left(w, jnp.uint32(16)), F32)
    hi = pltpu.bitcast(w & jnp.uint32(0xFFFF0000), F32)
    return jnp.concatenate([lo, hi], axis=1).astype(BF16)


def _post_kernel(x_ref, attn_ref, u_ref, halo_ref, gate_ref, wao_ref, wmix_ref, pscale_ref,
                 wpo_ref, wo_ref, g2_ref, wr_ref, br_ref,
                 xo_ref, h2_ref, ri_ref, rf_ref, cnt_ref):
    @pl.when(pl.program_id(0) == 0)
    def _():
        cnt_ref[...] = jnp.zeros_like(cnt_ref)

    tile_pos = lax.rem(pl.program_id(0) * x_ref.shape[0], SEQ)
    for r0 in range(0, x_ref.shape[0], POST_SUBTILE):
        _post_subtile(r0, tile_pos, x_ref, attn_ref, u_ref, halo_ref, gate_ref, wao_ref, wmix_ref,
                      pscale_ref, wpo_ref, wo_ref, g2_ref, wr_ref, br_ref,
                      xo_ref, h2_ref, ri_ref, rf_ref, cnt_ref)


def _post_subtile(r0, tile_pos, x_ref, attn_ref, u_ref, halo_ref, gate_ref, wao_ref, wmix_ref,
                  pscale_ref, wpo_ref, wo_ref, g2_ref, wr_ref, br_ref,
                  xo_ref, h2_ref, ri_ref, rf_ref, cnt_ref):
    tm = POST_SUBTILE
    rows = slice(r0, r0 + tm)
    pos0 = tile_pos + r0
    pos = pos0 + lax.broadcasted_iota(jnp.int32, (tm, 1), 0)
    u = u_ref[rows, :]
    if r0 == 0:
        halo = halo_ref[...] * (pos0 > 0).astype(F32)
    else:
        halo = u_ref[r0 - POOL_HALO:r0, :]

    mixed = []
    for gi, w in enumerate(POOL_WINDOWS):
        lo = gi * POOL_GROUP_DIM
        ug = u[:, lo:lo + POOL_GROUP_DIM]
        ext = jnp.concatenate([halo[:, lo:lo + POOL_GROUP_DIM], ug], axis=0)
        shift = 1
        while shift < w:
            ext = ext + pltpu.roll(ext, shift, 0)
            shift *= 2
        cnt = jnp.minimum(pos + 1, w).astype(F32)
        pooled = ext[POOL_HALO:] / cnt - ug
        mixed.append(jnp.dot(pooled.astype(BF16), wmix_ref[gi], preferred_element_type=F32))
    pool_out = (jnp.concatenate(mixed, axis=-1) * pscale_ref[...]).astype(BF16)

    y_a = jnp.dot(attn_ref[rows, :].astype(BF16), wao_ref[...], preferred_element_type=F32)
    y_p = jnp.dot(pool_out, wpo_ref[...], preferred_element_type=F32)
    gates = gate_ref[rows, :]
    merged = gates[:, :D_MODEL].astype(F32) * y_a + gates[:, D_MODEL:].astype(F32) * y_p
    x_new = x_ref[rows, :] + jnp.dot(merged.astype(BF16), wo_ref[...], preferred_element_type=F32)
    xo_ref[rows, :] = x_new

    ms = jnp.mean(x_new * x_new, axis=-1, keepdims=True)
    h2 = x_new * lax.rsqrt(ms + RMS_EPS) * g2_ref[...]
    h_hi = h2.astype(BF16)
    h2_ref[rows, :] = _pack_bf16_pairs(h2)

    h_lo = (h2 - h_hi.astype(F32)).astype(BF16)
    nt = (((1,), (1,)), ((), ()))
    both = lax.dot_general(wr_ref[...], h_hi, nt, preferred_element_type=F32)
    cross = lax.dot_general(wr_ref[:ROUTER_ROWS, :], h_lo, nt, preferred_element_type=F32)
    logits = both[:ROUTER_ROWS] + both[ROUTER_ROWS:] + cross + jnp.concatenate(
        [br_ref[...]] * (tm // LANES), axis=1)

    sub = lax.broadcasted_iota(jnp.int32, (8, tm), 0)
    npg = EXPERTS_PER_GROUP

    def first_max(vals):
        vmax = jnp.max(vals, axis=0, keepdims=True)
        return vmax, jnp.min(jnp.where(vals == vmax, sub, npg), axis=0, keepdims=True)

    def of_group(parts, g_sel):
        out = parts[-1]
        for g in range(N_EXPERT_GROUPS - 2, -1, -1):
            out = jnp.where(g_sel == g, parts[g], out)
        return out

    glog = jnp.where(sub < N_EXPERT_GROUPS, logits[0:8], -jnp.inf)
    gmax, g_sel = first_max(glog)
    g_gate = 1.0 / jnp.sum(jnp.exp(glog - gmax), axis=0, keepdims=True)

    elog = of_group([logits[npg * (g + 1):npg * (g + 2)] for g in range(N_EXPERT_GROUPS)], g_sel)
    e_exp = jnp.exp(elog - jnp.max(elog, axis=0, keepdims=True))
    prob = e_exp / jnp.sum(e_exp, axis=0, keepdims=True)
    p1, i1 = first_max(prob)
    p2, i2 = first_max(jnp.where(sub == i1, -1.0, prob))
    scale = g_gate / (p1 + p2)

    chosen = (sub == i1) | (sub == i2)
    onehot = jnp.concatenate(
        [jnp.where(chosen & (g_sel == g), 1.0, 0.0) for g in range(N_EXPERT_GROUPS)],
        axis=0).astype(BF16)
    t_row = lax.broadcasted_iota(jnp.int32, (tm, tm), 0)
    t_col = lax.broadcasted_iota(jnp.int32, (tm, tm), 1)
    before = jnp.dot(onehot, jnp.where(t_row < t_col, 1.0, 0.0).astype(BF16),
                     preferred_element_type=F32)
    total = jnp.dot(onehot, jnp.ones((tm, LANES), BF16), preferred_element_type=F32)
    seen = cnt_ref[...]
    slot = before + jnp.concatenate([seen] * (tm // LANES), axis=1)
    cnt_ref[...] = seen + total
    slot = of_group([slot[npg * g:npg * (g + 1)] for g in range(N_EXPERT_GROUPS)], g_sel)
    r1 = jnp.sum(jnp.where(sub == i1, slot, 0.0), axis=0, keepdims=True).astype(jnp.int32)
    r2 = jnp.sum(jnp.where(sub == i2, slot, 0.0), axis=0, keepdims=True).astype(jnp.int32)

    e_base = g_sel * npg
    ri_ref[:, rows] = jnp.where(sub == 0, e_base + i1,
                            jnp.where(sub == 1, e_base + i2,
                                      jnp.where(sub == 2, r1, jnp.where(sub == 3, r2, 0))))
    rf_ref[:, rows] = jnp.where(sub == 0, p1 * scale, jnp.where(sub == 1, p2 * scale, 0.0))


ROUTER_ROWS = 8 + N_EXPERTS + 8


def _router_operands(w_group, b_group, w_expert, b_expert):
    d = w_group.shape[0]
    wt = jnp.zeros((ROUTER_ROWS, d), F32)
    wt = wt.at[:N_EXPERT_GROUPS].set(w_group.T)
    wt = wt.at[8:8 + N_EXPERTS].set(w_expert.transpose(0, 2, 1).reshape(N_EXPERTS, d))
    hi = wt.astype(BF16)
    lo = (wt - hi.astype(F32)).astype(BF16)
    b = jnp.zeros((ROUTER_ROWS,), F32).at[:N_EXPERT_GROUPS].set(b_group)
    b = b.at[8:8 + N_EXPERTS].set(b_expert.reshape(-1))
    return jnp.concatenate([hi, lo], axis=0), jnp.broadcast_to(b[:, None], (ROUTER_ROWS, LANES))


def _post(x, attn, u, gates, wao, wmix, pscale, wpo, wo, g2, wr, br):
    n = x.shape[0]
    tm = POST_TILE
    row = lambda i: (i, 0)
    const = lambda i: (0, 0)
    halo_blocks = tm // POOL_HALO
    return pl.pallas_call(
        _post_kernel,
        grid=(n // tm,),
        in_specs=[
            pl.BlockSpec((tm, D_MODEL), row),
            pl.BlockSpec((tm, ATTN_WIDTH), row),
            pl.BlockSpec((tm, POOL_WIDTH), row),
            pl.BlockSpec((POOL_HALO, POOL_WIDTH),
                         lambda i: (jnp.maximum(i * halo_blocks - 1, 0), 0)),
            pl.BlockSpec((tm, 2 * D_MODEL), row),
            pl.BlockSpec((ATTN_WIDTH, D_MODEL), const),
            pl.BlockSpec((len(POOL_WINDOWS), POOL_GROUP_DIM, POOL_GROUP_DIM), lambda i: (0, 0, 0)),
            pl.BlockSpec((1, POOL_WIDTH), const),
            pl.BlockSpec((POOL_WIDTH, D_MODEL), const),
            pl.BlockSpec((D_MODEL, D_MODEL), const),
            pl.BlockSpec((1, D_MODEL), const),
            pl.BlockSpec((2 * ROUTER_ROWS, D_MODEL), const),
            pl.BlockSpec((ROUTER_ROWS, LANES), const),
        ],
        out_specs=[
            pl.BlockSpec((tm, D_MODEL), row),
            pl.BlockSpec((tm, D_MODEL // 2), row),
            pl.BlockSpec((8, tm), lambda i: (0, i)),
            pl.BlockSpec((8, tm), lambda i: (0, i)),
            pl.BlockSpec((N_EXPERTS, LANES), const),
        ],
        out_shape=[
            jax.ShapeDtypeStruct((n, D_MODEL), F32),
            jax.ShapeDtypeStruct((n, D_MODEL // 2), jnp.uint32),
            jax.ShapeDtypeStruct((8, n), jnp.int32),
            jax.ShapeDtypeStruct((8, n), F32),
            jax.ShapeDtypeStruct((N_EXPERTS, LANES), F32),
        ],
        compiler_params=_params(1),
        name="post_attn_router",
    )(x, attn, u, u, gates, wao, wmix, pscale, wpo, wo, g2, wr, br)


ZERO_ROWS = 8
SLAB = D_MODEL // 2 // LANES


def _experts_kernel(layer, be_ref, nused_ref, tok_ref, plan_ref, h_hbm, wg_hbm, wu_hbm, wd_hbm,
                    y_ref, h_vmem, xs0, xs1, wg_f32, wu_f32, wd_f32, wg_bf, wu_bf, wd_bf,
                    h_sem, w_sems):
    step = pl.program_id(0)
    bm = xs0.shape[0]
    n_tok = h_hbm.shape[0]
    n_used = nused_ref[0]

    def weight_copies(e):
        return [pltpu.make_async_copy(src.at[layer, e], dst, w_sems.at[n])
                for n, (src, dst) in enumerate(
                    ((wg_hbm, wg_f32), (wu_hbm, wu_f32), (wd_hbm, wd_f32)))]

    @pl.when(step == 0)
    def _():
        @pl.when(n_used > 0)
        def _():
            for c in weight_copies(be_ref[0]):
                c.start(priority=1)
        copy = pltpu.make_async_copy(h_hbm, h_vmem.at[pl.ds(0, n_tok)], h_sem)
        copy.start()
        h_vmem[pl.ds(n_tok, ZERO_ROWS), :] = jnp.zeros((ZERO_ROWS, h_vmem.shape[1]), h_vmem.dtype)
        copy.wait()

        def gather(j, carry):
            xs0[pl.ds(j, 1), :] = h_vmem[pl.ds(tok_ref[j], 1), :]
            return carry
        lax.fori_loop(0, bm, gather, 0, unroll=8)

    def block(i, x_ref, x_next_ref, y_rows):
        used = i < n_used
        changed = (i == 0) | (be_ref[i] != be_ref[jnp.maximum(i - 1, 0)])

        @pl.when(changed & used)
        def _():
            for c in weight_copies(be_ref[i]):
                c.wait()
            wg_bf[...] = wg_f32[...].astype(BF16)
            wu_bf[...] = wu_f32[...].astype(BF16)
            wd_bf[...] = wd_f32[...].astype(BF16)

            @pl.when(plan_ref[i] != be_ref[i])
            def _():
                for n, c in enumerate(weight_copies(plan_ref[i])):
                    c.start(priority=n % 2)

        @pl.when(used)
        def _():
            nxt = jnp.where(i + 1 < n_used, i + 1, i) * bm
            for j in range(bm):
                x_next_ref[pl.ds(j, 1), :] = h_vmem[pl.ds(tok_ref[nxt + j], 1), :]

            x = _unpack_bf16_pairs(x_ref[...])
            a = jnp.dot(x, wg_bf[...], preferred_element_type=F32)
            b = jnp.dot(x, wu_bf[...], preferred_element_type=F32)
            mid = (a * jax.nn.sigmoid(a) * b).astype(BF16)
            y = _pack_bf16_pairs(jnp.dot(mid, wd_bf[...], preferred_element_type=F32))
            for c in range(SLAB):
                y_ref[pl.ds(y_rows + c, bm, stride=SLAB), :] = y[:, c * LANES:(c + 1) * LANES]

        @pl.when(jnp.logical_not(used))
        def _():
            y_ref[pl.ds(y_rows, bm * SLAB), :] = jnp.zeros((bm * SLAB, LANES), y_ref.dtype)

    block(2 * step, xs0, xs1, 0)
    block(2 * step + 1, xs1, xs0, bm * SLAB)


def _experts(block_expert, n_used, sorted_tok, next_expert, h2_packed, w_gate, w_up, w_down, layer):
    n_tok, width = h2_packed.shape
    bm = EXPERT_BLOCK
    any_space = pl.BlockSpec(memory_space=pl.ANY)
    n_blocks = sorted_tok.shape[0] // bm
    assert n_blocks % 2 == 0
    grid_spec = pltpu.PrefetchScalarGridSpec(
        num_scalar_prefetch=4,
        grid=(n_blocks // 2,),
        in_specs=[any_space, any_space, any_space, any_space],
        out_specs=pl.BlockSpec((2 * bm * SLAB, LANES), lambda i, *_: (i, 0)),
        scratch_shapes=[
            pltpu.VMEM((n_tok + ZERO_ROWS, width), h2_packed.dtype),
            pltpu.VMEM((bm, width), h2_packed.dtype),
            pltpu.VMEM((bm, width), h2_packed.dtype),
            pltpu.VMEM((D_MODEL, EXPERT_HIDDEN), F32),
            pltpu.VMEM((D_MODEL, EXPERT_HIDDEN), F32),
            pltpu.VMEM((EXPERT_HIDDEN, D_MODEL), F32),
            pltpu.VMEM((D_MODEL, EXPERT_HIDDEN), BF16),
            pltpu.VMEM((D_MODEL, EXPERT_HIDDEN), BF16),
            pltpu.VMEM((EXPERT_HIDDEN, D_MODEL), BF16),
            pltpu.SemaphoreType.DMA(()),
            pltpu.SemaphoreType.DMA((3,)),
        ],
    )
    return pl.pallas_call(
        functools.partial(_experts_kernel, layer),
        grid_spec=grid_spec,
        out_shape=jax.ShapeDtypeStruct((sorted_tok.shape[0] * SLAB, LANES), h2_packed.dtype),
        compiler_params=pltpu.CompilerParams(
            dimension_semantics=("arbitrary",), vmem_limit_bytes=EXPERTS_VMEM_LIMIT),
        name="experts",
    )(block_expert, n_used, sorted_tok, next_expert, h2_packed, w_gate, w_up, w_down)


def _sorted_tokens_kernel(dest_ref, seg_end_ref, tok_ref):
    n_tok = dest_ref.shape[0]
    bm = EXPERT_BLOCK
    buf_len = tok_ref.shape[0]

    def fill_from(start):
        def fill(p, carry):
            tok_ref[start + p] = n_tok
            return carry
        return fill

    for e in range(N_EXPERTS):
        lax.fori_loop(0, bm, fill_from(jnp.maximum(seg_end_ref[e] - bm, 0)), 0, unroll=16)
    lax.fori_loop(0, N_EXPERTS * bm, fill_from(buf_len - N_EXPERTS * bm), 0, unroll=16)

    def place(t, carry):
        both = dest_ref[t]
        tok_ref[both & 0xFFFF] = t
        tok_ref[lax.shift_right_logical(both, 16)] = t
        return carry
    lax.fori_loop(0, n_tok, place, 0, unroll=8)


def _sorted_tokens(dest, seg_end, buf_len):
    assert TOP_K == 2 and buf_len < (1 << 16)
    n_tok = dest.shape[0] // TOP_K
    packed = dest[:n_tok] | (dest[n_tok:] << 16)
    smem = pl.BlockSpec(memory_space=pltpu.SMEM)
    return pl.pallas_call(
        _sorted_tokens_kernel,
        in_specs=[smem, smem],
        out_specs=smem,
        out_shape=jax.ShapeDtypeStruct((buf_len,), jnp.int32),
        name="moe_sorted_tokens",
    )(packed, seg_end)


def _combine_kernel(dest_ref, x_ref, gate_ref, y_ref, out_ref, ya0, ya1, yb0, yb1, sems):
    tm = x_ref.shape[0]
    i = pl.program_id(0)
    n_steps = pl.num_programs(0)
    n_tok = n_steps * tm
    half = x_ref.shape[1] // 2
    pieces = half // LANES

    def row_copy(step, bufs, sem, r, k):
        d = pl.multiple_of(dest_ref[k * n_tok + step * tm + r], SLAB)
        dst_row = r * SLAB if isinstance(r, int) else pl.multiple_of(r * SLAB, SLAB)
        return pltpu.make_async_copy(y_ref.at[pl.ds(d, pieces)],
                                     bufs[k].at[pl.ds(dst_row, pieces)], sem)

    def drain(step, bufs, sem):
        def wait(r, carry):
            for k in range(TOP_K):
                row_copy(step, bufs, sem, r, k).wait()
            return carry
        lax.fori_loop(0, tm, wait, 0, unroll=8)

    @pl.when(i == 0)
    def _():
        def issue(r, carry):
            for k in range(TOP_K):
                row_copy(0, (ya0, ya1), sems.at[0], r, k).start()
            return carry
        lax.fori_loop(0, tm, issue, 0, unroll=8)

    def tile(bufs, sem, next_bufs, next_sem):
        @pl.when(i + 1 < n_steps)
        def _():
            for r in range(tm):
                for k in range(TOP_K):
                    row_copy(i + 1, next_bufs, next_sem, r, k).start(priority=(r + k) % 2)

        drain(i, bufs, sem)
        g = gate_ref[...]
        for c in range(pieces):
            w = [b[pl.ds(c, tm, stride=SLAB), :] for b in bufs]
            for base, unpack in ((0, lambda v: lax.shift_left(v, jnp.uint32(16))),
                                 (half, lambda v: v & jnp.uint32(0xFFFF0000))):
                cols = slice(base + c * LANES, base + (c + 1) * LANES)
                out_ref[:, cols] = (x_ref[:, cols]
                                    + g[:, 0:1] * pltpu.bitcast(unpack(w[0]), F32)
                                    + g[:, 1:2] * pltpu.bitcast(unpack(w[1]), F32))

    parity = i & 1
    pl.when(parity == 0)(lambda: tile((ya0, ya1), sems.at[0], (yb0, yb1), sems.at[1]))
    pl.when(parity == 1)(lambda: tile((yb0, yb1), sems.at[1], (ya0, ya1), sems.at[0]))


def _combine(dest_slab, x_new, gates, y_slabs):
    n, d = x_new.shape
    tm = ROW_TILE
    grid_spec = pltpu.PrefetchScalarGridSpec(
        num_scalar_prefetch=1,
        grid=(n // tm,),
        in_specs=[
            pl.BlockSpec((tm, d), lambda i, dest: (i, 0)),
            pl.BlockSpec((tm, LANES), lambda i, dest: (i, 0)),
            pl.BlockSpec(memory_space=pl.ANY),
        ],
        out_specs=pl.BlockSpec((tm, d), lambda i, dest: (i, 0)),
        scratch_shapes=[pltpu.VMEM((tm * SLAB, LANES), y_slabs.dtype) for _ in range(2 * TOP_K)]
        + [pltpu.SemaphoreType.DMA((2,))],
    )
    return pl.pallas_call(
        _combine_kernel,
        grid_spec=grid_spec,
        out_shape=jax.ShapeDtypeStruct((n, d), F32),
        compiler_params=_params(1),
        name="moe_combine",
    )(dest_slab, x_new, gates, y_slabs)


def _dispatch_plan(expert_id, slot, counts):
    bm = EXPERT_BLOCK
    n_assign = expert_id.size
    padded = (counts + bm - 1) // bm * bm
    pend = jnp.cumsum(padded)
    pstart = pend - padded
    experts = jnp.arange(N_EXPERTS, dtype=jnp.int32)
    seg_start = jnp.sum(jnp.where(expert_id[..., None] == experts, pstart, 0), axis=-1)
    dest = (seg_start + slot).reshape(-1).astype(jnp.int32)
    buf_len = n_assign + N_EXPERTS * bm
    block_start = jnp.arange(buf_len // bm, dtype=jnp.int32) * bm
    n_used = (pend[-1:] // bm).astype(jnp.int32)
    block_start = jnp.minimum(block_start, pend[-1] - bm)
    block_expert = jnp.sum((pend[None, :] <= block_start[:, None]).astype(jnp.int32), axis=1)
    block_expert = jnp.minimum(block_expert, N_EXPERTS - 1)
    later = jnp.where(block_expert[None, :] > block_expert[:, None], block_expert[None, :], N_EXPERTS)
    next_expert = jnp.min(later, axis=1)
    next_expert = jnp.where(next_expert == N_EXPERTS, block_expert, next_expert).astype(jnp.int32)
    return dest, block_expert, next_expert, n_used, pend.astype(jnp.int32), buf_len


def kernel(x, norm1_g, w_in, q_norm_g, k_norm_g, w_attn_out, w_pool_mix, pool_scale, w_pool_out,
           w_o, norm2_g, w_router_group, b_router_group, w_router_expert, b_router_expert,
           w_exp_gate, w_exp_up, w_exp_down):
    batch, seq, d = x.shape
    assert (seq, d) == (SEQ, D_MODEL)
    n_tok = batch * seq
    depth = w_in.shape[0]
    half = ATTN_WIDTH // 2
    hsum = (jnp.arange(half)[:, None] // HEAD_DIM == jnp.arange(half)[None, :] // HEAD_DIM).astype(BF16)

    xf = x.reshape(n_tok, d)
    for layer in range(depth):
        qg = jnp.tile(q_norm_g[layer], N_HEADS)[None, :]
        kg = jnp.tile(k_norm_g[layer], N_HEADS)[None, :]
        q, k, v, u, gates = _inproj(xf, norm1_g[layer][None, :], w_in[layer].astype(BF16), qg, kg, hsum)
        attn = _attn(q.reshape(batch, seq, ATTN_WIDTH), k.reshape(batch, seq, ATTN_WIDTH),
                     v.reshape(batch, seq, ATTN_WIDTH), batch).reshape(n_tok, ATTN_WIDTH)

        w_router, b_router = _router_operands(
            w_router_group[layer], b_router_group[layer], w_router_expert[layer], b_router_expert[layer])
        x_new, h2, route_i, route_f, counts = _post(
            xf, attn, u, gates, w_attn_out[layer].astype(BF16), w_pool_mix[layer].astype(BF16),
            pool_scale[layer][None, :], w_pool_out[layer].astype(BF16), w_o[layer].astype(BF16),
            norm2_g[layer][None, :], w_router, b_router)

        dest, block_expert, next_expert, n_used, seg_end, buf_len = _dispatch_plan(
            route_i[:TOP_K], route_i[TOP_K:2 * TOP_K], counts[:, 0].astype(jnp.int32))
        sorted_tok = _sorted_tokens(dest, seg_end, buf_len)
        y_rows = _experts(block_expert, n_used, sorted_tok, next_expert, h2,
                          w_exp_gate, w_exp_up, w_exp_down, layer)
        gate_rows = jnp.pad(route_f[:TOP_K].T, ((0, 0), (0, LANES - TOP_K)))
        xf = _combine(dest * SLAB, x_new, gate_rows, y_rows)
    return xf.reshape(batch, seq, d)
```

```python
import functools

import jax
import jax.numpy as jnp
import numpy as np
from jax import lax
from jax.experimental import pallas as pl
from jax.experimental.pallas import tpu as pltpu

D_MODEL = 1024
SEQ = 4096
N_HEADS = 8
HEAD_DIM = 64
ATTN_WIDTH = N_HEADS * HEAD_DIM
DILATIONS = (1, 4, 16)
ATTN_BLOCK = 128
POOL_WINDOWS = (2, 4, 8, 16)
POOL_GROUP_DIM = 128
POOL_WIDTH = len(POOL_WINDOWS) * POOL_GROUP_DIM
POOL_HALO = 16
IN_PROJ_WIDTH = 3 * ATTN_WIDTH + POOL_WIDTH + 2 * D_MODEL
N_EXPERT_GROUPS = 4
EXPERTS_PER_GROUP = 8
N_EXPERTS = N_EXPERT_GROUPS * EXPERTS_PER_GROUP
TOP_K = 2
EXPERT_HIDDEN = 512
RMS_EPS = 1e-6
NEG_INF = -1e30

LANES = 128
ROW_TILE = 512
INPROJ_TILE = 512
POST_TILE = 512
POST_SUBTILE = 512
EXPERT_BLOCK = 256
VMEM_LIMIT = 48 * 1024 * 1024
ATTN_VMEM_LIMIT = 56 * 1024 * 1024
EXPERTS_VMEM_LIMIT = 56 * 1024 * 1024

F32 = jnp.float32
BF16 = jnp.bfloat16


def _params(n_axes):
    return pltpu.CompilerParams(
        dimension_semantics=("arbitrary",) * n_axes, vmem_limit_bytes=VMEM_LIMIT)


def _inproj_kernel(x_ref, g_ref, w_ref, qg_ref, kg_ref, hsum_ref,
                   q_ref, k_ref, v_ref, u_ref, gate_ref):
    x = x_ref[...]
    ms = jnp.mean(x * x, axis=-1, keepdims=True)
    h = (x * lax.rsqrt(ms + RMS_EPS) * g_ref[...]).astype(BF16)

    def proj(lo, hi):
        return jnp.dot(h, w_ref[:, lo:hi], preferred_element_type=F32)

    def head_norm(t, gain):
        sq = (t * t).astype(BF16)
        half = ATTN_WIDTH // 2
        ssq = jnp.concatenate(
            [jnp.dot(sq[:, j * half:(j + 1) * half], hsum_ref[...], preferred_element_type=F32)
             for j in range(2)], axis=-1)
        return t * lax.rsqrt(ssq * (1.0 / HEAD_DIM) + RMS_EPS) * gain

    w = ATTN_WIDTH
    q_ref[...] = head_norm(proj(0, w), qg_ref[...])
    k_ref[...] = head_norm(proj(w, 2 * w), kg_ref[...])
    v_ref[...] = proj(2 * w, 3 * w)
    u_ref[...] = proj(3 * w, 3 * w + POOL_WIDTH)
    base = 3 * w + POOL_WIDTH
    for j in range(2 * D_MODEL // 512):
        gate_ref[:, j * 512:(j + 1) * 512] = jax.nn.sigmoid(
            proj(base + j * 512, base + (j + 1) * 512)).astype(BF16)


def _inproj(x, g1, w_in, qg, kg, hsum):
    n = x.shape[0]
    tm = INPROJ_TILE
    row = lambda i: (i, 0)
    const = lambda i: (0, 0)
    return pl.pallas_call(
        _inproj_kernel,
        grid=(n // tm,),
        in_specs=[
            pl.BlockSpec((tm, D_MODEL), row),
            pl.BlockSpec((1, D_MODEL), const),
            pl.BlockSpec((D_MODEL, IN_PROJ_WIDTH), const),
            pl.BlockSpec((1, ATTN_WIDTH), const),
            pl.BlockSpec((1, ATTN_WIDTH), const),
            pl.BlockSpec((ATTN_WIDTH // 2, ATTN_WIDTH // 2), const),
        ],
        out_specs=[
            pl.BlockSpec((tm, ATTN_WIDTH), row),
            pl.BlockSpec((tm, ATTN_WIDTH), row),
            pl.BlockSpec((tm, ATTN_WIDTH), row),
            pl.BlockSpec((tm, POOL_WIDTH), row),
            pl.BlockSpec((tm, 2 * D_MODEL), row),
        ],
        out_shape=[
            jax.ShapeDtypeStruct((n, ATTN_WIDTH), F32),
            jax.ShapeDtypeStruct((n, ATTN_WIDTH), F32),
            jax.ShapeDtypeStruct((n, ATTN_WIDTH), F32),
            jax.ShapeDtypeStruct((n, POOL_WIDTH), F32),
            jax.ShapeDtypeStruct((n, 2 * D_MODEL), BF16),
        ],
        compiler_params=_params(1),
        name="inproj",
    )(x, g1, w_in, qg, kg, hsum)


ATTN_UNROLL = 32
QUAD = 4
QUAD_ROWS = SEQ // QUAD
LOG2E = 1.4426950408889634


def _attn_bias():
    blk = ATTN_BLOCK
    r = np.arange(2 * blk) % blk
    c = np.arange(2 * blk)
    per_q, per_k = blk // QUAD, 2 * blk // QUAD
    tq = QUAD * (r % per_q) + r // per_q
    tk = QUAD * (c % per_k) + c // per_k
    masks = []
    for tq_, tk_ in ((tq, tk), (r, c)):
        first = tq_[:, None] - tk_[None, :]
        later = first + blk
        masks += [first >= 0, (later >= 0) & (later <= blk)]
    return np.where(np.stack(masks), 0.0, NEG_INF).astype(np.float32)


def _attn_kernel(q_ref, k_ref, v_ref, bias_ref, out_ref, q4, k4, v4,
                 o0, o1, o2, m0, m1, m2, d0, d1, d2):
    blk = ATTN_BLOCK
    head_a = lax.broadcasted_iota(jnp.int32, (blk, LANES), 1) < HEAD_DIM
    ones_cols = jnp.ones((2 * blk, LANES), BF16)

    for c in range(QUAD):
        dst = pl.ds(c * QUAD_ROWS, QUAD_ROWS)
        src = pl.ds(c, QUAD_ROWS, stride=QUAD)
        q4[dst, :] = q_ref[src, :] * (HEAD_DIM ** -0.5 * LOG2E)
        k4[dst, :] = k_ref[src, :]
        v4[dst, :] = v_ref[src, :]

    def attend(q2, k2, v2, bias):
        qs = jnp.concatenate(
            [jnp.where(head_a, q2, 0.0), jnp.where(head_a, 0.0, q2)], axis=0).astype(BF16)
        s = lax.dot_general(qs, k2.astype(BF16), (((1,), (1,)), ((), ())),
                            preferred_element_type=F32) + bias
        m = jnp.max(s, axis=-1, keepdims=True)
        e = jnp.exp2(s - m).astype(BF16)
        r = jnp.dot(e, jnp.concatenate([v2.astype(BF16), ones_cols], axis=1),
                    preferred_element_type=F32)
        mb = jnp.broadcast_to(m, (2 * blk, LANES))
        return (jnp.where(head_a, r[:blk, :LANES], r[blk:, :LANES]),
                jnp.where(head_a, mb[:blk], mb[blk:]),
                jnp.where(head_a, r[:blk, LANES:], r[blk:, LANES:]))

    def load(ref, pieces):
        return jnp.concatenate([ref[p, :] for p in pieces], axis=0)

    def store(refs, pieces, vals):
        for ref, val in zip(refs, vals):
            row = 0
            for p in pieces:
                ref[p, :] = val[row:row + p.size]
                row += p.size

    def body1(nb, carry):
        kb = jnp.maximum(nb - 1, 0)
        per_q, per_k = blk // QUAD, 2 * blk // QUAD
        qp = [pl.ds(pl.multiple_of(c * QUAD_ROWS + nb * per_q, per_q), per_q) for c in range(QUAD)]
        kp = [pl.ds(pl.multiple_of(c * QUAD_ROWS + kb * per_q, per_q), per_k) for c in range(QUAD)]
        res = attend(load(q4, qp), load(k4, kp), load(v4, kp), bias_ref[jnp.minimum(nb, 1)])
        store((o0, m0, d0), qp, res)
        return carry

    lax.fori_loop(0, SEQ // blk, body1, 0, unroll=ATTN_UNROLL)

    n_blk4 = QUAD_ROWS // blk

    def body4(idx, carry):
        nb = idx & (n_blk4 - 1)
        base = (idx - nb) * blk
        kb = jnp.maximum(nb - 1, 0)
        qp = [pl.ds(pl.multiple_of(base + nb * blk, blk), blk)]
        kp = [pl.ds(pl.multiple_of(base + kb * blk, blk), 2 * blk)]
        res = attend(load(q4, qp), load(k4, kp), load(v4, kp), bias_ref[2 + jnp.minimum(nb, 1)])
        store((o1, m1, d1), qp, res)
        return carry

    lax.fori_loop(0, SEQ // blk, body4, 0, unroll=ATTN_UNROLL)

    def body16(idx, carry):
        start = (idx & (QUAD - 1)) * QUAD_ROWS + lax.shift_right_logical(idx, 2)
        kp = [pl.ds(start, 2 * blk, stride=QUAD)]
        k2, v2 = load(k4, kp), load(v4, kp)
        for nb in range(2):
            qp = [pl.ds(start + nb * blk * QUAD, blk, stride=QUAD)]
            store((o2, m2, d2), qp, attend(load(q4, qp), k2, v2, bias_ref[2 + nb]))
        return carry

    lax.fori_loop(0, SEQ // (2 * blk), body16, 0, unroll=ATTN_UNROLL // 2)

    chunk = 512

    def mix(i, carry):
        r = pl.ds(pl.multiple_of(i * chunk, chunk), chunk)
        ma, mb, mc = m0[r, :], m1[r, :], m2[r, :]
        m = jnp.maximum(jnp.maximum(ma, mb), mc)
        wa, wb, wc = jnp.exp2(ma - m), jnp.exp2(mb - m), jnp.exp2(mc - m)
        acc = wa * o0[r, :] + wb * o1[r, :] + wc * o2[r, :]
        den = wa * d0[r, :] + wb * d1[r, :] + wc * d2[r, :]
        per_class = QUAD_ROWS // chunk
        c = i // per_class
        n0 = (i - c * per_class) * chunk
        out_ref[pl.ds(QUAD * n0 + c, chunk, stride=QUAD), :] = acc / den
        return carry

    lax.fori_loop(0, SEQ // chunk, mix, 0)


def _attn(q, k, v, batch):
    spec = pl.BlockSpec((None, SEQ, LANES), lambda b, hp: (b, 0, hp))
    bias = _attn_bias()
    return pl.pallas_call(
        _attn_kernel,
        grid=(batch, ATTN_WIDTH // LANES),
        in_specs=[spec, spec, spec, pl.BlockSpec(bias.shape, lambda b, hp: (0, 0, 0))],
        out_specs=spec,
        out_shape=jax.ShapeDtypeStruct((batch, SEQ, ATTN_WIDTH), F32),
        scratch_shapes=[pltpu.VMEM((SEQ, LANES), F32) for _ in range(12)],
        compiler_params=pltpu.CompilerParams(
            dimension_semantics=("arbitrary", "arbitrary"), vmem_limit_bytes=ATTN_VMEM_LIMIT),
        name="dilated_attn",
    )(q, k, v, jnp.asarray(bias))


def _pack_bf16_pairs(x):
    c = x.shape[1] // 2
    bits = pltpu.bitcast(x.astype(BF16).astype(F32), jnp.uint32)
    return lax.shift_right_logical(bits[:, :c], jnp.uint32(16)) | bits[:, c:]


def _unpack_bf16_pairs(w):
    lo = pltpu.bitcast(lax.shift_left(w, jnp.uint32(16)), F32)
    hi = pltpu.bitcast(w & jnp.uint32(0xFFFF0000), F32)
    return jnp.concatenate([lo, hi], axis=1).astype(BF16)


def _post_kernel(x_ref, attn_ref, u_ref, halo_ref, gate_ref, wao_ref, wmix_ref, pscale_ref,
                 wpo_ref, wo_ref, g2_ref, wr_ref, br_ref,
                 xo_ref, h2_ref, ri_ref, rf_ref, cnt_ref):
    @pl.when(pl.program_id(0) == 0)
    def _():
        cnt_ref[...] = jnp.zeros_like(cnt_ref)

    tile_pos = lax.rem(pl.program_id(0) * x_ref.shape[0], SEQ)
    for r0 in range(0, x_ref.shape[0], POST_SUBTILE):
        _post_subtile(r0, tile_pos, x_ref, attn_ref, u_ref, halo_ref, gate_ref, wao_ref, wmix_ref,
                      pscale_ref, wpo_ref, wo_ref, g2_ref, wr_ref, br_ref,
                      xo_ref, h2_ref, ri_ref, rf_ref, cnt_ref)


def _post_subtile(r0, tile_pos, x_ref, attn_ref, u_ref, halo_ref, gate_ref, wao_ref, wmix_ref,
                  pscale_ref, wpo_ref, wo_ref, g2_ref, wr_ref, br_ref,
                  xo_ref, h2_ref, ri_ref, rf_ref, cnt_ref):
    tm = POST_SUBTILE
    rows = slice(r0, r0 + tm)
    pos0 = tile_pos + r0
    pos = pos0 + lax.broadcasted_iota(jnp.int32, (tm, 1), 0)
    u = u_ref[rows, :]
    if r0 == 0:
        halo = halo_ref[...] * (pos0 > 0).astype(F32)
    else:
        halo = u_ref[r0 - POOL_HALO:r0, :]

    mixed = []
    for gi, w in enumerate(POOL_WINDOWS):
        lo = gi * POOL_GROUP_DIM
        ug = u[:, lo:lo + POOL_GROUP_DIM]
        ext = jnp.concatenate([halo[:, lo:lo + POOL_GROUP_DIM], ug], axis=0)
        shift = 1
        while shift < w:
            ext = ext + pltpu.roll(ext, shift, 0)
            shift *= 2
        cnt = jnp.minimum(pos + 1, w).astype(F32)
        pooled = ext[POOL_HALO:] / cnt - ug
        mixed.append(jnp.dot(pooled.astype(BF16), wmix_ref[gi], preferred_element_type=F32))
    pool_out = (jnp.concatenate(mixed, axis=-1) * pscale_ref[...]).astype(BF16)

    y_a = jnp.dot(attn_ref[rows, :].astype(BF16), wao_ref[...], preferred_element_type=F32)
    y_p = jnp.dot(pool_out, wpo_ref[...], preferred_element_type=F32)
    gates = gate_ref[rows, :]
    merged = gates[:, :D_MODEL].astype(F32) * y_a + gates[:, D_MODEL:].astype(F32) * y_p
    x_new = x_ref[rows, :] + jnp.dot(merged.astype(BF16), wo_ref[...], preferred_element_type=F32)
    xo_ref[rows, :] = x_new

    ms = jnp.mean(x_new * x_new, axis=-1, keepdims=True)
    h2 = x_new * lax.rsqrt(ms + RMS_EPS) * g2_ref[...]
    h_hi = h2.astype(BF16)
    h2_ref[rows, :] = _pack_bf16_pairs(h2)

    h_lo = (h2 - h_hi.astype(F32)).astype(BF16)
    nt = (((1,), (1,)), ((), ()))
    both = lax.dot_general(wr_ref[...], h_hi, nt, preferred_element_type=F32)
    cross = lax.dot_general(wr_ref[:ROUTER_ROWS, :], h_lo, nt, preferred_element_type=F32)
    logits = both[:ROUTER_ROWS] + both[ROUTER_ROWS:] + cross + jnp.concatenate(
        [br_ref[...]] * (tm // LANES), axis=1)

    sub = lax.broadcasted_iota(jnp.int32, (8, tm), 0)
    npg = EXPERTS_PER_GROUP

    def first_max(vals):
        vmax = jnp.max(vals, axis=0, keepdims=True)
        return vmax, jnp.min(jnp.where(vals == vmax, sub, npg), axis=0, keepdims=True)

    def of_group(parts, g_sel):
        out = parts[-1]
        for g in range(N_EXPERT_GROUPS - 2, -1, -1):
            out = jnp.where(g_sel == g, parts[g], out)
        return out

    glog = jnp.where(sub < N_EXPERT_GROUPS, logits[0:8], -jnp.inf)
    gmax, g_sel = first_max(glog)
    g_gate = 1.0 / jnp.sum(jnp.exp(glog - gmax), axis=0, keepdims=True)

    elog = of_group([logits[npg * (g + 1):npg * (g + 2)] for g in range(N_EXPERT_GROUPS)], g_sel)
    e_exp = jnp.exp(elog - jnp.max(elog, axis=0, keepdims=True))
    prob = e_exp / jnp.sum(e_exp, axis=0, keepdims=True)
    p1, i1 = first_max(prob)
    p2, i2 = first_max(jnp.where(sub == i1, -1.0, prob))
    scale = g_gate / (p1 + p2)

    chosen = (sub == i1) | (sub == i2)
    onehot = jnp.concatenate(
        [jnp.where(chosen & (g_sel == g), 1.0, 0.0) for g in range(N_EXPERT_GROUPS)],
        axis=0).astype(BF16)
    t_row = lax.broadcasted_iota(jnp.int32, (tm, tm), 0)
    t_col = lax.broadcasted_iota(jnp.int32, (tm, tm), 1)
    before = jnp.dot(onehot, jnp.where(t_row < t_col, 1.0, 0.0).astype(BF16),
                     preferred_element_type=F32)
    total = jnp.dot(onehot, jnp.ones((tm, LANES), BF16), preferred_element_type=F32)
    seen = cnt_ref[...]
    slot = before + jnp.concatenate([seen] * (tm // LANES), axis=1)
    cnt_ref[...] = seen + total
    slot = of_group([slot[npg * g:npg * (g + 1)] for g in range(N_EXPERT_GROUPS)], g_sel)
    r1 = jnp.sum(jnp.where(sub == i1, slot, 0.0), axis=0, keepdims=True).astype(jnp.int32)
    r2 = jnp.sum(jnp.where(sub == i2, slot, 0.0), axis=0, keepdims=True).astype(jnp.int32)

    e_base = g_sel * npg
    ri_ref[:, rows] = jnp.where(sub == 0, e_base + i1,
                            jnp.where(sub == 1, e_base + i2,
                                      jnp.where(sub == 2, r1, jnp.where(sub == 3, r2, 0))))
    rf_ref[:, rows] = jnp.where(sub == 0, p1 * scale, jnp.where(sub == 1, p2 * scale, 0.0))


ROUTER_ROWS = 8 + N_EXPERTS + 8


def _router_operands(w_group, b_group, w_expert, b_expert):
    d = w_group.shape[0]
    wt = jnp.zeros((ROUTER_ROWS, d), F32)
    wt = wt.at[:N_EXPERT_GROUPS].set(w_group.T)
    wt = wt.at[8:8 + N_EXPERTS].set(w_expert.transpose(0, 2, 1).reshape(N_EXPERTS, d))
    hi = wt.astype(BF16)
    lo = (wt - hi.astype(F32)).astype(BF16)
    b = jnp.zeros((ROUTER_ROWS,), F32).at[:N_EXPERT_GROUPS].set(b_group)
    b = b.at[8:8 + N_EXPERTS].set(b_expert.reshape(-1))
    return jnp.concatenate([hi, lo], axis=0), jnp.broadcast_to(b[:, None], (ROUTER_ROWS, LANES))


def _post(x, attn, u, gates, wao, wmix, pscale, wpo, wo, g2, wr, br):
    n = x.shape[0]
    tm = POST_TILE
    row = lambda i: (i, 0)
    const = lambda i: (0, 0)
    halo_blocks = tm // POOL_HALO
    return pl.pallas_call(
        _post_kernel,
        grid=(n // tm,),
        in_specs=[
            pl.BlockSpec((tm, D_MODEL), row),
            pl.BlockSpec((tm, ATTN_WIDTH), row),
            pl.BlockSpec((tm, POOL_WIDTH), row),
            pl.BlockSpec((POOL_HALO, POOL_WIDTH),
                         lambda i: (jnp.maximum(i * halo_blocks - 1, 0), 0)),
            pl.BlockSpec((tm, 2 * D_MODEL), row),
            pl.BlockSpec((ATTN_WIDTH, D_MODEL), const),
            pl.BlockSpec((len(POOL_WINDOWS), POOL_GROUP_DIM, POOL_GROUP_DIM), lambda i: (0, 0, 0)),
            pl.BlockSpec((1, POOL_WIDTH), const),
            pl.BlockSpec((POOL_WIDTH, D_MODEL), const),
            pl.BlockSpec((D_MODEL, D_MODEL), const),
            pl.BlockSpec((1, D_MODEL), const),
            pl.BlockSpec((2 * ROUTER_ROWS, D_MODEL), const),
            pl.BlockSpec((ROUTER_ROWS, LANES), const),
        ],
        out_specs=[
            pl.BlockSpec((tm, D_MODEL), row),
            pl.BlockSpec((tm, D_MODEL // 2), row),
            pl.BlockSpec((8, tm), lambda i: (0, i)),
            pl.BlockSpec((8, tm), lambda i: (0, i)),
            pl.BlockSpec((N_EXPERTS, LANES), const),
        ],
        out_shape=[
            jax.ShapeDtypeStruct((n, D_MODEL), F32),
            jax.ShapeDtypeStruct((n, D_MODEL // 2), jnp.uint32),
            jax.ShapeDtypeStruct((8, n), jnp.int32),
            jax.ShapeDtypeStruct((8, n), F32),
            jax.ShapeDtypeStruct((N_EXPERTS, LANES), F32),
        ],
        compiler_params=_params(1),
        name="post_attn_router",
    )(x, attn, u, u, gates, wao, wmix, pscale, wpo, wo, g2, wr, br)


ZERO_ROWS = 8
SLAB = D_MODEL // 2 // LANES


def _experts_kernel(layer, be_ref, nused_ref, tok_ref, plan_ref, h_hbm, wg_hbm, wu_hbm, wd_hbm,
                    y_ref, h_vmem, xs0, xs1, wg_f32, wu_f32, wd_f32, wg_bf, wu_bf, wd_bf,
                    h_sem, w_sems):
    step = pl.program_id(0)
    bm = xs0.shape[0]
    n_tok = h_hbm.shape[0]
    n_used = nused_ref[0]

    def weight_copies(e):
        return [pltpu.make_async_copy(src.at[layer, e], dst, w_sems.at[n])
                for n, (src, dst) in enumerate(
                    ((wg_hbm, wg_f32), (wu_hbm, wu_f32), (wd_hbm, wd_f32)))]

    @pl.when(step == 0)
    def _():
        @pl.when(n_used > 0)
        def _():
            for c in weight_copies(be_ref[0]):
                c.start(priority=1)
        copy = pltpu.make_async_copy(h_hbm, h_vmem.at[pl.ds(0, n_tok)], h_sem)
        copy.start()
        h_vmem[pl.ds(n_tok, ZERO_ROWS), :] = jnp.zeros((ZERO_ROWS, h_vmem.shape[1]), h_vmem.dtype)
        copy.wait()

        def gather(j, carry):
            xs0[pl.ds(j, 1), :] = h_vmem[pl.ds(tok_ref[j], 1), :]
            return carry
        lax.fori_loop(0, bm, gather, 0, unroll=8)

    def block(i, x_ref, x_next_ref, y_rows):
        used = i < n_used
        changed = (i == 0) | (be_ref[i] != be_ref[jnp.maximum(i - 1, 0)])

        @pl.when(changed & used)
        def _():
            for c in weight_copies(be_ref[i]):
                c.wait()
            wg_bf[...] = wg_f32[...].astype(BF16)
            wu_bf[...] = wu_f32[...].astype(BF16)
            wd_bf[...] = wd_f32[...].astype(BF16)

            @pl.when(plan_ref[i] != be_ref[i])
            def _():
                for n, c in enumerate(weight_copies(plan_ref[i])):
                    c.start(priority=n % 2)

        @pl.when(used)
        def _():
            nxt = jnp.where(i + 1 < n_used, i + 1, i) * bm
            for j in range(bm):
                x_next_ref[pl.ds(j, 1), :] = h_vmem[pl.ds(tok_ref[nxt + j], 1), :]

            x = _unpack_bf16_pairs(x_ref[...])
            a = jnp.dot(x, wg_bf[...], preferred_element_type=F32)
            b = jnp.dot(x, wu_bf[...], preferred_element_type=F32)
            mid = (a * jax.nn.sigmoid(a) * b).astype(BF16)
            y = _pack_bf16_pairs(jnp.dot(mid, wd_bf[...], preferred_element_type=F32))
            for c in range(SLAB):
                y_ref[pl.ds(y_rows + c, bm, stride=SLAB), :] = y[:, c * LANES:(c + 1) * LANES]

        @pl.when(jnp.logical_not(used))
        def _():
            y_ref[pl.ds(y_rows, bm * SLAB), :] = jnp.zeros((bm * SLAB, LANES), y_ref.dtype)

    block(2 * step, xs0, xs1, 0)
    block(2 * step + 1, xs1, xs0, bm * SLAB)


def _experts(block_expert, n_used, sorted_tok, next_expert, h2_packed, w_gate, w_up, w_down, layer):
    n_tok, width = h2_packed.shape
    bm = EXPERT_BLOCK
    any_space = pl.BlockSpec(memory_space=pl.ANY)
    n_blocks = sorted_tok.shape[0] // bm
    assert n_blocks % 2 == 0
    grid_spec = pltpu.PrefetchScalarGridSpec(
        num_scalar_prefetch=4,
        grid=(n_blocks // 2,),
        in_specs=[any_space, any_space, any_space, any_space],
        out_specs=pl.BlockSpec((2 * bm * SLAB, LANES), lambda i, *_: (i, 0)),
        scratch_shapes=[
            pltpu.VMEM((n_tok + ZERO_ROWS, width), h2_packed.dtype),
            pltpu.VMEM((bm, width), h2_packed.dtype),
            pltpu.VMEM((bm, width), h2_packed.dtype),
            pltpu.VMEM((D_MODEL, EXPERT_HIDDEN), F32),
            pltpu.VMEM((D_MODEL, EXPERT_HIDDEN), F32),
            pltpu.VMEM((EXPERT_HIDDEN, D_MODEL), F32),
            pltpu.VMEM((D_MODEL, EXPERT_HIDDEN), BF16),
            pltpu.VMEM((D_MODEL, EXPERT_HIDDEN), BF16),
            pltpu.VMEM((EXPERT_HIDDEN, D_MODEL), BF16),
            pltpu.SemaphoreType.DMA(()),
            pltpu.SemaphoreType.DMA((3,)),
        ],
    )
    return pl.pallas_call(
        functools.partial(_experts_kernel, layer),
        grid_spec=grid_spec,
        out_shape=jax.ShapeDtypeStruct((sorted_tok.shape[0] * SLAB, LANES), h2_packed.dtype),
        compiler_params=pltpu.CompilerParams(
            dimension_semantics=("arbitrary",), vmem_limit_bytes=EXPERTS_VMEM_LIMIT),
        name="experts",
    )(block_expert, n_used, sorted_tok, next_expert, h2_packed, w_gate, w_up, w_down)


def _sorted_tokens_kernel(dest_ref, seg_end_ref, tok_ref):
    n_tok = dest_ref.shape[0]
    bm = EXPERT_BLOCK
    buf_len = tok_ref.shape[0]

    def fill_from(start):
        def fill(p, carry):
            tok_ref[start + p] = n_tok
            return carry
        return fill

    for e in range(N_EXPERTS):
        lax.fori_loop(0, bm, fill_from(jnp.maximum(seg_end_ref[e] - bm, 0)), 0, unroll=16)
    lax.fori_loop(0, N_EXPERTS * bm, fill_from(buf_len - N_EXPERTS * bm), 0, unroll=16)

    def place(t, carry):
        both = dest_ref[t]
        tok_ref[both & 0xFFFF] = t
        tok_ref[lax.shift_right_logical(both, 16)] = t
        return carry
    lax.fori_loop(0, n_tok, place, 0, unroll=8)


def _sorted_tokens(dest, seg_end, buf_len):
    assert TOP_K == 2 and buf_len < (1 << 16)
    n_tok = dest.shape[0] // TOP_K
    packed = dest[:n_tok] | (dest[n_tok:] << 16)
    smem = pl.BlockSpec(memory_space=pltpu.SMEM)
    return pl.pallas_call(
        _sorted_tokens_kernel,
        in_specs=[smem, smem],
        out_specs=smem,
        out_shape=jax.ShapeDtypeStruct((buf_len,), jnp.int32),
        name="moe_sorted_tokens",
    )(packed, seg_end)


def _combine_kernel(dest_ref, x_ref, gate_ref, y_ref, out_ref, ya0, ya1, yb0, yb1, sems):
    tm = x_ref.shape[0]
    i = pl.program_id(0)
    n_steps = pl.num_programs(0)
    n_tok = n_steps * tm
    half = x_ref.shape[1] // 2
    pieces = half // LANES

    def row_copy(step, bufs, sem, r, k):
        d = pl.multiple_of(dest_ref[k * n_tok + step * tm + r], SLAB)
        dst_row = r * SLAB if isinstance(r, int) else pl.multiple_of(r * SLAB, SLAB)
        return pltpu.make_async_copy(y_ref.at[pl.ds(d, pieces)],
                                     bufs[k].at[pl.ds(dst_row, pieces)], sem)

    def drain(step, bufs, sem):
        def wait(r, carry):
            for k in range(TOP_K):
                row_copy(step, bufs, sem, r, k).wait()
            return carry
        lax.fori_loop(0, tm, wait, 0, unroll=8)

    @pl.when(i == 0)
    def _():
        def issue(r, carry):
            for k in range(TOP_K):
                row_copy(0, (ya0, ya1), sems.at[0], r, k).start()
            return carry
        lax.fori_loop(0, tm, issue, 0, unroll=8)

    def tile(bufs, sem, next_bufs, next_sem):
        @pl.when(i + 1 < n_steps)
        def _():
            for r in range(tm):
                for k in range(TOP_K):
                    row_copy(i + 1, next_bufs, next_sem, r, k).start(priority=(r + k) % 2)

        drain(i, bufs, sem)
        g = gate_ref[...]
        for c in range(pieces):
            w = [b[pl.ds(c, tm, stride=SLAB), :] for b in bufs]
            for base, unpack in ((0, lambda v: lax.shift_left(v, jnp.uint32(16))),
                                 (half, lambda v: v & jnp.uint32(0xFFFF0000))):
                cols = slice(base + c * LANES, base + (c + 1) * LANES)
                out_ref[:, cols] = (x_ref[:, cols]
                                    + g[:, 0:1] * pltpu.bitcast(unpack(w[0]), F32)
                                    + g[:, 1:2] * pltpu.bitcast(unpack(w[1]), F32))

    parity = i & 1
    pl.when(parity == 0)(lambda: tile((ya0, ya1), sems.at[0], (yb0, yb1), sems.at[1]))
    pl.when(parity == 1)(lambda: tile((yb0, yb1), sems.at[1], (ya0, ya1), sems.at[0]))


def _combine(dest_slab, x_new, gates, y_slabs):
    n, d = x_new.shape
    tm = ROW_TILE
    grid_spec = pltpu.PrefetchScalarGridSpec(
        num_scalar_prefetch=1,
        grid=(n // tm,),
        in_specs=[
            pl.BlockSpec((tm, d), lambda i, dest: (i, 0)),
            pl.BlockSpec((tm, LANES), lambda i, dest: (i, 0)),
            pl.BlockSpec(memory_space=pl.ANY),
        ],
        out_specs=pl.BlockSpec((tm, d), lambda i, dest: (i, 0)),
        scratch_shapes=[pltpu.VMEM((tm * SLAB, LANES), y_slabs.dtype) for _ in range(2 * TOP_K)]
        + [pltpu.SemaphoreType.DMA((2,))],
    )
    return pl.pallas_call(
        _combine_kernel,
        grid_spec=grid_spec,
        out_shape=jax.ShapeDtypeStruct((n, d), F32),
        compiler_params=_params(1),
        name="moe_combine",
    )(dest_slab, x_new, gates, y_slabs)


def _dispatch_plan(expert_id, slot, counts):
    bm = EXPERT_BLOCK
    n_assign = expert_id.size
    padded = (counts + bm - 1) // bm * bm
    pend = jnp.cumsum(padded)
    pstart = pend - padded
    experts = jnp.arange(N_EXPERTS, dtype=jnp.int32)
    seg_start = jnp.sum(jnp.where(expert_id[..., None] == experts, pstart, 0), axis=-1)
    dest = (seg_start + slot).reshape(-1).astype(jnp.int32)
    buf_len = n_assign + N_EXPERTS * bm
    block_start = jnp.arange(buf_len // bm, dtype=jnp.int32) * bm
    n_used = (pend[-1:] // bm).astype(jnp.int32)
    block_start = jnp.minimum(block_start, pend[-1] - bm)
    block_expert = jnp.sum((pend[None, :] <= block_start[:, None]).astype(jnp.int32), axis=1)
    block_expert = jnp.minimum(block_expert, N_EXPERTS - 1)
    later = jnp.where(block_expert[None, :] > block_expert[:, None], block_expert[None, :], N_EXPERTS)
    next_expert = jnp.min(later, axis=1)
    next_expert = jnp.where(next_expert == N_EXPERTS, block_expert, next_expert).astype(jnp.int32)
    return dest, block_expert, next_expert, n_used, pend.astype(jnp.int32), buf_len


def kernel(x, norm1_g, w_in, q_norm_g, k_norm_g, w_attn_out, w_pool_mix, pool_scale, w_pool_out,
           w_o, norm2_g, w_router_group, b_router_group, w_router_expert, b_router_expert,
           w_exp_gate, w_exp_up, w_exp_down):
    batch, seq, d = x.shape
    assert (seq, d) == (SEQ, D_MODEL)
    n_tok = batch * seq
    depth = w_in.shape[0]
    half = ATTN_WIDTH // 2
    hsum = (jnp.arange(half)[:, None] // HEAD_DIM == jnp.arange(half)[None, :] // HEAD_DIM).astype(BF16)

    xf = x.reshape(n_tok, d)
    for layer in range(depth):
        qg = jnp.tile(q_norm_g[layer], N_HEADS)[None, :]
        kg = jnp.tile(k_norm_g[layer], N_HEADS)[None, :]
        q, k, v, u, gates = _inproj(xf, norm1_g[layer][None, :], w_in[layer].astype(BF16), qg, kg, hsum)
        attn = _attn(q.reshape(batch, seq, ATTN_WIDTH), k.reshape(batch, seq, ATTN_WIDTH),
                     v.reshape(batch, seq, ATTN_WIDTH), batch).reshape(n_tok, ATTN_WIDTH)

        w_router, b_router = _router_operands(
            w_router_group[layer], b_router_group[layer], w_router_expert[layer], b_router_expert[layer])
        x_new, h2, route_i, route_f, counts = _post(
            xf, attn, u, gates, w_attn_out[layer].astype(BF16), w_pool_mix[layer].astype(BF16),
            pool_scale[layer][None, :], w_pool_out[layer].astype(BF16), w_o[layer].astype(BF16),
            norm2_g[layer][None, :], w_router, b_router)

        dest, block_expert, next_expert, n_used, seg_end, buf_len = _dispatch_plan(
            route_i[:TOP_K], route_i[TOP_K:2 * TOP_K], counts[:, 0].astype(jnp.int32))
        sorted_tok = _sorted_tokens(dest, seg_end, buf_len)
        y_rows = _experts(block_expert, n_used, sorted_tok, next_expert, h2,
                          w_exp_gate, w_exp_up, w_exp_down, layer)
        gate_rows = jnp.pad(route_f[:TOP_K].T, ((0, 0), (0, LANES - TOP_K)))
        xf = _combine(dest * SLAB, x_new, gate_rows, y_rows)
    return xf.reshape(batch, seq, d)
```

```python
import functools

import jax
import jax.numpy as jnp
import numpy as np
from jax import lax
from jax.experimental import pallas as pl
from jax.experimental.pallas import tpu as pltpu

D_MODEL = 1024
SEQ = 4096
N_HEADS = 8
HEAD_DIM = 64
ATTN_WIDTH = N_HEADS * HEAD_DIM
DILATIONS = (1, 4, 16)
ATTN_BLOCK = 128
POOL_WINDOWS = (2, 4, 8, 16)
POOL_GROUP_DIM = 128
POOL_WIDTH = len(POOL_WINDOWS) * POOL_GROUP_DIM
POOL_HALO = 16
IN_PROJ_WIDTH = 3 * ATTN_WIDTH + POOL_WIDTH + 2 * D_MODEL
N_EXPERT_GROUPS = 4
EXPERTS_PER_GROUP = 8
N_EXPERTS = N_EXPERT_GROUPS * EXPERTS_PER_GROUP
TOP_K = 2
EXPERT_HIDDEN = 512
RMS_EPS = 1e-6
NEG_INF = -1e30

LANES = 128
ROW_TILE = 512
INPROJ_TILE = 512
POST_TILE = 512
POST_SUBTILE = 512
EXPERT_BLOCK = 256
VMEM_LIMIT = 48 * 1024 * 1024
ATTN_VMEM_LIMIT = 56 * 1024 * 1024
EXPERTS_VMEM_LIMIT = 56 * 1024 * 1024

F32 = jnp.float32
BF16 = jnp.bfloat16


def _params(n_axes):
    return pltpu.CompilerParams(
        dimension_semantics=("arbitrary",) * n_axes, vmem_limit_bytes=VMEM_LIMIT)


def _inproj_kernel(x_ref, g_ref, w_ref, qg_ref, kg_ref, hsum_ref,
                   q_ref, k_ref, v_ref, u_ref, gate_ref):
    x = x_ref[...]
    ms = jnp.mean(x * x, axis=-1, keepdims=True)
    h = (x * lax.rsqrt(ms + RMS_EPS) * g_ref[...]).astype(BF16)

    def proj(lo, hi):
        return jnp.dot(h, w_ref[:, lo:hi], preferred_element_type=F32)

    def head_norm(t, gain):
        sq = (t * t).astype(BF16)
        half = ATTN_WIDTH // 2
        ssq = jnp.concatenate(
            [jnp.dot(sq[:, j * half:(j + 1) * half], hsum_ref[...], preferred_element_type=F32)
             for j in range(2)], axis=-1)
        return t * lax.rsqrt(ssq * (1.0 / HEAD_DIM) + RMS_EPS) * gain

    w = ATTN_WIDTH
    q_ref[...] = head_norm(proj(0, w), qg_ref[...])
    k_ref[...] = head_norm(proj(w, 2 * w), kg_ref[...])
    v_ref[...] = proj(2 * w, 3 * w)
    u_ref[...] = proj(3 * w, 3 * w + POOL_WIDTH)
    base = 3 * w + POOL_WIDTH
    for j in range(2 * D_MODEL // 512):
        gate_ref[:, j * 512:(j + 1) * 512] = jax.nn.sigmoid(
            proj(base + j * 512, base + (j + 1) * 512)).astype(BF16)


def _inproj(x, g1, w_in, qg, kg, hsum):
    n = x.shape[0]
    tm = INPROJ_TILE
    row = lambda i: (i, 0)
    const = lambda i: (0, 0)
    return pl.pallas_call(
        _inproj_kernel,
        grid=(n // tm,),
        in_specs=[
            pl.BlockSpec((tm, D_MODEL), row),
            pl.BlockSpec((1, D_MODEL), const),
            pl.BlockSpec((D_MODEL, IN_PROJ_WIDTH), const),
            pl.BlockSpec((1, ATTN_WIDTH), const),
            pl.BlockSpec((1, ATTN_WIDTH), const),
            pl.BlockSpec((ATTN_WIDTH // 2, ATTN_WIDTH // 2), const),
        ],
        out_specs=[
            pl.BlockSpec((tm, ATTN_WIDTH), row),
            pl.BlockSpec((tm, ATTN_WIDTH), row),
            pl.BlockSpec((tm, ATTN_WIDTH), row),
            pl.BlockSpec((tm, POOL_WIDTH), row),
            pl.BlockSpec((tm, 2 * D_MODEL), row),
        ],
        out_shape=[
            jax.ShapeDtypeStruct((n, ATTN_WIDTH), F32),
            jax.ShapeDtypeStruct((n, ATTN_WIDTH), F32),
            jax.ShapeDtypeStruct((n, ATTN_WIDTH), F32),
            jax.ShapeDtypeStruct((n, POOL_WIDTH), F32),
            jax.ShapeDtypeStruct((n, 2 * D_MODEL), BF16),
        ],
        compiler_params=_params(1),
        name="inproj",
    )(x, g1, w_in, qg, kg, hsum)


ATTN_UNROLL = 32
QUAD = 4
QUAD_ROWS = SEQ // QUAD
LOG2E = 1.4426950408889634


def _attn_bias():
    blk = ATTN_BLOCK
    r = np.arange(2 * blk) % blk
    c = np.arange(2 * blk)
    per_q, per_k = blk // QUAD, 2 * blk // QUAD
    tq = QUAD * (r % per_q) + r // per_q
    tk = QUAD * (c % per_k) + c // per_k
    masks = []
    for tq_, tk_ in ((tq, tk), (r, c)):
        first = tq_[:, None] - tk_[None, :]
        later = first + blk
        masks += [first >= 0, (later >= 0) & (later <= blk)]
    return np.where(np.stack(masks), 0.0, NEG_INF).astype(np.float32)


def _attn_kernel(q_ref, k_ref, v_ref, bias_ref, out_ref, q4, k4, v4,
                 o0, o1, o2, m0, m1, m2, d0, d1, d2):
    blk = ATTN_BLOCK
    head_a = lax.broadcasted_iota(jnp.int32, (blk, LANES), 1) < HEAD_DIM
    ones_cols = jnp.ones((2 * blk, LANES), BF16)

    for c in range(QUAD):
        dst = pl.ds(c * QUAD_ROWS, QUAD_ROWS)
        src = pl.ds(c, QUAD_ROWS, stride=QUAD)
        q4[dst, :] = q_ref[src, :] * (HEAD_DIM ** -0.5 * LOG2E)
        k4[dst, :] = k_ref[src, :]
        v4[dst, :] = v_ref[src, :]

    def attend(q2, k2, v2, bias):
        qs = jnp.concatenate(
            [jnp.where(head_a, q2, 0.0), jnp.where(head_a, 0.0, q2)], axis=0).astype(BF16)
        s = lax.dot_general(qs, k2.astype(BF16), (((1,), (1,)), ((), ())),
                            preferred_element_type=F32) + bias
        m = jnp.max(s, axis=-1, keepdims=True)
        e = jnp.exp2(s - m).astype(BF16)
        r = jnp.dot(e, jnp.concatenate([v2.astype(BF16), ones_cols], axis=1),
                    preferred_element_type=F32)
        mb = jnp.broadcast_to(m, (2 * blk, LANES))
        return (jnp.where(head_a, r[:blk, :LANES], r[blk:, :LANES]),
                jnp.where(head_a, mb[:blk], mb[blk:]),
                jnp.where(head_a, r[:blk, LANES:], r[blk:, LANES:]))

    def load(ref, pieces):
        return jnp.concatenate([ref[p, :] for p in pieces], axis=0)

    def store(refs, pieces, vals):
        for ref, val in zip(refs, vals):
            row = 0
            for p in pieces:
                ref[p, :] = val[row:row + p.size]
                row += p.size

    def body1(nb, carry):
        kb = jnp.maximum(nb - 1, 0)
        per_q, per_k = blk // QUAD, 2 * blk // QUAD
        qp = [pl.ds(pl.multiple_of(c * QUAD_ROWS + nb * per_q, per_q), per_q) for c in range(QUAD)]
        kp = [pl.ds(pl.multiple_of(c * QUAD_ROWS + kb * per_q, per_q), per_k) for c in range(QUAD)]
        res = attend(load(q4, qp), load(k4, kp), load(v4, kp), bias_ref[jnp.minimum(nb, 1)])
        store((o0, m0, d0), qp, res)
        return carry

    lax.fori_loop(0, SEQ // blk, body1, 0, unroll=ATTN_UNROLL)

    n_blk4 = QUAD_ROWS // blk

    def body4(idx, carry):
        nb = idx & (n_blk4 - 1)
        base = (idx - nb) * blk
        kb = jnp.maximum(nb - 1, 0)
        qp = [pl.ds(pl.multiple_of(base + nb * blk, blk), blk)]
        kp = [pl.ds(pl.multiple_of(base + kb * blk, blk), 2 * blk)]
        res = attend(load(q4, qp), load(k4, kp), load(v4, kp), bias_ref[2 + jnp.minimum(nb, 1)])
        store((o1, m1, d1), qp, res)
        return carry

    lax.fori_loop(0, SEQ // blk, body4, 0, unroll=ATTN_UNROLL)

    def body16(idx, carry):
        start = (idx & (QUAD - 1)) * QUAD_ROWS + lax.shift_right_logical(idx, 2)
        kp = [pl.ds(start, 2 * blk, stride=QUAD)]
        k2, v2 = load(k4, kp), load(v4, kp)
        for nb in range(2):
            qp = [pl.ds(start + nb * blk * QUAD, blk, stride=QUAD)]
            store((o2, m2, d2), qp, attend(load(q4, qp), k2, v2, bias_ref[2 + nb]))
        return carry

    lax.fori_loop(0, SEQ // (2 * blk), body16, 0, unroll=ATTN_UNROLL // 2)

    chunk = 512

    def mix(i, carry):
        r = pl.ds(pl.multiple_of(i * chunk, chunk), chunk)
        ma, mb, mc = m0[r, :], m1[r, :], m2[r, :]
        m = jnp.maximum(jnp.maximum(ma, mb), mc)
        wa, wb, wc = jnp.exp2(ma - m), jnp.exp2(mb - m), jnp.exp2(mc - m)
        acc = wa * o0[r, :] + wb * o1[r, :] + wc * o2[r, :]
        den = wa * d0[r, :] + wb * d1[r, :] + wc * d2[r, :]
        per_class = QUAD_ROWS // chunk
        c = i // per_class
        n0 = (i - c * per_class) * chunk
        out_ref[pl.ds(QUAD * n0 + c, chunk, stride=QUAD), :] = acc / den
        return carry

    lax.fori_loop(0, SEQ // chunk, mix, 0)


def _attn(q, k, v, batch):
    spec = pl.BlockSpec((None, SEQ, LANES), lambda b, hp: (b, 0, hp))
    bias = _attn_bias()
    return pl.pallas_call(
        _attn_kernel,
        grid=(batch, ATTN_WIDTH // LANES),
        in_specs=[spec, spec, spec, pl.BlockSpec(bias.shape, lambda b, hp: (0, 0, 0))],
        out_specs=spec,
        out_shape=jax.ShapeDtypeStruct((batch, SEQ, ATTN_WIDTH), F32),
        scratch_shapes=[pltpu.VMEM((SEQ, LANES), F32) for _ in range(12)],
        compiler_params=pltpu.CompilerParams(
            dimension_semantics=("arbitrary", "arbitrary"), vmem_limit_bytes=ATTN_VMEM_LIMIT),
        name="dilated_attn",
    )(q, k, v, jnp.asarray(bias))


def _pack_bf16_pairs(x):
    c = x.shape[1] // 2
    bits = pltpu.bitcast(x.astype(BF16).astype(F32), jnp.uint32)
    return lax.shift_right_logical(bits[:, :c], jnp.uint32(16)) | bits[:, c:]


def _unpack_bf16_pairs(w):
    lo = pltpu.bitcast(lax.shift_left(w, jnp.uint32(16)), F32)
    hi = pltpu.bitcast(w & jnp.uint32(0xFFFF0000), F32)
    return jnp.concatenate([lo, hi], axis=1).astype(BF16)


def _post_kernel(x_ref, attn_ref, u_ref, halo_ref, gate_ref, wao_ref, wmix_ref, pscale_ref,
                 wpo_ref, wo_ref, g2_ref, wr_ref, br_ref,
                 xo_ref, h2_ref, ri_ref, rf_ref, cnt_ref):
    @pl.when(pl.program_id(0) == 0)
    def _():
        cnt_ref[...] = jnp.zeros_like(cnt_ref)

    tile_pos = lax.rem(pl.program_id(0) * x_ref.shape[0], SEQ)
    for r0 in range(0, x_ref.shape[0], POST_SUBTILE):
        _post_subtile(r0, tile_pos, x_ref, attn_ref, u_ref, halo_ref, gate_ref, wao_ref, wmix_ref,
                      pscale_ref, wpo_ref, wo_ref, g2_ref, wr_ref, br_ref,
                      xo_ref, h2_ref, ri_ref, rf_ref, cnt_ref)


def _post_subtile(r0, tile_pos, x_ref, attn_ref, u_ref, halo_ref, gate_ref, wao_ref, wmix_ref,
                  pscale_ref, wpo_ref, wo_ref, g2_ref, wr_ref, br_ref,
                  xo_ref, h2_ref, ri_ref, rf_ref, cnt_ref):
    tm = POST_SUBTILE
    rows = slice(r0, r0 + tm)
    pos0 = tile_pos + r0
    pos = pos0 + lax.broadcasted_iota(jnp.int32, (tm, 1), 0)
    u = u_ref[rows, :]
    if r0 == 0:
        halo = halo_ref[...] * (pos0 > 0).astype(F32)
    else:
        halo = u_ref[r0 - POOL_HALO:r0, :]

    mixed = []
    for gi, w in enumerate(POOL_WINDOWS):
        lo = gi * POOL_GROUP_DIM
        ug = u[:, lo:lo + POOL_GROUP_DIM]
        ext = jnp.concatenate([halo[:, lo:lo + POOL_GROUP_DIM], ug], axis=0)
        shift = 1
        while shift < w:
            ext = ext + pltpu.roll(ext, shift, 0)
            shift *= 2
        cnt = jnp.minimum(pos + 1, w).astype(F32)
        pooled = ext[POOL_HALO:] / cnt - ug
        mixed.append(jnp.dot(pooled.astype(BF16), wmix_ref[gi], preferred_element_type=F32))
    pool_out = (jnp.concatenate(mixed, axis=-1) * pscale_ref[...]).astype(BF16)

    y_a = jnp.dot(attn_ref[rows, :].astype(BF16), wao_ref[...], preferred_element_type=F32)
    y_p = jnp.dot(pool_out, wpo_ref[...], preferred_element_type=F32)
    gates = gate_ref[rows, :]
    merged = gates[:, :D_MODEL].astype(F32) * y_a + gates[:, D_MODEL:].astype(F32) * y_p
    x_new = x_ref[rows, :] + jnp.dot(merged.astype(BF16), wo_ref[...], preferred_element_type=F32)
    xo_ref[rows, :] = x_new

    ms = jnp.mean(x_new * x_new, axis=-1, keepdims=True)
    h2 = x_new * lax.rsqrt(ms + RMS_EPS) * g2_ref[...]
    h_hi = h2.astype(BF16)
    h2_ref[rows, :] = _pack_bf16_pairs(h2)

    h_lo = (h2 - h_hi.astype(F32)).astype(BF16)
    nt = (((1,), (1,)), ((), ()))
    both = lax.dot_general(wr_ref[...], h_hi, nt, preferred_element_type=F32)
    cross = lax.dot_general(wr_ref[:ROUTER_ROWS, :], h_lo, nt, preferred_element_type=F32)
    logits = both[:ROUTER_ROWS] + both[ROUTER_ROWS:] + cross + jnp.concatenate(
        [br_ref[...]] * (tm // LANES), axis=1)

    sub = lax.broadcasted_iota(jnp.int32, (8, tm), 0)
    npg = EXPERTS_PER_GROUP

    def first_max(vals):
        vmax = jnp.max(vals, axis=0, keepdims=True)
        return vmax, jnp.min(jnp.where(vals == vmax, sub, npg), axis=0, keepdims=True)

    def of_group(parts, g_sel):
        out = parts[-1]
        for g in range(N_EXPERT_GROUPS - 2, -1, -1):
            out = jnp.where(g_sel == g, parts[g], out)
        return out

    glog = jnp.where(sub < N_EXPERT_GROUPS, logits[0:8], -jnp.inf)
    gmax, g_sel = first_max(glog)
    g_gate = 1.0 / jnp.sum(jnp.exp(glog - gmax), axis=0, keepdims=True)

    elog = of_group([logits[npg * (g + 1):npg * (g + 2)] for g in range(N_EXPERT_GROUPS)], g_sel)
    e_exp = jnp.exp(elog - jnp.max(elog, axis=0, keepdims=True))
    prob = e_exp / jnp.sum(e_exp, axis=0, keepdims=True)
    p1, i1 = first_max(prob)
    p2, i2 = first_max(jnp.where(sub == i1, -1.0, prob))
    scale = g_gate / (p1 + p2)

    chosen = (sub == i1) | (sub == i2)
    onehot = jnp.concatenate(
        [jnp.where(chosen & (g_sel == g), 1.0, 0.0) for g in range(N_EXPERT_GROUPS)],
        axis=0).astype(BF16)
    t_row = lax.broadcasted_iota(jnp.int32, (tm, tm), 0)
    t_col = lax.broadcasted_iota(jnp.int32, (tm, tm), 1)
    before = jnp.dot(onehot, jnp.where(t_row < t_col, 1.0, 0.0).astype(BF16),
                     preferred_element_type=F32)
    total = jnp.dot(onehot, jnp.ones((tm, LANES), BF16), preferred_element_type=F32)
    seen = cnt_ref[...]
    slot = before + jnp.concatenate([seen] * (tm // LANES), axis=1)
    cnt_ref[...] = seen + total
    slot = of_group([slot[npg * g:npg * (g + 1)] for g in range(N_EXPERT_GROUPS)], g_sel)
    r1 = jnp.sum(jnp.where(sub == i1, slot, 0.0), axis=0, keepdims=True).astype(jnp.int32)
    r2 = jnp.sum(jnp.where(sub == i2, slot, 0.0), axis=0, keepdims=True).astype(jnp.int32)

    e_base = g_sel * npg
    ri_ref[:, rows] = jnp.where(sub == 0, e_base + i1,
                            jnp.where(sub == 1, e_base + i2,
                                      jnp.where(sub == 2, r1, jnp.where(sub == 3, r2, 0))))
    rf_ref[:, rows] = jnp.where(sub == 0, p1 * scale, jnp.where(sub == 1, p2 * scale, 0.0))


ROUTER_ROWS = 8 + N_EXPERTS + 8


def _router_operands(w_group, b_group, w_expert, b_expert):
    d = w_group.shape[0]
    wt = jnp.zeros((ROUTER_ROWS, d), F32)
    wt = wt.at[:N_EXPERT_GROUPS].set(w_group.T)
    wt = wt.at[8:8 + N_EXPERTS].set(w_expert.transpose(0, 2, 1).reshape(N_EXPERTS, d))
    hi = wt.astype(BF16)
    lo = (wt - hi.astype(F32)).astype(BF16)
    b = jnp.zeros((ROUTER_ROWS,), F32).at[:N_EXPERT_GROUPS].set(b_group)
    b = b.at[8:8 + N_EXPERTS].set(b_expert.reshape(-1))
    return jnp.concatenate([hi, lo], axis=0), jnp.broadcast_to(b[:, None], (ROUTER_ROWS, LANES))


def _post(x, attn, u, gates, wao, wmix, pscale, wpo, wo, g2, wr, br):
    n = x.shape[0]
    tm = POST_TILE
    row = lambda i: (i, 0)
    const = lambda i: (0, 0)
    halo_blocks = tm // POOL_HALO
    return pl.pallas_call(
        _post_kernel,
        grid=(n // tm,),
        in_specs=[
            pl.BlockSpec((tm, D_MODEL), row),
            pl.BlockSpec((tm, ATTN_WIDTH), row),
            pl.BlockSpec((tm, POOL_WIDTH), row),
            pl.BlockSpec((POOL_HALO, POOL_WIDTH),
                         lambda i: (jnp.maximum(i * halo_blocks - 1, 0), 0)),
            pl.BlockSpec((tm, 2 * D_MODEL), row),
            pl.BlockSpec((ATTN_WIDTH, D_MODEL), const),
            pl.BlockSpec((len(POOL_WINDOWS), POOL_GROUP_DIM, POOL_GROUP_DIM), lambda i: (0, 0, 0)),
            pl.BlockSpec((1, POOL_WIDTH), const),
            pl.BlockSpec((POOL_WIDTH, D_MODEL), const),
            pl.BlockSpec((D_MODEL, D_MODEL), const),
            pl.BlockSpec((1, D_MODEL), const),
            pl.BlockSpec((2 * ROUTER_ROWS, D_MODEL), const),
            pl.BlockSpec((ROUTER_ROWS, LANES), const),
        ],
        out_specs=[
            pl.BlockSpec((tm, D_MODEL), row),
            pl.BlockSpec((tm, D_MODEL // 2), row),
            pl.BlockSpec((8, tm), lambda i: (0, i)),
            pl.BlockSpec((8, tm), lambda i: (0, i)),
            pl.BlockSpec((N_EXPERTS, LANES), const),
        ],
        out_shape=[
            jax.ShapeDtypeStruct((n, D_MODEL), F32),
            jax.ShapeDtypeStruct((n, D_MODEL // 2), jnp.uint32),
            jax.ShapeDtypeStruct((8, n), jnp.int32),
            jax.ShapeDtypeStruct((8, n), F32),
            jax.ShapeDtypeStruct((N_EXPERTS, LANES), F32),
        ],
        compiler_params=_params(1),
        name="post_attn_router",
    )(x, attn, u, u, gates, wao, wmix, pscale, wpo, wo, g2, wr, br)


ZERO_ROWS = 8
SLAB = D_MODEL // 2 // LANES


def _experts_kernel(layer, be_ref, nused_ref, dest_ref, plan_ref, seg_end_ref,
                    h_hbm, wg_hbm, wu_hbm, wd_hbm,
                    y_ref, h_vmem, xs0, xs1, wg_f32, wu_f32, wd_f32, wg_bf, wu_bf, wd_bf,
                    h_sem, w_sems, tok_ref):
    step = pl.program_id(0)
    bm = xs0.shape[0]
    n_tok = h_hbm.shape[0]
    n_used = nused_ref[0]

    def weight_copies(e):
        return [pltpu.make_async_copy(src.at[layer, e], dst, w_sems.at[n])
                for n, (src, dst) in enumerate(
                    ((wg_hbm, wg_f32), (wu_hbm, wu_f32), (wd_hbm, wd_f32)))]

    @pl.when(step == 0)
    def _():
        @pl.when(n_used > 0)
        def _():
            for c in weight_copies(be_ref[0]):
                c.start(priority=1)
        copy = pltpu.make_async_copy(h_hbm, h_vmem.at[pl.ds(0, n_tok)], h_sem)
        copy.start()
        h_vmem[pl.ds(n_tok, ZERO_ROWS), :] = jnp.zeros((ZERO_ROWS, h_vmem.shape[1]), h_vmem.dtype)
        _sorted_tokens_kernel(dest_ref, seg_end_ref, tok_ref)
        copy.wait()

        def gather(j, carry):
            xs0[pl.ds(j, 1), :] = h_vmem[pl.ds(tok_ref[j], 1), :]
            return carry
        lax.fori_loop(0, bm, gather, 0, unroll=8)

    def block(i, x_ref, x_next_ref, y_rows):
        used = i < n_used
        changed = (i == 0) | (be_ref[i] != be_ref[jnp.maximum(i - 1, 0)])

        @pl.when(changed & used)
        def _():
            for c in weight_copies(be_ref[i]):
                c.wait()
            wg_bf[...] = wg_f32[...].astype(BF16)
            wu_bf[...] = wu_f32[...].astype(BF16)
            wd_bf[...] = wd_f32[...].astype(BF16)

            @pl.when(plan_ref[i] != be_ref[i])
            def _():
                for n, c in enumerate(weight_copies(plan_ref[i])):
                    c.start(priority=n % 2)

        @pl.when(used)
        def _():
            nxt = jnp.where(i + 1 < n_used, i + 1, i) * bm
            for j in range(bm):
                x_next_ref[pl.ds(j, 1), :] = h_vmem[pl.ds(tok_ref[nxt + j], 1), :]

            x = _unpack_bf16_pairs(x_ref[...])
            a = jnp.dot(x, wg_bf[...], preferred_element_type=F32)
            b = jnp.dot(x, wu_bf[...], preferred_element_type=F32)
            mid = (a * jax.nn.sigmoid(a) * b).astype(BF16)
            y = _pack_bf16_pairs(jnp.dot(mid, wd_bf[...], preferred_element_type=F32))
            for c in range(SLAB):
                y_ref[pl.ds(y_rows + c, bm, stride=SLAB), :] = y[:, c * LANES:(c + 1) * LANES]

        @pl.when(jnp.logical_not(used))
        def _():
            y_ref[pl.ds(y_rows, bm * SLAB), :] = jnp.zeros((bm * SLAB, LANES), y_ref.dtype)

    block(2 * step, xs0, xs1, 0)
    block(2 * step + 1, xs1, xs0, bm * SLAB)


def _experts(block_expert, n_used, dest, next_expert, seg_end, buf_len, h2_packed,
             w_gate, w_up, w_down, layer):
    assert TOP_K == 2 and buf_len < (1 << 16)
    n_tok, width = h2_packed.shape
    packed_dest = dest[:n_tok] | (dest[n_tok:] << 16)
    bm = EXPERT_BLOCK
    any_space = pl.BlockSpec(memory_space=pl.ANY)
    n_blocks = buf_len // bm
    assert n_blocks % 2 == 0
    grid_spec = pltpu.PrefetchScalarGridSpec(
        num_scalar_prefetch=5,
        grid=(n_blocks // 2,),
        in_specs=[any_space, any_space, any_space, any_space],
        out_specs=pl.BlockSpec((2 * bm * SLAB, LANES), lambda i, *_: (i, 0)),
        scratch_shapes=[
            pltpu.VMEM((n_tok + ZERO_ROWS, width), h2_packed.dtype),
            pltpu.VMEM((bm, width), h2_packed.dtype),
            pltpu.VMEM((bm, width), h2_packed.dtype),
            pltpu.VMEM((D_MODEL, EXPERT_HIDDEN), F32),
            pltpu.VMEM((D_MODEL, EXPERT_HIDDEN), F32),
            pltpu.VMEM((EXPERT_HIDDEN, D_MODEL), F32),
            pltpu.VMEM((D_MODEL, EXPERT_HIDDEN), BF16),
            pltpu.VMEM((D_MODEL, EXPERT_HIDDEN), BF16),
            pltpu.VMEM((EXPERT_HIDDEN, D_MODEL), BF16),
            pltpu.SemaphoreType.DMA(()),
            pltpu.SemaphoreType.DMA((3,)),
            pltpu.SMEM((buf_len,), jnp.int32),
        ],
    )
    return pl.pallas_call(
        functools.partial(_experts_kernel, layer),
        grid_spec=grid_spec,
        out_shape=jax.ShapeDtypeStruct((buf_len * SLAB, LANES), h2_packed.dtype),
        compiler_params=pltpu.CompilerParams(
            dimension_semantics=("arbitrary",), vmem_limit_bytes=EXPERTS_VMEM_LIMIT),
        name="experts",
    )(block_expert, n_used, packed_dest, next_expert, seg_end, h2_packed, w_gate, w_up, w_down)


def _sorted_tokens_kernel(dest_ref, seg_end_ref, tok_ref):
    n_tok = dest_ref.shape[0]
    bm = EXPERT_BLOCK
    buf_len = tok_ref.shape[0]

    def fill_from(start):
        def fill(p, carry):
            tok_ref[start + p] = n_tok
            return carry
        return fill

    for e in range(N_EXPERTS):
        lax.fori_loop(0, bm, fill_from(jnp.maximum(seg_end_ref[e] - bm, 0)), 0, unroll=16)
    lax.fori_loop(0, N_EXPERTS * bm, fill_from(buf_len - N_EXPERTS * bm), 0, unroll=16)

    def place(t, carry):
        both = dest_ref[t]
        tok_ref[both & 0xFFFF] = t
        tok_ref[lax.shift_right_logical(both, 16)] = t
        return carry
    lax.fori_loop(0, n_tok, place, 0, unroll=8)


def _combine_kernel(dest_ref, x_ref, gate_ref, y_ref, out_ref, ya0, ya1, yb0, yb1, sems):
    tm = x_ref.shape[0]
    i = pl.program_id(0)
    n_steps = pl.num_programs(0)
    n_tok = n_steps * tm
    half = x_ref.shape[1] // 2
    pieces = half // LANES

    def row_copy(step, bufs, sem, r, k):
        d = pl.multiple_of(dest_ref[k * n_tok + step * tm + r], SLAB)
        dst_row = r * SLAB if isinstance(r, int) else pl.multiple_of(r * SLAB, SLAB)
        return pltpu.make_async_copy(y_ref.at[pl.ds(d, pieces)],
                                     bufs[k].at[pl.ds(dst_row, pieces)], sem)

    def drain(step, bufs, sem):
        def wait(r, carry):
            for k in range(TOP_K):
                row_copy(step, bufs, sem, r, k).wait()
            return carry
        lax.fori_loop(0, tm, wait, 0, unroll=8)

    @pl.when(i == 0)
    def _():
        def issue(r, carry):
            for k in range(TOP_K):
                row_copy(0, (ya0, ya1), sems.at[0], r, k).start()
            return carry
        lax.fori_loop(0, tm, issue, 0, unroll=8)

    def tile(bufs, sem, next_bufs, next_sem):
        @pl.when(i + 1 < n_steps)
        def _():
            for r in range(tm):
                for k in range(TOP_K):
                    row_copy(i + 1, next_bufs, next_sem, r, k).start(priority=(r + k) % 2)

        drain(i, bufs, sem)
        g = gate_ref[...]
        for c in range(pieces):
            w = [b[pl.ds(c, tm, stride=SLAB), :] for b in bufs]
            for base, unpack in ((0, lambda v: lax.shift_left(v, jnp.uint32(16))),
                                 (half, lambda v: v & jnp.uint32(0xFFFF0000))):
                cols = slice(base + c * LANES, base + (c + 1) * LANES)
                out_ref[:, cols] = (x_ref[:, cols]
                                    + g[:, 0:1] * pltpu.bitcast(unpack(w[0]), F32)
                                    + g[:, 1:2] * pltpu.bitcast(unpack(w[1]), F32))

    parity = i & 1
    pl.when(parity == 0)(lambda: tile((ya0, ya1), sems.at[0], (yb0, yb1), sems.at[1]))
    pl.when(parity == 1)(lambda: tile((yb0, yb1), sems.at[1], (ya0, ya1), sems.at[0]))


def _combine(dest_slab, x_new, gates, y_slabs):
    n, d = x_new.shape
    tm = ROW_TILE
    grid_spec = pltpu.PrefetchScalarGridSpec(
        num_scalar_prefetch=1,
        grid=(n // tm,),
        in_specs=[
            pl.BlockSpec((tm, d), lambda i, dest: (i, 0)),
            pl.BlockSpec((tm, LANES), lambda i, dest: (i, 0)),
            pl.BlockSpec(memory_space=pl.ANY),
        ],
        out_specs=pl.BlockSpec((tm, d), lambda i, dest: (i, 0)),
        scratch_shapes=[pltpu.VMEM((tm * SLAB, LANES), y_slabs.dtype) for _ in range(2 * TOP_K)]
        + [pltpu.SemaphoreType.DMA((2,))],
    )
    return pl.pallas_call(
        _combine_kernel,
        grid_spec=grid_spec,
        out_shape=jax.ShapeDtypeStruct((n, d), F32),
        compiler_params=_params(1),
        name="moe_combine",
    )(dest_slab, x_new, gates, y_slabs)


def _dispatch_plan(expert_id, slot, counts):
    bm = EXPERT_BLOCK
    n_assign = expert_id.size
    padded = (counts + bm - 1) // bm * bm
    pend = jnp.cumsum(padded)
    pstart = pend - padded
    experts = jnp.arange(N_EXPERTS, dtype=jnp.int32)
    seg_start = jnp.sum(jnp.where(expert_id[..., None] == experts, pstart, 0), axis=-1)
    dest = (seg_start + slot).reshape(-1).astype(jnp.int32)
    buf_len = n_assign + N_EXPERTS * bm
    block_start = jnp.arange(buf_len // bm, dtype=jnp.int32) * bm
    n_used = (pend[-1:] // bm).astype(jnp.int32)
    block_start = jnp.minimum(block_start, pend[-1] - bm)
    block_expert = jnp.sum((pend[None, :] <= block_start[:, None]).astype(jnp.int32), axis=1)
    block_expert = jnp.minimum(block_expert, N_EXPERTS - 1)
    later = jnp.where(block_expert[None, :] > block_expert[:, None], block_expert[None, :], N_EXPERTS)
    next_expert = jnp.min(later, axis=1)
    next_expert = jnp.where(next_expert == N_EXPERTS, block_expert, next_expert).astype(jnp.int32)
    return dest, block_expert, next_expert, n_used, pend.astype(jnp.int32), buf_len


def kernel(x, norm1_g, w_in, q_norm_g, k_norm_g, w_attn_out, w_pool_mix, pool_scale, w_pool_out,
           w_o, norm2_g, w_router_group, b_router_group, w_router_expert, b_router_expert,
           w_exp_gate, w_exp_up, w_exp_down):
    batch, seq, d = x.shape
    assert (seq, d) == (SEQ, D_MODEL)
    n_tok = batch * seq
    depth = w_in.shape[0]
    half = ATTN_WIDTH // 2
    hsum = (jnp.arange(half)[:, None] // HEAD_DIM == jnp.arange(half)[None, :] // HEAD_DIM).astype(BF16)

    xf = x.reshape(n_tok, d)
    for layer in range(depth):
        qg = jnp.tile(q_norm_g[layer], N_HEADS)[None, :]
        kg = jnp.tile(k_norm_g[layer], N_HEADS)[None, :]
        q, k, v, u, gates = _inproj(xf, norm1_g[layer][None, :], w_in[layer].astype(BF16), qg, kg, hsum)
        attn = _attn(q.reshape(batch, seq, ATTN_WIDTH), k.reshape(batch, seq, ATTN_WIDTH),
                     v.reshape(batch, seq, ATTN_WIDTH), batch).reshape(n_tok, ATTN_WIDTH)

        w_router, b_router = _router_operands(
            w_router_group[layer], b_router_group[layer], w_router_expert[layer], b_router_expert[layer])
        x_new, h2, route_i, route_f, counts = _post(
            xf, attn, u, gates, w_attn_out[layer].astype(BF16), w_pool_mix[layer].astype(BF16),
            pool_scale[layer][None, :], w_pool_out[layer].astype(BF16), w_o[layer].astype(BF16),
            norm2_g[layer][None, :], w_router, b_router)

        dest, block_expert, next_expert, n_used, seg_end, buf_len = _dispatch_plan(
            route_i[:TOP_K], route_i[TOP_K:2 * TOP_K], counts[:, 0].astype(jnp.int32))
        y_rows = _experts(block_expert, n_used, dest, next_expert, seg_end, buf_len, h2,
                          w_exp_gate, w_exp_up, w_exp_down, layer)
        gate_rows = jnp.pad(route_f[:TOP_K].T, ((0, 0), (0, LANES - TOP_K)))
        xf = _combine(dest * SLAB, x_new, gate_rows, y_rows)
    return xf.reshape(batch, seq, d)
```

```python
import functools

import jax
import jax.numpy as jnp
import numpy as np
from jax import lax
from jax.experimental import pallas as pl
from jax.experimental.pallas import tpu as pltpu

D_MODEL = 1024
SEQ = 4096
N_HEADS = 8
HEAD_DIM = 64
ATTN_WIDTH = N_HEADS * HEAD_DIM
DILATIONS = (1, 4, 16)
ATTN_BLOCK = 128
POOL_WINDOWS = (2, 4, 8, 16)
POOL_GROUP_DIM = 128
POOL_WIDTH = len(POOL_WINDOWS) * POOL_GROUP_DIM
POOL_HALO = 16
IN_PROJ_WIDTH = 3 * ATTN_WIDTH + POOL_WIDTH + 2 * D_MODEL
N_EXPERT_GROUPS = 4
EXPERTS_PER_GROUP = 8
N_EXPERTS = N_EXPERT_GROUPS * EXPERTS_PER_GROUP
TOP_K = 2
EXPERT_HIDDEN = 512
RMS_EPS = 1e-6
NEG_INF = -1e30

LANES = 128
ROW_TILE = 512
INPROJ_TILE = 512
POST_TILE = 512
POST_SUBTILE = 512
EXPERT_BLOCK = 256
VMEM_LIMIT = 48 * 1024 * 1024
ATTN_VMEM_LIMIT = 56 * 1024 * 1024
EXPERTS_VMEM_LIMIT = 56 * 1024 * 1024

F32 = jnp.float32
BF16 = jnp.bfloat16


def _params(n_axes):
    return pltpu.CompilerParams(
        dimension_semantics=("arbitrary",) * n_axes, vmem_limit_bytes=VMEM_LIMIT)


def _inproj_kernel(x_ref, g_ref, w_ref, qg_ref, kg_ref, hsum_ref,
                   q_ref, k_ref, v_ref, u_ref, gate_ref):
    x = x_ref[...]
    ms = jnp.mean(x * x, axis=-1, keepdims=True)
    h = (x * lax.rsqrt(ms + RMS_EPS) * g_ref[...]).astype(BF16)

    def proj(lo, hi):
        return jnp.dot(h, w_ref[:, lo:hi], preferred_element_type=F32)

    def head_norm(t, gain):
        sq = (t * t).astype(BF16)
        half = ATTN_WIDTH // 2
        ssq = jnp.concatenate(
            [jnp.dot(sq[:, j * half:(j + 1) * half], hsum_ref[...], preferred_element_type=F32)
             for j in range(2)], axis=-1)
        return t * lax.rsqrt(ssq * (1.0 / HEAD_DIM) + RMS_EPS) * gain

    w = ATTN_WIDTH
    q_ref[...] = head_norm(proj(0, w), qg_ref[...])
    k_ref[...] = head_norm(proj(w, 2 * w), kg_ref[...])
    v_ref[...] = proj(2 * w, 3 * w)
    u_ref[...] = proj(3 * w, 3 * w + POOL_WIDTH)
    base = 3 * w + POOL_WIDTH
    for j in range(2 * D_MODEL // 512):
        gate_ref[:, j * 512:(j + 1) * 512] = jax.nn.sigmoid(
            proj(base + j * 512, base + (j + 1) * 512)).astype(BF16)


def _inproj(x, g1, w_in, qg, kg, hsum):
    n = x.shape[0]
    tm = INPROJ_TILE
    row = lambda i: (i, 0)
    const = lambda i: (0, 0)
    return pl.pallas_call(
        _inproj_kernel,
        grid=(n // tm,),
        in_specs=[
            pl.BlockSpec((tm, D_MODEL), row),
            pl.BlockSpec((1, D_MODEL), const),
            pl.BlockSpec((D_MODEL, IN_PROJ_WIDTH), const),
            pl.BlockSpec((1, ATTN_WIDTH), const),
            pl.BlockSpec((1, ATTN_WIDTH), const),
            pl.BlockSpec((ATTN_WIDTH // 2, ATTN_WIDTH // 2), const),
        ],
        out_specs=[
            pl.BlockSpec((tm, ATTN_WIDTH), row),
            pl.BlockSpec((tm, ATTN_WIDTH), row),
            pl.BlockSpec((tm, ATTN_WIDTH), row),
            pl.BlockSpec((tm, POOL_WIDTH), row),
            pl.BlockSpec((tm, 2 * D_MODEL), row),
        ],
        out_shape=[
            jax.ShapeDtypeStruct((n, ATTN_WIDTH), F32),
            jax.ShapeDtypeStruct((n, ATTN_WIDTH), F32),
            jax.ShapeDtypeStruct((n, ATTN_WIDTH), F32),
            jax.ShapeDtypeStruct((n, POOL_WIDTH), F32),
            jax.ShapeDtypeStruct((n, 2 * D_MODEL), BF16),
        ],
        compiler_params=_params(1),
        name="inproj",
    )(x, g1, w_in, qg, kg, hsum)


ATTN_UNROLL = 32
QUAD = 4
QUAD_ROWS = SEQ // QUAD
LOG2E = 1.4426950408889634


def _attn_bias():
    blk = ATTN_BLOCK
    r = np.arange(2 * blk) % blk
    c = np.arange(2 * blk)
    per_q, per_k = blk // QUAD, 2 * blk // QUAD
    tq = QUAD * (r % per_q) + r // per_q
    tk = QUAD * (c % per_k) + c // per_k
    masks = []
    for tq_, tk_ in ((tq, tk), (r, c)):
        first = tq_[:, None] - tk_[None, :]
        later = first + blk
        masks += [first >= 0, (later >= 0) & (later <= blk)]
    return np.where(np.stack(masks), 0.0, NEG_INF).astype(np.float32)


def _attn_kernel(q_ref, k_ref, v_ref, bias_ref, out_ref, q4, k4, v4,
                 o0, o1, o2, m0, m1, m2, d0, d1, d2):
    blk = ATTN_BLOCK
    head_a = lax.broadcasted_iota(jnp.int32, (blk, LANES), 1) < HEAD_DIM
    ones_cols = jnp.ones((2 * blk, LANES), BF16)

    for c in range(QUAD):
        dst = pl.ds(c * QUAD_ROWS, QUAD_ROWS)
        src = pl.ds(c, QUAD_ROWS, stride=QUAD)
        q4[dst, :] = q_ref[src, :] * (HEAD_DIM ** -0.5 * LOG2E)
        k4[dst, :] = k_ref[src, :]
        v4[dst, :] = v_ref[src, :]

    def attend(q2, k2, v2, bias):
        qs = jnp.concatenate(
            [jnp.where(head_a, q2, 0.0), jnp.where(head_a, 0.0, q2)], axis=0).astype(BF16)
        s = lax.dot_general(qs, k2.astype(BF16), (((1,), (1,)), ((), ())),
                            preferred_element_type=F32) + bias
        m = jnp.max(s, axis=-1, keepdims=True)
        e = jnp.exp2(s - m).astype(BF16)
        r = jnp.dot(e, jnp.concatenate([v2.astype(BF16), ones_cols], axis=1),
                    preferred_element_type=F32)
        mb = jnp.broadcast_to(m, (2 * blk, LANES))
        return (jnp.where(head_a, r[:blk, :LANES], r[blk:, :LANES]),
                jnp.where(head_a, mb[:blk], mb[blk:]),
                jnp.where(head_a, r[:blk, LANES:], r[blk:, LANES:]))

    def load(ref, pieces):
        return jnp.concatenate([ref[p, :] for p in pieces], axis=0)

    def store(refs, pieces, vals):
        for ref, val in zip(refs, vals):
            row = 0
            for p in pieces:
                ref[p, :] = val[row:row + p.size]
                row += p.size

    def body1(nb, carry):
        kb = jnp.maximum(nb - 1, 0)
        per_q, per_k = blk // QUAD, 2 * blk // QUAD
        qp = [pl.ds(pl.multiple_of(c * QUAD_ROWS + nb * per_q, per_q), per_q) for c in range(QUAD)]
        kp = [pl.ds(pl.multiple_of(c * QUAD_ROWS + kb * per_q, per_q), per_k) for c in range(QUAD)]
        res = attend(load(q4, qp), load(k4, kp), load(v4, kp), bias_ref[jnp.minimum(nb, 1)])
        store((o0, m0, d0), qp, res)
        return carry

    lax.fori_loop(0, SEQ // blk, body1, 0, unroll=ATTN_UNROLL)

    n_blk4 = QUAD_ROWS // blk

    def body4(idx, carry):
        nb = idx & (n_blk4 - 1)
        base = (idx - nb) * blk
        kb = jnp.maximum(nb - 1, 0)
        qp = [pl.ds(pl.multiple_of(base + nb * blk, blk), blk)]
        kp = [pl.ds(pl.multiple_of(base + kb * blk, blk), 2 * blk)]
        res = attend(load(q4, qp), load(k4, kp), load(v4, kp), bias_ref[2 + jnp.minimum(nb, 1)])
        store((o1, m1, d1), qp, res)
        return carry

    lax.fori_loop(0, SEQ // blk, body4, 0, unroll=ATTN_UNROLL)

    def body16(idx, carry):
        start = (idx & (QUAD - 1)) * QUAD_ROWS + lax.shift_right_logical(idx, 2)
        kp = [pl.ds(start, 2 * blk, stride=QUAD)]
        k2, v2 = load(k4, kp), load(v4, kp)
        for nb in range(2):
            qp = [pl.ds(start + nb * blk * QUAD, blk, stride=QUAD)]
            store((o2, m2, d2), qp, attend(load(q4, qp), k2, v2, bias_ref[2 + nb]))
        return carry

    lax.fori_loop(0, SEQ // (2 * blk), body16, 0, unroll=ATTN_UNROLL // 2)

    chunk = 512

    def mix(i, carry):
        r = pl.ds(pl.multiple_of(i * chunk, chunk), chunk)
        ma, mb, mc = m0[r, :], m1[r, :], m2[r, :]
        m = jnp.maximum(jnp.maximum(ma, mb), mc)
        wa, wb, wc = jnp.exp2(ma - m), jnp.exp2(mb - m), jnp.exp2(mc - m)
        acc = wa * o0[r, :] + wb * o1[r, :] + wc * o2[r, :]
        den = wa * d0[r, :] + wb * d1[r, :] + wc * d2[r, :]
        per_class = QUAD_ROWS // chunk
        c = i // per_class
        n0 = (i - c * per_class) * chunk
        out_ref[pl.ds(QUAD * n0 + c, chunk, stride=QUAD), :] = acc / den
        return carry

    lax.fori_loop(0, SEQ // chunk, mix, 0)


def _attn(q, k, v, batch):
    spec = pl.BlockSpec((None, SEQ, LANES), lambda b, hp: (b, 0, hp))
    bias = _attn_bias()
    return pl.pallas_call(
        _attn_kernel,
        grid=(batch, ATTN_WIDTH // LANES),
        in_specs=[spec, spec, spec, pl.BlockSpec(bias.shape, lambda b, hp: (0, 0, 0))],
        out_specs=spec,
        out_shape=jax.ShapeDtypeStruct((batch, SEQ, ATTN_WIDTH), F32),
        scratch_shapes=[pltpu.VMEM((SEQ, LANES), F32) for _ in range(12)],
        compiler_params=pltpu.CompilerParams(
            dimension_semantics=("arbitrary", "arbitrary"), vmem_limit_bytes=ATTN_VMEM_LIMIT),
        name="dilated_attn",
    )(q, k, v, jnp.asarray(bias))


def _pack_bf16_pairs(x):
    c = x.shape[1] // 2
    bits = pltpu.bitcast(x.astype(BF16).astype(F32), jnp.uint32)
    return lax.shift_right_logical(bits[:, :c], jnp.uint32(16)) | bits[:, c:]


def _unpack_bf16_pairs(w):
    lo = pltpu.bitcast(lax.shift_left(w, jnp.uint32(16)), F32)
    hi = pltpu.bitcast(w & jnp.uint32(0xFFFF0000), F32)
    return jnp.concatenate([lo, hi], axis=1).astype(BF16)


def _post_kernel(x_ref, attn_ref, u_ref, halo_ref, gate_ref, wao_ref, wmix_ref, pscale_ref,
                 wpo_ref, wo_ref, g2_ref, wr_ref, br_ref,
                 xo_ref, h2_ref, ri_ref, rf_ref, cnt_ref):
    @pl.when(pl.program_id(0) == 0)
    def _():
        cnt_ref[...] = jnp.zeros_like(cnt_ref)

    tile_pos = lax.rem(pl.program_id(0) * x_ref.shape[0], SEQ)
    for r0 in range(0, x_ref.shape[0], POST_SUBTILE):
        _post_subtile(r0, tile_pos, x_ref, attn_ref, u_ref, halo_ref, gate_ref, wao_ref, wmix_ref,
                      pscale_ref, wpo_ref, wo_ref, g2_ref, wr_ref, br_ref,
                      xo_ref, h2_ref, ri_ref, rf_ref, cnt_ref)


def _post_subtile(r0, tile_pos, x_ref, attn_ref, u_ref, halo_ref, gate_ref, wao_ref, wmix_ref,
                  pscale_ref, wpo_ref, wo_ref, g2_ref, wr_ref, br_ref,
                  xo_ref, h2_ref, ri_ref, rf_ref, cnt_ref):
    tm = POST_SUBTILE
    rows = slice(r0, r0 + tm)
    pos0 = tile_pos + r0
    pos = pos0 + lax.broadcasted_iota(jnp.int32, (tm, 1), 0)
    u = u_ref[rows, :]
    if r0 == 0:
        halo = halo_ref[...] * (pos0 > 0).astype(F32)
    else:
        halo = u_ref[r0 - POOL_HALO:r0, :]

    mixed = []
    for gi, w in enumerate(POOL_WINDOWS):
        lo = gi * POOL_GROUP_DIM
        ug = u[:, lo:lo + POOL_GROUP_DIM]
        ext = jnp.concatenate([halo[:, lo:lo + POOL_GROUP_DIM], ug], axis=0)
        shift = 1
        while shift < w:
            ext = ext + pltpu.roll(ext, shift, 0)
            shift *= 2
        cnt = jnp.minimum(pos + 1, w).astype(F32)
        pooled = ext[POOL_HALO:] / cnt - ug
        mixed.append(jnp.dot(pooled.astype(BF16), wmix_ref[gi], preferred_element_type=F32))
    pool_out = (jnp.concatenate(mixed, axis=-1) * pscale_ref[...]).astype(BF16)

    y_a = jnp.dot(attn_ref[rows, :].astype(BF16), wao_ref[...], preferred_element_type=F32)
    y_p = jnp.dot(pool_out, wpo_ref[...], preferred_element_type=F32)
    gates = gate_ref[rows, :]
    merged = gates[:, :D_MODEL].astype(F32) * y_a + gates[:, D_MODEL:].astype(F32) * y_p
    x_new = x_ref[rows, :] + jnp.dot(merged.astype(BF16), wo_ref[...], preferred_element_type=F32)
    xo_ref[rows, :] = x_new

    ms = jnp.mean(x_new * x_new, axis=-1, keepdims=True)
    h2 = x_new * lax.rsqrt(ms + RMS_EPS) * g2_ref[...]
    h_hi = h2.astype(BF16)
    h2_ref[rows, :] = _pack_bf16_pairs(h2)

    h_lo = (h2 - h_hi.astype(F32)).astype(BF16)
    nt = (((1,), (1,)), ((), ()))
    both = lax.dot_general(wr_ref[...], h_hi, nt, preferred_element_type=F32)
    cross = lax.dot_general(wr_ref[:ROUTER_ROWS, :], h_lo, nt, preferred_element_type=F32)
    logits = both[:ROUTER_ROWS] + both[ROUTER_ROWS:] + cross + jnp.concatenate(
        [br_ref[...]] * (tm // LANES), axis=1)

    sub = lax.broadcasted_iota(jnp.int32, (8, tm), 0)
    npg = EXPERTS_PER_GROUP

    def first_max(vals):
        vmax = jnp.max(vals, axis=0, keepdims=True)
        return vmax, jnp.min(jnp.where(vals == vmax, sub, npg), axis=0, keepdims=True)

    def of_group(parts, g_sel):
        out = parts[-1]
        for g in range(N_EXPERT_GROUPS - 2, -1, -1):
            out = jnp.where(g_sel == g, parts[g], out)
        return out

    glog = jnp.where(sub < N_EXPERT_GROUPS, logits[0:8], -jnp.inf)
    gmax, g_sel = first_max(glog)
    g_gate = 1.0 / jnp.sum(jnp.exp(glog - gmax), axis=0, keepdims=True)

    elog = of_group([logits[npg * (g + 1):npg * (g + 2)] for g in range(N_EXPERT_GROUPS)], g_sel)
    e_exp = jnp.exp(elog - jnp.max(elog, axis=0, keepdims=True))
    prob = e_exp / jnp.sum(e_exp, axis=0, keepdims=True)
    p1, i1 = first_max(prob)
    p2, i2 = first_max(jnp.where(sub == i1, -1.0, prob))
    scale = g_gate / (p1 + p2)

    chosen = (sub == i1) | (sub == i2)
    onehot = jnp.concatenate(
        [jnp.where(chosen & (g_sel == g), 1.0, 0.0) for g in range(N_EXPERT_GROUPS)],
        axis=0).astype(BF16)
    t_row = lax.broadcasted_iota(jnp.int32, (tm, tm), 0)
    t_col = lax.broadcasted_iota(jnp.int32, (tm, tm), 1)
    before = jnp.dot(onehot, jnp.where(t_row < t_col, 1.0, 0.0).astype(BF16),
                     preferred_element_type=F32)
    total = jnp.dot(onehot, jnp.ones((tm, LANES), BF16), preferred_element_type=F32)
    seen = cnt_ref[...]
    slot = before + jnp.concatenate([seen] * (tm // LANES), axis=1)
    cnt_ref[...] = seen + total
    slot = of_group([slot[npg * g:npg * (g + 1)] for g in range(N_EXPERT_GROUPS)], g_sel)
    r1 = jnp.sum(jnp.where(sub == i1, slot, 0.0), axis=0, keepdims=True).astype(jnp.int32)
    r2 = jnp.sum(jnp.where(sub == i2, slot, 0.0), axis=0, keepdims=True).astype(jnp.int32)

    e_base = g_sel * npg
    ri_ref[:, rows] = jnp.where(sub == 0, e_base + i1,
                            jnp.where(sub == 1, e_base + i2,
                                      jnp.where(sub == 2, r1, jnp.where(sub == 3, r2, 0))))
    gates_t = jnp.where(sub == 0, p1 * scale, jnp.where(sub == 1, p2 * scale, 0.0))
    rf_ref[rows, :] = jnp.concatenate([gates_t, jnp.zeros((LANES - 8, tm), F32)], axis=0).T


ROUTER_ROWS = 8 + N_EXPERTS + 8


def _router_operands(w_group, b_group, w_expert, b_expert):
    d = w_group.shape[0]
    wt = jnp.zeros((ROUTER_ROWS, d), F32)
    wt = wt.at[:N_EXPERT_GROUPS].set(w_group.T)
    wt = wt.at[8:8 + N_EXPERTS].set(w_expert.transpose(0, 2, 1).reshape(N_EXPERTS, d))
    hi = wt.astype(BF16)
    lo = (wt - hi.astype(F32)).astype(BF16)
    b = jnp.zeros((ROUTER_ROWS,), F32).at[:N_EXPERT_GROUPS].set(b_group)
    b = b.at[8:8 + N_EXPERTS].set(b_expert.reshape(-1))
    return jnp.concatenate([hi, lo], axis=0), jnp.broadcast_to(b[:, None], (ROUTER_ROWS, LANES))


def _post(x, attn, u, gates, wao, wmix, pscale, wpo, wo, g2, wr, br):
    n = x.shape[0]
    tm = POST_TILE
    row = lambda i: (i, 0)
    const = lambda i: (0, 0)
    halo_blocks = tm // POOL_HALO
    return pl.pallas_call(
        _post_kernel,
        grid=(n // tm,),
        in_specs=[
            pl.BlockSpec((tm, D_MODEL), row),
            pl.BlockSpec((tm, ATTN_WIDTH), row),
            pl.BlockSpec((tm, POOL_WIDTH), row),
            pl.BlockSpec((POOL_HALO, POOL_WIDTH),
                         lambda i: (jnp.maximum(i * halo_blocks - 1, 0), 0)),
            pl.BlockSpec((tm, 2 * D_MODEL), row),
            pl.BlockSpec((ATTN_WIDTH, D_MODEL), const),
            pl.BlockSpec((len(POOL_WINDOWS), POOL_GROUP_DIM, POOL_GROUP_DIM), lambda i: (0, 0, 0)),
            pl.BlockSpec((1, POOL_WIDTH), const),
            pl.BlockSpec((POOL_WIDTH, D_MODEL), const),
            pl.BlockSpec((D_MODEL, D_MODEL), const),
            pl.BlockSpec((1, D_MODEL), const),
            pl.BlockSpec((2 * ROUTER_ROWS, D_MODEL), const),
            pl.BlockSpec((ROUTER_ROWS, LANES), const),
        ],
        out_specs=[
            pl.BlockSpec((tm, D_MODEL), row),
            pl.BlockSpec((tm, D_MODEL // 2), row),
            pl.BlockSpec((8, tm), lambda i: (0, i)),
            pl.BlockSpec((tm, LANES), row),
            pl.BlockSpec((N_EXPERTS, LANES), const),
        ],
        out_shape=[
            jax.ShapeDtypeStruct((n, D_MODEL), F32),
            jax.ShapeDtypeStruct((n, D_MODEL // 2), jnp.uint32),
            jax.ShapeDtypeStruct((8, n), jnp.int32),
            jax.ShapeDtypeStruct((n, LANES), F32),
            jax.ShapeDtypeStruct((N_EXPERTS, LANES), F32),
        ],
        compiler_params=_params(1),
        name="post_attn_router",
    )(x, attn, u, u, gates, wao, wmix, pscale, wpo, wo, g2, wr, br)


ZERO_ROWS = 8
SLAB = D_MODEL // 2 // LANES


def _experts_kernel(layer, be_ref, nused_ref, dest_ref, plan_ref, seg_end_ref,
                    h_hbm, wg_hbm, wu_hbm, wd_hbm,
                    y_ref, h_vmem, xs0, xs1, wg_f32, wu_f32, wd_f32, wg_bf, wu_bf, wd_bf,
                    h_sem, w_sems, tok_ref):
    step = pl.program_id(0)
    bm = xs0.shape[0]
    n_tok = h_hbm.shape[0]
    n_used = nused_ref[0]

    def weight_copies(e):
        return [pltpu.make_async_copy(src.at[layer, e], dst, w_sems.at[n])
                for n, (src, dst) in enumerate(
                    ((wg_hbm, wg_f32), (wu_hbm, wu_f32), (wd_hbm, wd_f32)))]

    @pl.when(step == 0)
    def _():
        @pl.when(n_used > 0)
        def _():
            for c in weight_copies(be_ref[0]):
                c.start(priority=1)
        copy = pltpu.make_async_copy(h_hbm, h_vmem.at[pl.ds(0, n_tok)], h_sem)
        copy.start()
        h_vmem[pl.ds(n_tok, ZERO_ROWS), :] = jnp.zeros((ZERO_ROWS, h_vmem.shape[1]), h_vmem.dtype)
        _sorted_tokens_kernel(dest_ref, seg_end_ref, tok_ref)
        copy.wait()

        def gather(j, carry):
            xs0[pl.ds(j, 1), :] = h_vmem[pl.ds(tok_ref[j], 1), :]
            return carry
        lax.fori_loop(0, bm, gather, 0, unroll=8)

    def block(i, x_ref, x_next_ref, y_rows):
        used = i < n_used
        changed = (i == 0) | (be_ref[i] != be_ref[jnp.maximum(i - 1, 0)])

        @pl.when(changed & used)
        def _():
            for c in weight_copies(be_ref[i]):
                c.wait()
            wg_bf[...] = wg_f32[...].astype(BF16)
            wu_bf[...] = wu_f32[...].astype(BF16)
            wd_bf[...] = wd_f32[...].astype(BF16)

            @pl.when(plan_ref[i] != be_ref[i])
            def _():
                for n, c in enumerate(weight_copies(plan_ref[i])):
                    c.start(priority=n % 2)

        @pl.when(used)
        def _():
            nxt = jnp.where(i + 1 < n_used, i + 1, i) * bm
            for j in range(bm):
                x_next_ref[pl.ds(j, 1), :] = h_vmem[pl.ds(tok_ref[nxt + j], 1), :]

            x = _unpack_bf16_pairs(x_ref[...])
            a = jnp.dot(x, wg_bf[...], preferred_element_type=F32)
            b = jnp.dot(x, wu_bf[...], preferred_element_type=F32)
            mid = (a * jax.nn.sigmoid(a) * b).astype(BF16)
            y = _pack_bf16_pairs(jnp.dot(mid, wd_bf[...], preferred_element_type=F32))
            for c in range(SLAB):
                y_ref[pl.ds(y_rows + c, bm, stride=SLAB), :] = y[:, c * LANES:(c + 1) * LANES]

        @pl.when(jnp.logical_not(used))
        def _():
            y_ref[pl.ds(y_rows, bm * SLAB), :] = jnp.zeros((bm * SLAB, LANES), y_ref.dtype)

    block(2 * step, xs0, xs1, 0)
    block(2 * step + 1, xs1, xs0, bm * SLAB)


def _experts(block_expert, n_used, dest, next_expert, seg_end, buf_len, h2_packed,
             w_gate, w_up, w_down, layer):
    assert TOP_K == 2 and buf_len < (1 << 16)
    n_tok, width = h2_packed.shape
    packed_dest = dest[:n_tok] | (dest[n_tok:] << 16)
    bm = EXPERT_BLOCK
    any_space = pl.BlockSpec(memory_space=pl.ANY)
    n_blocks = buf_len // bm
    assert n_blocks % 2 == 0
    grid_spec = pltpu.PrefetchScalarGridSpec(
        num_scalar_prefetch=5,
        grid=(n_blocks // 2,),
        in_specs=[any_space, any_space, any_space, any_space],
        out_specs=pl.BlockSpec((2 * bm * SLAB, LANES), lambda i, *_: (i, 0)),
        scratch_shapes=[
            pltpu.VMEM((n_tok + ZERO_ROWS, width), h2_packed.dtype),
            pltpu.VMEM((bm, width), h2_packed.dtype),
            pltpu.VMEM((bm, width), h2_packed.dtype),
            pltpu.VMEM((D_MODEL, EXPERT_HIDDEN), F32),
            pltpu.VMEM((D_MODEL, EXPERT_HIDDEN), F32),
            pltpu.VMEM((EXPERT_HIDDEN, D_MODEL), F32),
            pltpu.VMEM((D_MODEL, EXPERT_HIDDEN), BF16),
            pltpu.VMEM((D_MODEL, EXPERT_HIDDEN), BF16),
            pltpu.VMEM((EXPERT_HIDDEN, D_MODEL), BF16),
            pltpu.SemaphoreType.DMA(()),
            pltpu.SemaphoreType.DMA((3,)),
            pltpu.SMEM((buf_len,), jnp.int32),
        ],
    )
    return pl.pallas_call(
        functools.partial(_experts_kernel, layer),
        grid_spec=grid_spec,
        out_shape=jax.ShapeDtypeStruct((buf_len * SLAB, LANES), h2_packed.dtype),
        compiler_params=pltpu.CompilerParams(
            dimension_semantics=("arbitrary",), vmem_limit_bytes=EXPERTS_VMEM_LIMIT),
        name="experts",
    )(block_expert, n_used, packed_dest, next_expert, seg_end, h2_packed, w_gate, w_up, w_down)


def _sorted_tokens_kernel(dest_ref, seg_end_ref, tok_ref):
    n_tok = dest_ref.shape[0]
    bm = EXPERT_BLOCK
    buf_len = tok_ref.shape[0]

    def fill_from(start):
        def fill(p, carry):
            tok_ref[start + p] = n_tok
            return carry
        return fill

    for e in range(N_EXPERTS):
        lax.fori_loop(0, bm, fill_from(jnp.maximum(seg_end_ref[e] - bm, 0)), 0, unroll=16)
    lax.fori_loop(0, N_EXPERTS * bm, fill_from(buf_len - N_EXPERTS * bm), 0, unroll=16)

    def place(t, carry):
        both = dest_ref[t]
        tok_ref[both & 0xFFFF] = t
        tok_ref[lax.shift_right_logical(both, 16)] = t
        return carry
    lax.fori_loop(0, n_tok, place, 0, unroll=8)


def _combine_kernel(dest_ref, x_ref, gate_ref, y_ref, out_ref, ya0, ya1, yb0, yb1, sems):
    tm = x_ref.shape[0]
    i = pl.program_id(0)
    n_steps = pl.num_programs(0)
    n_tok = n_steps * tm
    half = x_ref.shape[1] // 2
    pieces = half // LANES

    def row_copy(step, bufs, sem, r, k):
        d = pl.multiple_of(dest_ref[k * n_tok + step * tm + r], SLAB)
        dst_row = r * SLAB if isinstance(r, int) else pl.multiple_of(r * SLAB, SLAB)
        return pltpu.make_async_copy(y_ref.at[pl.ds(d, pieces)],
                                     bufs[k].at[pl.ds(dst_row, pieces)], sem)

    def drain(step, bufs, sem):
        def wait(r, carry):
            for k in range(TOP_K):
                row_copy(step, bufs, sem, r, k).wait()
            return carry
        lax.fori_loop(0, tm, wait, 0, unroll=8)

    @pl.when(i == 0)
    def _():
        def issue(r, carry):
            for k in range(TOP_K):
                row_copy(0, (ya0, ya1), sems.at[0], r, k).start()
            return carry
        lax.fori_loop(0, tm, issue, 0, unroll=8)

    def tile(bufs, sem, next_bufs, next_sem):
        @pl.when(i + 1 < n_steps)
        def _():
            for r in range(tm):
                for k in range(TOP_K):
                    row_copy(i + 1, next_bufs, next_sem, r, k).start(priority=(r + k) % 2)

        drain(i, bufs, sem)
        g = gate_ref[...]
        for c in range(pieces):
            w = [b[pl.ds(c, tm, stride=SLAB), :] for b in bufs]
            for base, unpack in ((0, lambda v: lax.shift_left(v, jnp.uint32(16))),
                                 (half, lambda v: v & jnp.uint32(0xFFFF0000))):
                cols = slice(base + c * LANES, base + (c + 1) * LANES)
                out_ref[:, cols] = (x_ref[:, cols]
                                    + g[:, 0:1] * pltpu.bitcast(unpack(w[0]), F32)
                                    + g[:, 1:2] * pltpu.bitcast(unpack(w[1]), F32))

    parity = i & 1
    pl.when(parity == 0)(lambda: tile((ya0, ya1), sems.at[0], (yb0, yb1), sems.at[1]))
    pl.when(parity == 1)(lambda: tile((yb0, yb1), sems.at[1], (ya0, ya1), sems.at[0]))


def _combine(dest_slab, x_new, gates, y_slabs):
    n, d = x_new.shape
    tm = ROW_TILE
    grid_spec = pltpu.PrefetchScalarGridSpec(
        num_scalar_prefetch=1,
        grid=(n // tm,),
        in_specs=[
            pl.BlockSpec((tm, d), lambda i, dest: (i, 0)),
            pl.BlockSpec((tm, LANES), lambda i, dest: (i, 0)),
            pl.BlockSpec(memory_space=pl.ANY),
        ],
        out_specs=pl.BlockSpec((tm, d), lambda i, dest: (i, 0)),
        scratch_shapes=[pltpu.VMEM((tm * SLAB, LANES), y_slabs.dtype) for _ in range(2 * TOP_K)]
        + [pltpu.SemaphoreType.DMA((2,))],
    )
    return pl.pallas_call(
        _combine_kernel,
        grid_spec=grid_spec,
        out_shape=jax.ShapeDtypeStruct((n, d), F32),
        compiler_params=_params(1),
        name="moe_combine",
    )(dest_slab, x_new, gates, y_slabs)


def _dispatch_plan(expert_id, slot, counts):
    bm = EXPERT_BLOCK
    n_assign = expert_id.size
    padded = (counts + bm - 1) // bm * bm
    pend = jnp.cumsum(padded)
    pstart = pend - padded
    experts = jnp.arange(N_EXPERTS, dtype=jnp.int32)
    seg_start = jnp.sum(jnp.where(expert_id[..., None] == experts, pstart, 0), axis=-1)
    dest = (seg_start + slot).reshape(-1).astype(jnp.int32)
    buf_len = n_assign + N_EXPERTS * bm
    block_start = jnp.arange(buf_len // bm, dtype=jnp.int32) * bm
    n_used = (pend[-1:] // bm).astype(jnp.int32)
    block_start = jnp.minimum(block_start, pend[-1] - bm)
    block_expert = jnp.sum((pend[None, :] <= block_start[:, None]).astype(jnp.int32), axis=1)
    block_expert = jnp.minimum(block_expert, N_EXPERTS - 1)
    later = jnp.where(block_expert[None, :] > block_expert[:, None], block_expert[None, :], N_EXPERTS)
    next_expert = jnp.min(later, axis=1)
    next_expert = jnp.where(next_expert == N_EXPERTS, block_expert, next_expert).astype(jnp.int32)
    return dest, block_expert, next_expert, n_used, pend.astype(jnp.int32), buf_len


def kernel(x, norm1_g, w_in, q_norm_g, k_norm_g, w_attn_out, w_pool_mix, pool_scale, w_pool_out,
           w_o, norm2_g, w_router_group, b_router_group, w_router_expert, b_router_expert,
           w_exp_gate, w_exp_up, w_exp_down):
    batch, seq, d = x.shape
    assert (seq, d) == (SEQ, D_MODEL)
    n_tok = batch * seq
    depth = w_in.shape[0]
    half = ATTN_WIDTH // 2
    hsum = (jnp.arange(half)[:, None] // HEAD_DIM == jnp.arange(half)[None, :] // HEAD_DIM).astype(BF16)

    xf = x.reshape(n_tok, d)
    for layer in range(depth):
        qg = jnp.tile(q_norm_g[layer], N_HEADS)[None, :]
        kg = jnp.tile(k_norm_g[layer], N_HEADS)[None, :]
        q, k, v, u, gates = _inproj(xf, norm1_g[layer][None, :], w_in[layer].astype(BF16), qg, kg, hsum)
        attn = _attn(q.reshape(batch, seq, ATTN_WIDTH), k.reshape(batch, seq, ATTN_WIDTH),
                     v.reshape(batch, seq, ATTN_WIDTH), batch).reshape(n_tok, ATTN_WIDTH)

        w_router, b_router = _router_operands(
            w_router_group[layer], b_router_group[layer], w_router_expert[layer], b_router_expert[layer])
        x_new, h2, route_i, gate_rows, counts = _post(
            xf, attn, u, gates, w_attn_out[layer].astype(BF16), w_pool_mix[layer].astype(BF16),
            pool_scale[layer][None, :], w_pool_out[layer].astype(BF16), w_o[layer].astype(BF16),
            norm2_g[layer][None, :], w_router, b_router)

        dest, block_expert, next_expert, n_used, seg_end, buf_len = _dispatch_plan(
            route_i[:TOP_K], route_i[TOP_K:2 * TOP_K], counts[:, 0].astype(jnp.int32))
        y_rows = _experts(block_expert, n_used, dest, next_expert, seg_end, buf_len, h2,
                          w_exp_gate, w_exp_up, w_exp_down, layer)
        xf = _combine(dest * SLAB, x_new, gate_rows, y_rows)
    return xf.reshape(batch, seq, d)
```

```python
import functools

import jax
import jax.numpy as jnp
import numpy as np
from jax import lax
from jax.experimental import pallas as pl
from jax.experimental.pallas import tpu as pltpu

D_MODEL = 1024
SEQ = 4096
N_HEADS = 8
HEAD_DIM = 64
ATTN_WIDTH = N_HEADS * HEAD_DIM
DILATIONS = (1, 4, 16)
ATTN_BLOCK = 128
POOL_WINDOWS = (2, 4, 8, 16)
POOL_GROUP_DIM = 128
POOL_WIDTH = len(POOL_WINDOWS) * POOL_GROUP_DIM
POOL_HALO = 16
IN_PROJ_WIDTH = 3 * ATTN_WIDTH + POOL_WIDTH + 2 * D_MODEL
N_EXPERT_GROUPS = 4
EXPERTS_PER_GROUP = 8
N_EXPERTS = N_EXPERT_GROUPS * EXPERTS_PER_GROUP
TOP_K = 2
EXPERT_HIDDEN = 512
RMS_EPS = 1e-6
NEG_INF = -1e30

LANES = 128
ROW_TILE = 512
INPROJ_TILE = 1024
POST_TILE = 1024
POST_SUBTILE = 1024
EXPERT_BLOCK = 256
VMEM_LIMIT = 56 * 1024 * 1024
ATTN_VMEM_LIMIT = 56 * 1024 * 1024
EXPERTS_VMEM_LIMIT = 56 * 1024 * 1024

F32 = jnp.float32
BF16 = jnp.bfloat16


def _params(n_axes):
    return pltpu.CompilerParams(
        dimension_semantics=("arbitrary",) * n_axes, vmem_limit_bytes=VMEM_LIMIT)


def _inproj_kernel(x_ref, g_ref, w_ref, qg_ref, kg_ref, hsum_ref,
                   q_ref, k_ref, v_ref, u_ref, gate_ref):
    x = x_ref[...]
    ms = jnp.mean(x * x, axis=-1, keepdims=True)
    h = (x * lax.rsqrt(ms + RMS_EPS) * g_ref[...]).astype(BF16)

    def proj(lo, hi):
        return jnp.dot(h, w_ref[:, lo:hi], preferred_element_type=F32)

    def head_norm(t, gain):
        sq = (t * t).astype(BF16)
        half = ATTN_WIDTH // 2
        ssq = jnp.concatenate(
            [jnp.dot(sq[:, j * half:(j + 1) * half], hsum_ref[...], preferred_element_type=F32)
             for j in range(2)], axis=-1)
        return t * lax.rsqrt(ssq * (1.0 / HEAD_DIM) + RMS_EPS) * gain

    w = ATTN_WIDTH
    q_ref[...] = head_norm(proj(0, w), qg_ref[...])
    k_ref[...] = head_norm(proj(w, 2 * w), kg_ref[...])
    v_ref[...] = proj(2 * w, 3 * w)
    u_ref[...] = proj(3 * w, 3 * w + POOL_WIDTH)
    base = 3 * w + POOL_WIDTH
    for j in range(2 * D_MODEL // 512):
        gate_ref[:, j * 512:(j + 1) * 512] = jax.nn.sigmoid(
            proj(base + j * 512, base + (j + 1) * 512)).astype(BF16)


def _inproj(x, g1, w_in, qg, kg, hsum):
    n = x.shape[0]
    tm = INPROJ_TILE
    row = lambda i: (i, 0)
    const = lambda i: (0, 0)
    return pl.pallas_call(
        _inproj_kernel,
        grid=(n // tm,),
        in_specs=[
            pl.BlockSpec((tm, D_MODEL), row),
            pl.BlockSpec((1, D_MODEL), const),
            pl.BlockSpec((D_MODEL, IN_PROJ_WIDTH), const, pipeline_mode=pl.Buffered(1)),
            pl.BlockSpec((1, ATTN_WIDTH), const),
            pl.BlockSpec((1, ATTN_WIDTH), const),
            pl.BlockSpec((ATTN_WIDTH // 2, ATTN_WIDTH // 2), const),
        ],
        out_specs=[
            pl.BlockSpec((tm, ATTN_WIDTH), row),
            pl.BlockSpec((tm, ATTN_WIDTH), row),
            pl.BlockSpec((tm, ATTN_WIDTH), row),
            pl.BlockSpec((tm, POOL_WIDTH), row),
            pl.BlockSpec((tm, 2 * D_MODEL), row),
        ],
        out_shape=[
            jax.ShapeDtypeStruct((n, ATTN_WIDTH), F32),
            jax.ShapeDtypeStruct((n, ATTN_WIDTH), F32),
            jax.ShapeDtypeStruct((n, ATTN_WIDTH), F32),
            jax.ShapeDtypeStruct((n, POOL_WIDTH), F32),
            jax.ShapeDtypeStruct((n, 2 * D_MODEL), BF16),
        ],
        compiler_params=_params(1),
        name="inproj",
    )(x, g1, w_in, qg, kg, hsum)


ATTN_UNROLL = 32
QUAD = 4
QUAD_ROWS = SEQ // QUAD
LOG2E = 1.4426950408889634


def _attn_bias():
    blk = ATTN_BLOCK
    r = np.arange(2 * blk) % blk
    c = np.arange(2 * blk)
    per_q, per_k = blk // QUAD, 2 * blk // QUAD
    tq = QUAD * (r % per_q) + r // per_q
    tk = QUAD * (c % per_k) + c // per_k
    masks = []
    for tq_, tk_ in ((tq, tk), (r, c)):
        first = tq_[:, None] - tk_[None, :]
        later = first + blk
        masks += [first >= 0, (later >= 0) & (later <= blk)]
    return np.where(np.stack(masks), 0.0, NEG_INF).astype(np.float32)


def _attn_kernel(q_ref, k_ref, v_ref, bias_ref, out_ref, q4, k4, v4,
                 o0, o1, o2, m0, m1, m2, d0, d1, d2):
    blk = ATTN_BLOCK
    head_a = lax.broadcasted_iota(jnp.int32, (blk, LANES), 1) < HEAD_DIM
    ones_cols = jnp.ones((2 * blk, LANES), BF16)

    for c in range(QUAD):
        dst = pl.ds(c * QUAD_ROWS, QUAD_ROWS)
        src = pl.ds(c, QUAD_ROWS, stride=QUAD)
        q4[dst, :] = q_ref[src, :] * (HEAD_DIM ** -0.5 * LOG2E)
        k4[dst, :] = k_ref[src, :]
        v4[dst, :] = v_ref[src, :]

    def attend(q2, k2, v2, bias):
        qs = jnp.concatenate(
            [jnp.where(head_a, q2, 0.0), jnp.where(head_a, 0.0, q2)], axis=0).astype(BF16)
        s = lax.dot_general(qs, k2.astype(BF16), (((1,), (1,)), ((), ())),
                            preferred_element_type=F32) + bias
        m = jnp.max(s, axis=-1, keepdims=True)
        e = jnp.exp2(s - m).astype(BF16)
        r = jnp.dot(e, jnp.concatenate([v2.astype(BF16), ones_cols], axis=1),
                    preferred_element_type=F32)
        mb = jnp.broadcast_to(m, (2 * blk, LANES))
        return (jnp.where(head_a, r[:blk, :LANES], r[blk:, :LANES]),
                jnp.where(head_a, mb[:blk], mb[blk:]),
                jnp.where(head_a, r[:blk, LANES:], r[blk:, LANES:]))

    def load(ref, pieces):
        return jnp.concatenate([ref[p, :] for p in pieces], axis=0)

    def store(refs, pieces, vals):
        for ref, val in zip(refs, vals):
            row = 0
            for p in pieces:
                ref[p, :] = val[row:row + p.size]
                row += p.size

    def body1(nb, carry):
        kb = jnp.maximum(nb - 1, 0)
        per_q, per_k = blk // QUAD, 2 * blk // QUAD
        qp = [pl.ds(pl.multiple_of(c * QUAD_ROWS + nb * per_q, per_q), per_q) for c in range(QUAD)]
        kp = [pl.ds(pl.multiple_of(c * QUAD_ROWS + kb * per_q, per_q), per_k) for c in range(QUAD)]
        res = attend(load(q4, qp), load(k4, kp), load(v4, kp), bias_ref[jnp.minimum(nb, 1)])
        store((o0, m0, d0), qp, res)
        return carry

    lax.fori_loop(0, SEQ // blk, body1, 0, unroll=ATTN_UNROLL)

    n_blk4 = QUAD_ROWS // blk

    def body4(idx, carry):
        nb = idx & (n_blk4 - 1)
        base = (idx - nb) * blk
        kb = jnp.maximum(nb - 1, 0)
        qp = [pl.ds(pl.multiple_of(base + nb * blk, blk), blk)]
        kp = [pl.ds(pl.multiple_of(base + kb * blk, blk), 2 * blk)]
        res = attend(load(q4, qp), load(k4, kp), load(v4, kp), bias_ref[2 + jnp.minimum(nb, 1)])
        store((o1, m1, d1), qp, res)
        return carry

    lax.fori_loop(0, SEQ // blk, body4, 0, unroll=ATTN_UNROLL)

    def body16(idx, carry):
        start = (idx & (QUAD - 1)) * QUAD_ROWS + lax.shift_right_logical(idx, 2)
        kp = [pl.ds(start, 2 * blk, stride=QUAD)]
        k2, v2 = load(k4, kp), load(v4, kp)
        for nb in range(2):
            qp = [pl.ds(start + nb * blk * QUAD, blk, stride=QUAD)]
            store((o2, m2, d2), qp, attend(load(q4, qp), k2, v2, bias_ref[2 + nb]))
        return carry

    lax.fori_loop(0, SEQ // (2 * blk), body16, 0, unroll=ATTN_UNROLL // 2)

    chunk = 512

    def mix(i, carry):
        r = pl.ds(pl.multiple_of(i * chunk, chunk), chunk)
        ma, mb, mc = m0[r, :], m1[r, :], m2[r, :]
        m = jnp.maximum(jnp.maximum(ma, mb), mc)
        wa, wb, wc = jnp.exp2(ma - m), jnp.exp2(mb - m), jnp.exp2(mc - m)
        acc = wa * o0[r, :] + wb * o1[r, :] + wc * o2[r, :]
        den = wa * d0[r, :] + wb * d1[r, :] + wc * d2[r, :]
        per_class = QUAD_ROWS // chunk
        c = i // per_class
        n0 = (i - c * per_class) * chunk
        out_ref[pl.ds(QUAD * n0 + c, chunk, stride=QUAD), :] = acc / den
        return carry

    lax.fori_loop(0, SEQ // chunk, mix, 0)


def _attn(q, k, v, batch):
    spec = pl.BlockSpec((None, SEQ, LANES), lambda b, hp: (b, 0, hp))
    bias = _attn_bias()
    return pl.pallas_call(
        _attn_kernel,
        grid=(batch, ATTN_WIDTH // LANES),
        in_specs=[spec, spec, spec, pl.BlockSpec(bias.shape, lambda b, hp: (0, 0, 0))],
        out_specs=spec,
        out_shape=jax.ShapeDtypeStruct((batch, SEQ, ATTN_WIDTH), F32),
        scratch_shapes=[pltpu.VMEM((SEQ, LANES), F32) for _ in range(12)],
        compiler_params=pltpu.CompilerParams(
            dimension_semantics=("arbitrary", "arbitrary"), vmem_limit_bytes=ATTN_VMEM_LIMIT),
        name="dilated_attn",
    )(q, k, v, jnp.asarray(bias))


def _pack_bf16_pairs(x):
    c = x.shape[1] // 2
    bits = pltpu.bitcast(x.astype(BF16).astype(F32), jnp.uint32)
    return lax.shift_right_logical(bits[:, :c], jnp.uint32(16)) | bits[:, c:]


def _unpack_bf16_pairs(w):
    lo = pltpu.bitcast(lax.shift_left(w, jnp.uint32(16)), F32)
    hi = pltpu.bitcast(w & jnp.uint32(0xFFFF0000), F32)
    return jnp.concatenate([lo, hi], axis=1).astype(BF16)


def _post_kernel(x_ref, attn_ref, u_ref, halo_ref, gate_ref, wao_ref, wmix_ref, pscale_ref,
                 wpo_ref, wo_ref, g2_ref, wr_ref, br_ref,
                 xo_ref, h2_ref, ri_ref, rf_ref, cnt_ref):
    @pl.when(pl.program_id(0) == 0)
    def _():
        cnt_ref[...] = jnp.zeros_like(cnt_ref)

    tile_pos = lax.rem(pl.program_id(0) * x_ref.shape[0], SEQ)
    for r0 in range(0, x_ref.shape[0], POST_SUBTILE):
        _post_subtile(r0, tile_pos, x_ref, attn_ref, u_ref, halo_ref, gate_ref, wao_ref, wmix_ref,
                      pscale_ref, wpo_ref, wo_ref, g2_ref, wr_ref, br_ref,
                      xo_ref, h2_ref, ri_ref, rf_ref, cnt_ref)


def _post_subtile(r0, tile_pos, x_ref, attn_ref, u_ref, halo_ref, gate_ref, wao_ref, wmix_ref,
                  pscale_ref, wpo_ref, wo_ref, g2_ref, wr_ref, br_ref,
                  xo_ref, h2_ref, ri_ref, rf_ref, cnt_ref):
    tm = POST_SUBTILE
    rows = slice(r0, r0 + tm)
    pos0 = tile_pos + r0
    pos = pos0 + lax.broadcasted_iota(jnp.int32, (tm, 1), 0)
    u = u_ref[rows, :]
    if r0 == 0:
        halo = halo_ref[...] * (pos0 > 0).astype(F32)
    else:
        halo = u_ref[r0 - POOL_HALO:r0, :]

    mixed = []
    for gi, w in enumerate(POOL_WINDOWS):
        lo = gi * POOL_GROUP_DIM
        ug = u[:, lo:lo + POOL_GROUP_DIM]
        ext = jnp.concatenate([halo[:, lo:lo + POOL_GROUP_DIM], ug], axis=0)
        shift = 1
        while shift < w:
            ext = ext + pltpu.roll(ext, shift, 0)
            shift *= 2
        cnt = jnp.minimum(pos + 1, w).astype(F32)
        pooled = ext[POOL_HALO:] / cnt - ug
        mixed.append(jnp.dot(pooled.astype(BF16), wmix_ref[gi], preferred_element_type=F32))
    pool_out = (jnp.concatenate(mixed, axis=-1) * pscale_ref[...]).astype(BF16)

    y_a = jnp.dot(attn_ref[rows, :].astype(BF16), wao_ref[...], preferred_element_type=F32)
    y_p = jnp.dot(pool_out, wpo_ref[...], preferred_element_type=F32)
    gates = gate_ref[rows, :]
    merged = gates[:, :D_MODEL].astype(F32) * y_a + gates[:, D_MODEL:].astype(F32) * y_p
    x_new = x_ref[rows, :] + jnp.dot(merged.astype(BF16), wo_ref[...], preferred_element_type=F32)
    xo_ref[rows, :] = x_new

    ms = jnp.mean(x_new * x_new, axis=-1, keepdims=True)
    h2 = x_new * lax.rsqrt(ms + RMS_EPS) * g2_ref[...]
    h_hi = h2.astype(BF16)
    h2_ref[rows, :] = _pack_bf16_pairs(h2)

    h_lo = (h2 - h_hi.astype(F32)).astype(BF16)
    nt = (((1,), (1,)), ((), ()))
    both = lax.dot_general(wr_ref[...], h_hi, nt, preferred_element_type=F32)
    cross = lax.dot_general(wr_ref[:ROUTER_ROWS, :], h_lo, nt, preferred_element_type=F32)
    logits = both[:ROUTER_ROWS] + both[ROUTER_ROWS:] + cross + jnp.concatenate(
        [br_ref[...]] * (tm // LANES), axis=1)

    sub = lax.broadcasted_iota(jnp.int32, (8, tm), 0)
    npg = EXPERTS_PER_GROUP

    def first_max(vals):
        vmax = jnp.max(vals, axis=0, keepdims=True)
        return vmax, jnp.min(jnp.where(vals == vmax, sub, npg), axis=0, keepdims=True)

    def of_group(parts, g_sel):
        out = parts[-1]
        for g in range(N_EXPERT_GROUPS - 2, -1, -1):
            out = jnp.where(g_sel == g, parts[g], out)
        return out

    glog = jnp.where(sub < N_EXPERT_GROUPS, logits[0:8], -jnp.inf)
    gmax, g_sel = first_max(glog)
    g_gate = 1.0 / jnp.sum(jnp.exp(glog - gmax), axis=0, keepdims=True)

    elog = of_group([logits[npg * (g + 1):npg * (g + 2)] for g in range(N_EXPERT_GROUPS)], g_sel)
    e_exp = jnp.exp(elog - jnp.max(elog, axis=0, keepdims=True))
    prob = e_exp / jnp.sum(e_exp, axis=0, keepdims=True)
    p1, i1 = first_max(prob)
    p2, i2 = first_max(jnp.where(sub == i1, -1.0, prob))
    scale = g_gate / (p1 + p2)

    chosen = (sub == i1) | (sub == i2)
    onehot = jnp.concatenate(
        [jnp.where(chosen & (g_sel == g), 1.0, 0.0) for g in range(N_EXPERT_GROUPS)],
        axis=0).astype(BF16)
    t_row = lax.broadcasted_iota(jnp.int32, (tm, tm), 0)
    t_col = lax.broadcasted_iota(jnp.int32, (tm, tm), 1)
    before = jnp.dot(onehot, jnp.where(t_row < t_col, 1.0, 0.0).astype(BF16),
                     preferred_element_type=F32)
    total = jnp.dot(onehot, jnp.ones((tm, LANES), BF16), preferred_element_type=F32)
    seen = cnt_ref[...]
    slot = before + jnp.concatenate([seen] * (tm // LANES), axis=1)
    cnt_ref[...] = seen + total
    slot = of_group([slot[npg * g:npg * (g + 1)] for g in range(N_EXPERT_GROUPS)], g_sel)
    r1 = jnp.sum(jnp.where(sub == i1, slot, 0.0), axis=0, keepdims=True).astype(jnp.int32)
    r2 = jnp.sum(jnp.where(sub == i2, slot, 0.0), axis=0, keepdims=True).astype(jnp.int32)

    e_base = g_sel * npg
    ri_ref[:, rows] = jnp.where(sub == 0, e_base + i1,
                            jnp.where(sub == 1, e_base + i2,
                                      jnp.where(sub == 2, r1, jnp.where(sub == 3, r2, 0))))
    gates_t = jnp.where(sub == 0, p1 * scale, jnp.where(sub == 1, p2 * scale, 0.0))
    rf_ref[rows, :] = jnp.concatenate([gates_t, jnp.zeros((LANES - 8, tm), F32)], axis=0).T


ROUTER_ROWS = 8 + N_EXPERTS + 8


def _router_operands(w_group, b_group, w_expert, b_expert):
    d = w_group.shape[0]
    wt = jnp.zeros((ROUTER_ROWS, d), F32)
    wt = wt.at[:N_EXPERT_GROUPS].set(w_group.T)
    wt = wt.at[8:8 + N_EXPERTS].set(w_expert.transpose(0, 2, 1).reshape(N_EXPERTS, d))
    hi = wt.astype(BF16)
    lo = (wt - hi.astype(F32)).astype(BF16)
    b = jnp.zeros((ROUTER_ROWS,), F32).at[:N_EXPERT_GROUPS].set(b_group)
    b = b.at[8:8 + N_EXPERTS].set(b_expert.reshape(-1))
    return jnp.concatenate([hi, lo], axis=0), jnp.broadcast_to(b[:, None], (ROUTER_ROWS, LANES))


def _post(x, attn, u, gates, wao, wmix, pscale, wpo, wo, g2, wr, br):
    n = x.shape[0]
    tm = POST_TILE
    row = lambda i: (i, 0)
    const = lambda i: (0, 0)
    halo_blocks = tm // POOL_HALO
    return pl.pallas_call(
        _post_kernel,
        grid=(n // tm,),
        in_specs=[
            pl.BlockSpec((tm, D_MODEL), row),
            pl.BlockSpec((tm, ATTN_WIDTH), row),
            pl.BlockSpec((tm, POOL_WIDTH), row),
            pl.BlockSpec((POOL_HALO, POOL_WIDTH),
                         lambda i: (jnp.maximum(i * halo_blocks - 1, 0), 0)),
            pl.BlockSpec((tm, 2 * D_MODEL), row),
            pl.BlockSpec((ATTN_WIDTH, D_MODEL), const),
            pl.BlockSpec((len(POOL_WINDOWS), POOL_GROUP_DIM, POOL_GROUP_DIM), lambda i: (0, 0, 0)),
            pl.BlockSpec((1, POOL_WIDTH), const),
            pl.BlockSpec((POOL_WIDTH, D_MODEL), const),
            pl.BlockSpec((D_MODEL, D_MODEL), const),
            pl.BlockSpec((1, D_MODEL), const),
            pl.BlockSpec((2 * ROUTER_ROWS, D_MODEL), const),
            pl.BlockSpec((ROUTER_ROWS, LANES), const),
        ],
        out_specs=[
            pl.BlockSpec((tm, D_MODEL), row),
            pl.BlockSpec((tm, D_MODEL // 2), row),
            pl.BlockSpec((8, tm), lambda i: (0, i)),
            pl.BlockSpec((tm, LANES), row),
            pl.BlockSpec((N_EXPERTS, LANES), const),
        ],
        out_shape=[
            jax.ShapeDtypeStruct((n, D_MODEL), F32),
            jax.ShapeDtypeStruct((n, D_MODEL // 2), jnp.uint32),
            jax.ShapeDtypeStruct((8, n), jnp.int32),
            jax.ShapeDtypeStruct((n, LANES), F32),
            jax.ShapeDtypeStruct((N_EXPERTS, LANES), F32),
        ],
        compiler_params=_params(1),
        name="post_attn_router",
    )(x, attn, u, u, gates, wao, wmix, pscale, wpo, wo, g2, wr, br)


ZERO_ROWS = 8
SLAB = D_MODEL // 2 // LANES


def _experts_kernel(layer, be_ref, nused_ref, dest_ref, plan_ref, seg_end_ref,
                    h_hbm, wg_hbm, wu_hbm, wd_hbm,
                    y_ref, h_vmem, xs0, xs1, wg_f32, wu_f32, wd_f32, wg_bf, wu_bf, wd_bf,
                    h_sem, w_sems, tok_ref):
    step = pl.program_id(0)
    bm = xs0.shape[0]
    n_tok = h_hbm.shape[0]
    n_used = nused_ref[0]

    def weight_copies(e):
        return [pltpu.make_async_copy(src.at[layer, e], dst, w_sems.at[n])
                for n, (src, dst) in enumerate(
                    ((wg_hbm, wg_f32), (wu_hbm, wu_f32), (wd_hbm, wd_f32)))]

    @pl.when(step == 0)
    def _():
        @pl.when(n_used > 0)
        def _():
            for c in weight_copies(be_ref[0]):
                c.start(priority=1)
        copy = pltpu.make_async_copy(h_hbm, h_vmem.at[pl.ds(0, n_tok)], h_sem)
        copy.start()
        h_vmem[pl.ds(n_tok, ZERO_ROWS), :] = jnp.zeros((ZERO_ROWS, h_vmem.shape[1]), h_vmem.dtype)
        _sorted_tokens_kernel(dest_ref, seg_end_ref, tok_ref)
        copy.wait()

        def gather(j, carry):
            xs0[pl.ds(j, 1), :] = h_vmem[pl.ds(tok_ref[j], 1), :]
            return carry
        lax.fori_loop(0, bm, gather, 0, unroll=8)

    def block(i, x_ref, x_next_ref, y_rows):
        used = i < n_used
        changed = (i == 0) | (be_ref[i] != be_ref[jnp.maximum(i - 1, 0)])

        @pl.when(changed & used)
        def _():
            for c in weight_copies(be_ref[i]):
                c.wait()
            wg_bf[...] = wg_f32[...].astype(BF16)
            wu_bf[...] = wu_f32[...].astype(BF16)
            wd_bf[...] = wd_f32[...].astype(BF16)

            @pl.when(plan_ref[i] != be_ref[i])
            def _():
                for n, c in enumerate(weight_copies(plan_ref[i])):
                    c.start(priority=n % 2)

        @pl.when(used)
        def _():
            nxt = jnp.where(i + 1 < n_used, i + 1, i) * bm
            for j in range(bm):
                x_next_ref[pl.ds(j, 1), :] = h_vmem[pl.ds(tok_ref[nxt + j], 1), :]

            x = _unpack_bf16_pairs(x_ref[...])
            a = jnp.dot(x, wg_bf[...], preferred_element_type=F32)
            b = jnp.dot(x, wu_bf[...], preferred_element_type=F32)
            mid = (a * jax.nn.sigmoid(a) * b).astype(BF16)
            y = _pack_bf16_pairs(jnp.dot(mid, wd_bf[...], preferred_element_type=F32))
            for c in range(SLAB):
                y_ref[pl.ds(y_rows + c, bm, stride=SLAB), :] = y[:, c * LANES:(c + 1) * LANES]

        @pl.when(jnp.logical_not(used))
        def _():
            y_ref[pl.ds(y_rows, bm * SLAB), :] = jnp.zeros((bm * SLAB, LANES), y_ref.dtype)

    block(2 * step, xs0, xs1, 0)
    block(2 * step + 1, xs1, xs0, bm * SLAB)


def _experts(block_expert, n_used, dest, next_expert, seg_end, buf_len, h2_packed,
             w_gate, w_up, w_down, layer):
    assert TOP_K == 2 and buf_len < (1 << 16)
    n_tok, width = h2_packed.shape
    packed_dest = dest[:n_tok] | (dest[n_tok:] << 16)
    bm = EXPERT_BLOCK
    any_space = pl.BlockSpec(memory_space=pl.ANY)
    n_blocks = buf_len // bm
    assert n_blocks % 2 == 0
    grid_spec = pltpu.PrefetchScalarGridSpec(
        num_scalar_prefetch=5,
        grid=(n_blocks // 2,),
        in_specs=[any_space, any_space, any_space, any_space],
        out_specs=pl.BlockSpec((2 * bm * SLAB, LANES), lambda i, *_: (i, 0)),
        scratch_shapes=[
            pltpu.VMEM((n_tok + ZERO_ROWS, width), h2_packed.dtype),
            pltpu.VMEM((bm, width), h2_packed.dtype),
            pltpu.VMEM((bm, width), h2_packed.dtype),
            pltpu.VMEM((D_MODEL, EXPERT_HIDDEN), F32),
            pltpu.VMEM((D_MODEL, EXPERT_HIDDEN), F32),
            pltpu.VMEM((EXPERT_HIDDEN, D_MODEL), F32),
            pltpu.VMEM((D_MODEL, EXPERT_HIDDEN), BF16),
            pltpu.VMEM((D_MODEL, EXPERT_HIDDEN), BF16),
            pltpu.VMEM((EXPERT_HIDDEN, D_MODEL), BF16),
            pltpu.SemaphoreType.DMA(()),
            pltpu.SemaphoreType.DMA((3,)),
            pltpu.SMEM((buf_len,), jnp.int32),
        ],
    )
    return pl.pallas_call(
        functools.partial(_experts_kernel, layer),
        grid_spec=grid_spec,
        out_shape=jax.ShapeDtypeStruct((buf_len * SLAB, LANES), h2_packed.dtype),
        compiler_params=pltpu.CompilerParams(
            dimension_semantics=("arbitrary",), vmem_limit_bytes=EXPERTS_VMEM_LIMIT),
        name="experts",
    )(block_expert, n_used, packed_dest, next_expert, seg_end, h2_packed, w_gate, w_up, w_down)


def _sorted_tokens_kernel(dest_ref, seg_end_ref, tok_ref):
    n_tok = dest_ref.shape[0]
    bm = EXPERT_BLOCK
    buf_len = tok_ref.shape[0]

    def fill_from(start):
        def fill(p, carry):
            tok_ref[start + p] = n_tok
            return carry
        return fill

    for e in range(N_EXPERTS):
        lax.fori_loop(0, bm, fill_from(jnp.maximum(seg_end_ref[e] - bm, 0)), 0, unroll=16)
    lax.fori_loop(0, N_EXPERTS * bm, fill_from(buf_len - N_EXPERTS * bm), 0, unroll=16)

    def place(t, carry):
        both = dest_ref[t]
        tok_ref[both & 0xFFFF] = t
        tok_ref[lax.shift_right_logical(both, 16)] = t
        return carry
    lax.fori_loop(0, n_tok, place, 0, unroll=8)


def _combine_kernel(dest_ref, x_ref, gate_ref, y_ref, out_ref, ya0, ya1, yb0, yb1, sems):
    tm = x_ref.shape[0]
    i = pl.program_id(0)
    n_steps = pl.num_programs(0)
    n_tok = n_steps * tm
    half = x_ref.shape[1] // 2
    pieces = half // LANES

    def row_copy(step, bufs, sem, r, k):
        d = pl.multiple_of(dest_ref[k * n_tok + step * tm + r], SLAB)
        dst_row = r * SLAB if isinstance(r, int) else pl.multiple_of(r * SLAB, SLAB)
        return pltpu.make_async_copy(y_ref.at[pl.ds(d, pieces)],
                                     bufs[k].at[pl.ds(dst_row, pieces)], sem)

    def drain(step, bufs, sem):
        def wait(r, carry):
            for k in range(TOP_K):
                row_copy(step, bufs, sem, r, k).wait()
            return carry
        lax.fori_loop(0, tm, wait, 0, unroll=8)

    @pl.when(i == 0)
    def _():
        def issue(r, carry):
            for k in range(TOP_K):
                row_copy(0, (ya0, ya1), sems.at[0], r, k).start()
            return carry
        lax.fori_loop(0, tm, issue, 0, unroll=8)

    def tile(bufs, sem, next_bufs, next_sem):
        @pl.when(i + 1 < n_steps)
        def _():
            for r in range(tm):
                for k in range(TOP_K):
                    row_copy(i + 1, next_bufs, next_sem, r, k).start(priority=(r + k) % 2)

        drain(i, bufs, sem)
        g = gate_ref[...]
        for c in range(pieces):
            w = [b[pl.ds(c, tm, stride=SLAB), :] for b in bufs]
            for base, unpack in ((0, lambda v: lax.shift_left(v, jnp.uint32(16))),
                                 (half, lambda v: v & jnp.uint32(0xFFFF0000))):
                cols = slice(base + c * LANES, base + (c + 1) * LANES)
                out_ref[:, cols] = (x_ref[:, cols]
                                    + g[:, 0:1] * pltpu.bitcast(unpack(w[0]), F32)
                                    + g[:, 1:2] * pltpu.bitcast(unpack(w[1]), F32))

    parity = i & 1
    pl.when(parity == 0)(lambda: tile((ya0, ya1), sems.at[0], (yb0, yb1), sems.at[1]))
    pl.when(parity == 1)(lambda: tile((yb0, yb1), sems.at[1], (ya0, ya1), sems.at[0]))


def _combine(dest_slab, x_new, gates, y_slabs):
    n, d = x_new.shape
    tm = ROW_TILE
    grid_spec = pltpu.PrefetchScalarGridSpec(
        num_scalar_prefetch=1,
        grid=(n // tm,),
        in_specs=[
            pl.BlockSpec((tm, d), lambda i, dest: (i, 0)),
            pl.BlockSpec((tm, LANES), lambda i, dest: (i, 0)),
            pl.BlockSpec(memory_space=pl.ANY),
        ],
        out_specs=pl.BlockSpec((tm, d), lambda i, dest: (i, 0)),
        scratch_shapes=[pltpu.VMEM((tm * SLAB, LANES), y_slabs.dtype) for _ in range(2 * TOP_K)]
        + [pltpu.SemaphoreType.DMA((2,))],
    )
    return pl.pallas_call(
        _combine_kernel,
        grid_spec=grid_spec,
        out_shape=jax.ShapeDtypeStruct((n, d), F32),
        compiler_params=_params(1),
        name="moe_combine",
    )(dest_slab, x_new, gates, y_slabs)


def _dispatch_plan(expert_id, slot, counts):
    bm = EXPERT_BLOCK
    n_assign = expert_id.size
    padded = (counts + bm - 1) // bm * bm
    pend = jnp.cumsum(padded)
    pstart = pend - padded
    experts = jnp.arange(N_EXPERTS, dtype=jnp.int32)
    seg_start = jnp.sum(jnp.where(expert_id[..., None] == experts, pstart, 0), axis=-1)
    dest = (seg_start + slot).reshape(-1).astype(jnp.int32)
    buf_len = n_assign + N_EXPERTS * bm
    block_start = jnp.arange(buf_len // bm, dtype=jnp.int32) * bm
    n_used = (pend[-1:] // bm).astype(jnp.int32)
    block_start = jnp.minimum(block_start, pend[-1] - bm)
    block_expert = jnp.sum((pend[None, :] <= block_start[:, None]).astype(jnp.int32), axis=1)
    block_expert = jnp.minimum(block_expert, N_EXPERTS - 1)
    later = jnp.where(block_expert[None, :] > block_expert[:, None], block_expert[None, :], N_EXPERTS)
    next_expert = jnp.min(later, axis=1)
    next_expert = jnp.where(next_expert == N_EXPERTS, block_expert, next_expert).astype(jnp.int32)
    return dest, block_expert, next_expert, n_used, pend.astype(jnp.int32), buf_len


def kernel(x, norm1_g, w_in, q_norm_g, k_norm_g, w_attn_out, w_pool_mix, pool_scale, w_pool_out,
           w_o, norm2_g, w_router_group, b_router_group, w_router_expert, b_router_expert,
           w_exp_gate, w_exp_up, w_exp_down):
    batch, seq, d = x.shape
    assert (seq, d) == (SEQ, D_MODEL)
    n_tok = batch * seq
    depth = w_in.shape[0]
    half = ATTN_WIDTH // 2
    hsum = (jnp.arange(half)[:, None] // HEAD_DIM == jnp.arange(half)[None, :] // HEAD_DIM).astype(BF16)

    xf = x.reshape(n_tok, d)
    for layer in range(depth):
        qg = jnp.tile(q_norm_g[layer], N_HEADS)[None, :]
        kg = jnp.tile(k_norm_g[layer], N_HEADS)[None, :]
        q, k, v, u, gates = _inproj(xf, norm1_g[layer][None, :], w_in[layer].astype(BF16), qg, kg, hsum)
        attn = _attn(q.reshape(batch, seq, ATTN_WIDTH), k.reshape(batch, seq, ATTN_WIDTH),
                     v.reshape(batch, seq, ATTN_WIDTH), batch).reshape(n_tok, ATTN_WIDTH)

        w_router, b_router = _router_operands(
            w_router_group[layer], b_router_group[layer], w_router_expert[layer], b_router_expert[layer])
        x_new, h2, route_i, gate_rows, counts = _post(
            xf, attn, u, gates, w_attn_out[layer].astype(BF16), w_pool_mix[layer].astype(BF16),
            pool_scale[layer][None, :], w_pool_out[layer].astype(BF16), w_o[layer].astype(BF16),
            norm2_g[layer][None, :], w_router, b_router)

        dest, block_expert, next_expert, n_used, seg_end, buf_len = _dispatch_plan(
            route_i[:TOP_K], route_i[TOP_K:2 * TOP_K], counts[:, 0].astype(jnp.int32))
        y_rows = _experts(block_expert, n_used, dest, next_expert, seg_end, buf_len, h2,
                          w_exp_gate, w_exp_up, w_exp_down, layer)
        xf = _combine(dest * SLAB, x_new, gate_rows, y_rows)
    return xf.reshape(batch, seq, d)
```
